```python
import math, functools
import jax, jax.numpy as jnp
from jax import lax
import numpy as np

D_MODEL = 2048
BATCH = 8
SEQ = 2048
DEPTH = 2
DEC_BATCH = 128
DEC_SEQ = 8
PAST_LEN = 2048
PAGE_SIZE = 128

D_MIX = D_MODEL
DK_A = D_MODEL // 16
DV_A = 2 * DK_A
H_A = 4
W_A = H_A * DV_A
DH_B = D_MODEL // 16
H_B = 4
W_B = H_B * DH_B
W_C = D_MIX - W_A - W_B
NB_C = 4
BW_C = W_C // NB_C
CONV_W = 4
LRU_C = 8.0
ROPE_THETA = 500000.0
ROT_DIM = DK_A // 4
Q_BLOCK = 128
D_FF = 11 * D_MODEL // 4
N_EXPERTS = 8
TOP_K = 2
N_DENSE = (DEPTH + 1) // 2
N_MOE = DEPTH // 2
RMS_EPS = 1e-6
SUBLN_EPS = 1e-5

SPLIT_SIZES = (H_A * 2 * DK_A, H_A * 2 * DK_A, H_A * DV_A, W_B, W_B, W_B, H_B, W_C, W_C)
SPLIT_IDX = tuple(sum(SPLIT_SIZES[:i + 1]) for i in range(len(SPLIT_SIZES) - 1))
N_IN = sum(SPLIT_SIZES)

kernel_name = 'hymba_diff_fox_rglru_decode_step'


def rms_norm(x, g, eps=RMS_EPS):
    x32 = x.astype(jnp.float32)
    y = x32 * lax.rsqrt(jnp.mean(x32 * x32, axis=-1, keepdims=True) + eps)
    return (y * g.astype(jnp.float32)).astype(x.dtype)


def partial_rope(x, pos):
    half = ROT_DIM // 2
    inv = ROPE_THETA ** (-(jnp.arange(half, dtype=jnp.float32) * 2.0) / ROT_DIM)
    ang = pos.astype(jnp.float32)[:, None] * inv[None, :]
    shape = (pos.shape[0],) + (1,) * (x.ndim - 3) + (half,)
    cos = jnp.cos(ang).reshape(shape).astype(x.dtype)
    sin = jnp.sin(ang).reshape(shape).astype(x.dtype)
    x1, x2, rest = x[..., :half], x[..., half:ROT_DIM], x[..., ROT_DIM:]
    return jnp.concatenate([x1 * cos - x2 * sin, x1 * sin + x2 * cos, rest], axis=-1)


def per_token(fn, a, b):
    ba, ta, d = a.shape
    bb, tb, _ = b.shape
    out = fn(jnp.concatenate([a.reshape(ba * ta, d), b.reshape(bb * tb, d)], axis=0))
    return out[:ba * ta].reshape(ba, ta, -1), out[ba * ta:].reshape(bb, tb, -1)


def split_heads(proj, pos, b_f):
    B, T = proj.shape[:2]
    qa, ka, va, qb, kb, vb, fl, gc, xc = jnp.split(proj, SPLIT_IDX, axis=-1)
    qa = partial_rope(qa.reshape(B, T, H_A, 2, DK_A), pos)
    ka = partial_rope(ka.reshape(B, T, H_A, 2, DK_A), pos)
    va = va.reshape(B, T, H_A, DV_A)
    qb = qb.reshape(B, T, H_B, 1, DH_B)
    kb = kb.reshape(B, T, H_B, 1, DH_B)
    vb = vb.reshape(B, T, H_B, DH_B)
    logf = jax.nn.log_sigmoid((fl + b_f).astype(jnp.float32))
    return qa, ka, va, qb, kb, vb, logf, gc, xc


def diff_combine(p, lam):
    return p[:, :, 0] - lam * p[:, :, 1]


def fox_combine(p):
    return p[:, :, 0]


def prompt_attention(q, k, v, combine, c=None):
    B, S, H, M, Dk = q.shape
    nb = S // Q_BLOCK
    scale = Dk ** -0.5
    qb = jnp.moveaxis(q.reshape(B, nb, Q_BLOCK, H, M, Dk), 1, 0)
    kpos = jnp.arange(S)

    def block(args):
        i, qi = args
        start = i * Q_BLOCK
        s = jnp.einsum('bqhmd,bkhmd->bhmqk', qi, k).astype(jnp.float32) * scale
        if c is not None:
            cq = lax.dynamic_slice_in_dim(c, start, Q_BLOCK, axis=2)
            s = s + (cq[:, :, None, :, None] - c[:, :, None, None, :])
        qpos = start + jnp.arange(Q_BLOCK)
        s = jnp.where(kpos[None, :] <= qpos[:, None], s, -jnp.inf)
        w = combine(jax.nn.softmax(s, axis=-1))
        return jnp.einsum('bhqk,bkhd->bqhd', w, v).astype(v.dtype)

    out = lax.map(block, (jnp.arange(nb), qb))
    return jnp.moveaxis(out, 0, 1).reshape(B, S, H, v.shape[-1])


def sample_attention(q, k_past, v_past, k_new, v_new, combine, c_past=None, c_new=None):
    T, Dk = q.shape[1], q.shape[-1]
    P = k_past.shape[1]
    scale = Dk ** -0.5
    s_past = jnp.einsum('bqhmd,bkhmd->bhmqk', q, k_past).astype(jnp.float32) * scale
    s_new = jnp.einsum('bqhmd,bkhmd->bhmqk', q, k_new).astype(jnp.float32) * scale
    if c_new is not None:
        s_past = s_past + (c_new[:, :, None, :, None] - c_past[:, :, None, None, :])
        s_new = s_new + (c_new[:, :, None, :, None] - c_new[:, :, None, None, :])
    causal = jnp.tril(jnp.ones((T, T), dtype=bool))
    s_new = jnp.where(causal, s_new, -jnp.inf)
    w = combine(jax.nn.softmax(jnp.concatenate([s_past, s_new], axis=-1), axis=-1))
    out = (jnp.einsum('bhqk,bkhd->bqhd', w[..., :P], v_past)
           + jnp.einsum('bhqk,bkhd->bqhd', w[..., P:], v_new))
    return out.astype(v_new.dtype)


def rglru_branch(xc, gate, conv_prev, h0, cw, cb, wga, bga, wgx, bgx, lam_lru):
    B, T, W = xc.shape
    xp = jnp.concatenate([conv_prev.astype(xc.dtype), xc], axis=1)
    u = xp[:, :T] * cw[0]
    for k in range(1, CONV_W):
        u = u + xp[:, k:k + T] * cw[k]
    u = u + cb
    new_conv = xp[:, T:]
    ub = u.reshape(B, T, NB_C, BW_C)
    r = jax.nn.sigmoid(jnp.einsum('btnd,nde->btne', ub, wga).reshape(B, T, W) + bga)
    i = jax.nn.sigmoid(jnp.einsum('btnd,nde->btne', ub, wgx).reshape(B, T, W) + bgx)
    log_a = -LRU_C * r.astype(jnp.float32) * jax.nn.softplus(-lam_lru.astype(jnp.float32))
    a = jnp.exp(log_a)
    b = jnp.sqrt(-jnp.expm1(2.0 * log_a)) * (i * u).astype(jnp.float32)

    def step(h, inp):
        a_t, b_t = inp
        h = a_t * h + b_t
        return h, h

    hT, hs = lax.scan(step, h0.astype(jnp.float32), (jnp.swapaxes(a, 0, 1), jnp.swapaxes(b, 0, 1)))
    hs = jnp.swapaxes(hs, 0, 1)
    y = jax.nn.gelu(gate) * hs.astype(gate.dtype)
    return y, hT, new_conv


def swiglu(h, wg, wu, wd):
    return (jax.nn.silu(h @ wg) * (h @ wu)) @ wd


def moe_swiglu(h, w_r, wg, wu, wd):
    logits = (h @ w_r).astype(jnp.float32)
    top_v, top_i = lax.top_k(logits, TOP_K)
    gates = jax.nn.softmax(top_v, axis=-1)
    comb = jnp.sum(jax.nn.one_hot(top_i, N_EXPERTS, dtype=jnp.float32) * gates[..., None], axis=-2)
    out = jnp.zeros(h.shape, jnp.float32)
    for e in range(N_EXPERTS):
        out = out + comb[:, e:e + 1] * swiglu(h, wg[e], wu[e], wd[e])
    return out.astype(h.dtype)


def setup_inputs(seed: int = 0) -> dict:
    key = jax.random.key(seed)
    ks = iter(jax.random.split(key, 48))
    f32 = jnp.float32
    n_pages = PAST_LEN // PAGE_SIZE
    n_used = DEC_BATCH * n_pages
    n_pool = n_used + max(1, n_used // 4)

    def nrm(shape, scale):
        return jax.random.normal(next(ks), shape, f32) * scale

    x_prompt = nrm((BATCH, SEQ, D_MODEL), 1.0)
    x_sample = nrm((DEC_BATCH, DEC_SEQ, D_MODEL), 1.0)
    cache_k_diff = nrm((DEPTH, n_pool, PAGE_SIZE, H_A, 2, DK_A), 1.0)
    cache_v_diff = nrm((DEPTH, n_pool, PAGE_SIZE, H_A, DV_A), 1.0)
    cache_k_fox = nrm((DEPTH, n_pool, PAGE_SIZE, H_B, DH_B), 1.0)
    cache_v_fox = nrm((DEPTH, n_pool, PAGE_SIZE, H_B, DH_B), 1.0)
    cache_logf_fox = jax.nn.log_sigmoid(3.0 + nrm((DEPTH, n_pool, PAGE_SIZE, H_B), 1.0))
    state_rglru_h = nrm((DEPTH, DEC_BATCH, W_C), 0.5)
    state_rglru_conv = nrm((DEPTH, DEC_BATCH, CONV_W - 1, W_C), 1.0)
    page_table = jax.random.permutation(next(ks), n_pool)[:n_used].reshape(DEC_BATCH, n_pages).astype(jnp.int32)

    norm_mix_g = 1.0 + nrm((DEPTH, D_MODEL), 0.02)
    w_in = nrm((DEPTH, D_MODEL, N_IN), D_MODEL ** -0.5)
    b_fgate = 1.0 + 4.0 * jax.random.uniform(next(ks), (DEPTH, H_B), f32)
    lambda_q1 = nrm((DEPTH, DK_A), 0.1)
    lambda_k1 = nrm((DEPTH, DK_A), 0.1)
    lambda_q2 = nrm((DEPTH, DK_A), 0.1)
    lambda_k2 = nrm((DEPTH, DK_A), 0.1)
    subln_g = 1.0 + nrm((DEPTH, DV_A), 0.02)
    conv_w = nrm((DEPTH, CONV_W, W_C), CONV_W ** -0.5)
    conv_b = nrm((DEPTH, W_C), 0.01)
    w_gate_a = nrm((DEPTH, NB_C, BW_C, BW_C), BW_C ** -0.5)
    b_gate_a = nrm((DEPTH, W_C), 0.01)
    w_gate_x = nrm((DEPTH, NB_C, BW_C, BW_C), BW_C ** -0.5)
    b_gate_x = nrm((DEPTH, W_C), 0.01)
    u = jax.random.uniform(next(ks), (DEPTH, W_C), f32, 0.9, 0.999)
    a0 = u ** (1.0 / LRU_C)
    lru_lambda = jnp.log(a0) - jnp.log1p(-a0)
    w_out = nrm((DEPTH, D_MIX, D_MODEL), D_MIX ** -0.5)
    norm_ffn_g = 1.0 + nrm((DEPTH, D_MODEL), 0.02)
    w_dense_gate = nrm((N_DENSE, D_MODEL, D_FF), D_MODEL ** -0.5)
    w_dense_up = nrm((N_DENSE, D_MODEL, D_FF), D_MODEL ** -0.5)
    w_dense_down = nrm((N_DENSE, D_FF, D_MODEL), D_FF ** -0.5)
    w_router = nrm((N_MOE, D_MODEL, N_EXPERTS), D_MODEL ** -0.5)
    w_moe_gate = nrm((N_MOE, N_EXPERTS, D_MODEL, D_FF), D_MODEL ** -0.5)
    w_moe_up = nrm((N_MOE, N_EXPERTS, D_MODEL, D_FF), D_MODEL ** -0.5)
    w_moe_down = nrm((N_MOE, N_EXPERTS, D_FF, D_MODEL), D_FF ** -0.5)
    norm_final_g = 1.0 + nrm((D_MODEL,), 0.02)
    return {
        'x_prompt': x_prompt, 'x_sample': x_sample,
        'cache_k_diff': cache_k_diff, 'cache_v_diff': cache_v_diff,
        'cache_k_fox': cache_k_fox, 'cache_v_fox': cache_v_fox, 'cache_logf_fox': cache_logf_fox,
        'state_rglru_h': state_rglru_h, 'state_rglru_conv': state_rglru_conv,
        'page_table': page_table,
        'norm_mix_g': norm_mix_g, 'w_in': w_in, 'b_fgate': b_fgate,
        'lambda_q1': lambda_q1, 'lambda_k1': lambda_k1, 'lambda_q2': lambda_q2, 'lambda_k2': lambda_k2,
        'subln_g': subln_g, 'conv_w': conv_w, 'conv_b': conv_b,
        'w_gate_a': w_gate_a, 'b_gate_a': b_gate_a, 'w_gate_x': w_gate_x, 'b_gate_x': b_gate_x,
        'lru_lambda': lru_lambda, 'w_out': w_out, 'norm_ffn_g': norm_ffn_g,
        'w_dense_gate': w_dense_gate, 'w_dense_up': w_dense_up, 'w_dense_down': w_dense_down,
        'w_router': w_router, 'w_moe_gate': w_moe_gate, 'w_moe_up': w_moe_up, 'w_moe_down': w_moe_down,
        'norm_final_g': norm_final_g,
    }


def reference(x_prompt, x_sample, cache_k_diff, cache_v_diff, cache_k_fox, cache_v_fox, cache_logf_fox,
              state_rglru_h, state_rglru_conv, page_table,
              norm_mix_g, w_in, b_fgate, lambda_q1, lambda_k1, lambda_q2, lambda_k2, subln_g,
              conv_w, conv_b, w_gate_a, b_gate_a, w_gate_x, b_gate_x, lru_lambda, w_out, norm_ffn_g,
              w_dense_gate, w_dense_up, w_dense_down, w_router, w_moe_gate, w_moe_up, w_moe_down,
              norm_final_g):
    f32 = jnp.float32
    Bp, Sp, _ = x_prompt.shape
    Bs, Ts, _ = x_sample.shape
    n_past = page_table.shape[1] * cache_k_diff.shape[2]
    pos_p = jnp.arange(Sp)
    pos_s = n_past + jnp.arange(Ts)
    xp, xs = x_prompt, x_sample
    names = ('kd_p', 'vd_p', 'kf_p', 'vf_p', 'lf_p', 'h_p', 'cv_p',
             'kd_s', 'vd_s', 'kf_s', 'vf_s', 'lf_s', 'h_s', 'cv_s')
    st = {n: [] for n in names}

    for l in range(DEPTH):
        lam_init = 0.8 - 0.6 * math.exp(-0.3 * l)
        lam = (jnp.exp(jnp.sum(lambda_q1[l].astype(f32) * lambda_k1[l].astype(f32)))
               - jnp.exp(jnp.sum(lambda_q2[l].astype(f32) * lambda_k2[l].astype(f32))) + lam_init)
        comb_a = functools.partial(diff_combine, lam=lam)

        proj_p, proj_s = per_token(lambda t: rms_norm(t, norm_mix_g[l]) @ w_in[l], xp, xs)
        qa_p, ka_p, va_p, qb_p, kb_p, vb_p, lfb_p, gc_p, xc_p = split_heads(proj_p, pos_p, b_fgate[l])
        qa_s, ka_s, va_s, qb_s, kb_s, vb_s, lfb_s, gc_s, xc_s = split_heads(proj_s, pos_s, b_fgate[l])

        oa_p = prompt_attention(qa_p, ka_p, va_p, comb_a)
        c_p = jnp.swapaxes(jnp.cumsum(lfb_p, axis=1), 1, 2)
        ob_p = prompt_attention(qb_p, kb_p, vb_p, fox_combine, c_p)
        oc_p, hT_p, cv_p = rglru_branch(
            xc_p, gc_p, jnp.zeros((Bp, CONV_W - 1, W_C), xc_p.dtype), jnp.zeros((Bp, W_C), f32),
            conv_w[l], conv_b[l], w_gate_a[l], b_gate_a[l], w_gate_x[l], b_gate_x[l], lru_lambda[l])

        kd_past = cache_k_diff[l, page_table].reshape(Bs, n_past, H_A, 2, DK_A)
        vd_past = cache_v_diff[l, page_table].reshape(Bs, n_past, H_A, DV_A)
        oa_s = sample_attention(qa_s, kd_past, vd_past, ka_s, va_s, comb_a)
        kf_past = cache_k_fox[l, page_table].reshape(Bs, n_past, H_B, 1, DH_B)
        vf_past = cache_v_fox[l, page_table].reshape(Bs, n_past, H_B, DH_B)
        lf_past = cache_logf_fox[l, page_table].reshape(Bs, n_past, H_B).astype(f32)
        c_past = jnp.cumsum(lf_past, axis=1)
        c_new = c_past[:, -1:] + jnp.cumsum(lfb_s, axis=1)
        ob_s = sample_attention(qb_s, kf_past, vf_past, kb_s, vb_s, fox_combine,
                                jnp.swapaxes(c_past, 1, 2), jnp.swapaxes(c_new, 1, 2))
        oc_s, hT_s, cv_s = rglru_branch(
            xc_s, gc_s, state_rglru_conv[l], state_rglru_h[l],
            conv_w[l], conv_b[l], w_gate_a[l], b_gate_a[l], w_gate_x[l], b_gate_x[l], lru_lambda[l])

        oa_p = (rms_norm(oa_p, subln_g[l], SUBLN_EPS) * (1.0 - lam_init)).reshape(Bp, Sp, W_A)
        oa_s = (rms_norm(oa_s, subln_g[l], SUBLN_EPS) * (1.0 - lam_init)).reshape(Bs, Ts, W_A)
        mix_p = jnp.concatenate([oa_p, ob_p.reshape(Bp, Sp, W_B), oc_p], axis=-1)
        mix_s = jnp.concatenate([oa_s, ob_s.reshape(Bs, Ts, W_B), oc_s], axis=-1)
        mo_p, mo_s = per_token(lambda t: t @ w_out[l], mix_p, mix_s)
        xp = xp + mo_p
        xs = xs + mo_s

        if l % 2 == 0:
            m = l // 2
            ffn = lambda t: swiglu(rms_norm(t, norm_ffn_g[l]), w_dense_gate[m], w_dense_up[m], w_dense_down[m])
        else:
            m = l // 2
            ffn = lambda t: moe_swiglu(rms_norm(t, norm_ffn_g[l]), w_router[m], w_moe_gate[m], w_moe_up[m], w_moe_down[m])
        fo_p, fo_s = per_token(ffn, xp, xs)
        xp = xp + fo_p
        xs = xs + fo_s

        st['kd_p'].append(ka_p.astype(cache_k_diff.dtype))
        st['vd_p'].append(va_p.astype(cache_v_diff.dtype))
        st['kf_p'].append(kb_p[:, :, :, 0].astype(cache_k_fox.dtype))
        st['vf_p'].append(vb_p.astype(cache_v_fox.dtype))
        st['lf_p'].append(lfb_p.astype(cache_logf_fox.dtype))
        st['h_p'].append(hT_p.astype(state_rglru_h.dtype))
        st['cv_p'].append(cv_p.astype(state_rglru_conv.dtype))
        st['kd_s'].append(ka_s.astype(cache_k_diff.dtype))
        st['vd_s'].append(va_s.astype(cache_v_diff.dtype))
        st['kf_s'].append(kb_s[:, :, :, 0].astype(cache_k_fox.dtype))
        st['vf_s'].append(vb_s.astype(cache_v_fox.dtype))
        st['lf_s'].append(lfb_s.astype(cache_logf_fox.dtype))
        st['h_s'].append(hT_s.astype(state_rglru_h.dtype))
        st['cv_s'].append(cv_s.astype(state_rglru_conv.dtype))

    y_prompt = rms_norm(xp, norm_final_g)
    y_sample = rms_norm(xs, norm_final_g)
    return (y_prompt, y_sample,
            jnp.stack(st['kd_p']), jnp.stack(st['vd_p']), jnp.stack(st['kf_p']), jnp.stack(st['vf_p']),
            jnp.stack(st['lf_p']), jnp.stack(st['h_p']), jnp.stack(st['cv_p']),
            jnp.stack(st['kd_s']), jnp.stack(st['vd_s']), jnp.stack(st['kf_s']), jnp.stack(st['vf_s']),
            jnp.stack(st['lf_s']), jnp.stack(st['h_s']), jnp.stack(st['cv_s']))
```

```python
import functools
import math

import jax
import jax.numpy as jnp
from jax import lax
from jax.experimental import pallas as pl
from jax.experimental.pallas import tpu as pltpu

F32 = jnp.float32
BF16 = jnp.bfloat16

LANES = 128
SUBLANES = 8
VMEM_LIMIT = 56 * 1024 * 1024

TM_TOKENS = 1024
TN_PROJ = 512
TQ_ATTN = 512
TT_LRU = 256
TM_MIX = 512
TF_FFN = 256
TM_MOE = 512
TC_COMBINE = 256

RMS_EPS = 1e-6
SUBLN_EPS = 1e-5
LRU_C = 8.0
ROPE_THETA = 500000.0
TOP_K = 2
NEG_INF = float("-inf")


def _params(sem, vmem=VMEM_LIMIT):
    return pltpu.CompilerParams(dimension_semantics=sem, vmem_limit_bytes=vmem)


def _split3(x):
    hi = x.astype(BF16)
    r = x - hi.astype(F32)
    mid = r.astype(BF16)
    lo = (r - mid.astype(F32)).astype(BF16)
    return hi, mid, lo


def _dot_sel(x, m):
    hi, mid, lo = _split3(x)
    d = lambda a: jnp.dot(a, m, preferred_element_type=F32)
    return d(hi) + d(mid) + d(lo)


def _sel_dot(m, x):
    hi, mid, lo = _split3(x)
    d = lambda a: jnp.dot(m, a, preferred_element_type=F32)
    return d(hi) + d(mid) + d(lo)


def _dot_nt(a, b):
    return lax.dot_general(a, b, (((1,), (1,)), ((), ())), preferred_element_type=F32)


def _sigmoid(x):
    return 1.0 / (1.0 + jnp.exp(-x))


def _softplus(x):
    return jnp.maximum(x, 0.0) + jnp.log1p(jnp.exp(-jnp.abs(x)))


def _rms(x, g, eps):
    ms = jnp.mean(x * x, axis=-1, keepdims=True)
    return x * lax.rsqrt(ms + eps) * g


def _proj_kernel(x_ref, g_ref, w_ref, wfl_ref, bf_ref, cos_ref, sa_ref, sb_ref,
                 proj_ref, logf_ref, hn_ref, *, n_rope_tiles, tn, rot_half):
    j = pl.program_id(1)

    @pl.when(j == 0)
    def _():
        hn = _rms(x_ref[...], g_ref[...], RMS_EPS).astype(BF16)
        hn_ref[...] = hn
        z = jnp.dot(hn, wfl_ref[...], preferred_element_type=F32) + bf_ref[...]
        logf_ref[...] = jnp.minimum(z, 0.0) - jnp.log1p(jnp.exp(-jnp.abs(z)))

    y = jnp.dot(hn_ref[...], w_ref[...], preferred_element_type=F32)

    @pl.when(j < n_rope_tiles)
    def _():
        c, sa, sb = cos_ref[...], sa_ref[...], sb_ref[...]
        for k in range(tn // LANES):
            yc = y[:, k * LANES:(k + 1) * LANES]
            proj_ref[:, k * LANES:(k + 1) * LANES] = (
                yc * c + pltpu.roll(yc, rot_half, 1) * sa + pltpu.roll(yc, LANES - rot_half, 1) * sb)

    @pl.when(j >= n_rope_tiles)
    def _():
        proj_ref[...] = y


def _proj_call(x, g, w, wfl, bfl, rope, *, tm, tn, n_rope_cols, n_prompt_tiles, rope_prompt_tiles, rot_half):
    n, d = x.shape
    n_main = w.shape[1]
    cos_t, sa_t, sb_t = rope
    rope_map = lambda i, j: (jnp.where(i < n_prompt_tiles, i % rope_prompt_tiles,
                                       rope_prompt_tiles + i - n_prompt_tiles), 0)
    kern = functools.partial(_proj_kernel, n_rope_tiles=n_rope_cols // tn, tn=tn, rot_half=rot_half)
    return pl.pallas_call(
        kern,
        grid=(n // tm, n_main // tn),
        in_specs=[
            pl.BlockSpec((tm, d), lambda i, j: (i, 0)),
            pl.BlockSpec((1, d), lambda i, j: (0, 0)),
            pl.BlockSpec((d, tn), lambda i, j: (0, j)),
            pl.BlockSpec((d, LANES), lambda i, j: (0, 0)),
            pl.BlockSpec((1, LANES), lambda i, j: (0, 0)),
            pl.BlockSpec((tm, LANES), rope_map),
            pl.BlockSpec((tm, LANES), rope_map),
            pl.BlockSpec((tm, LANES), rope_map),
        ],
        out_specs=[
            pl.BlockSpec((tm, tn), lambda i, j: (i, j)),
            pl.BlockSpec((tm, LANES), lambda i, j: (i, 0)),
        ],
        out_shape=[jax.ShapeDtypeStruct((n, n_main), F32), jax.ShapeDtypeStruct((n, LANES), F32)],
        scratch_shapes=[pltpu.VMEM((tm, d), BF16)],
        compiler_params=_params(("parallel", "arbitrary")),
        name="proj",
    )(x, g, w, wfl, bfl, cos_t, sa_t, sb_t)


def _cumsum_kernel(x_ref, tri_ref, o_ref):
    o_ref[...] = _sel_dot(tri_ref[...], x_ref[...])


def _cumsum_call(logf_p, tri, batch, seq):
    return pl.pallas_call(
        _cumsum_kernel,
        grid=(batch,),
        in_specs=[pl.BlockSpec((seq, LANES), lambda b: (b, 0)),
                  pl.BlockSpec((seq, seq), lambda b: (0, 0))],
        out_specs=pl.BlockSpec((seq, LANES), lambda b: (b, 0)),
        out_shape=jax.ShapeDtypeStruct((batch * seq, LANES), F32),
        compiler_params=_params(("parallel",)),
        name="cumsum_prompt",
    )(logf_p, tri)


def _softmax_step(s, v, m_ref, l_ref, acc_ref, idx):
    m_old = m_ref[idx]
    m_new = jnp.maximum(m_old, jnp.max(s, axis=-1, keepdims=True))
    p = jnp.exp(s - m_new)
    alpha = jnp.exp(m_old - m_new)
    l_ref[idx] = alpha * l_ref[idx] + jnp.sum(p, axis=-1, keepdims=True)
    acc_ref[idx] = alpha * acc_ref[idx] + jnp.dot(p.astype(BF16), v, preferred_element_type=F32)
    m_ref[idx] = m_new


def _diff_lambda(lq1, lk1, lq2, lk2, lam_init):
    return (jnp.exp(jnp.sum(lq1[...] * lk1[...], axis=-1, keepdims=True))
            - jnp.exp(jnp.sum(lq2[...] * lk2[...], axis=-1, keepdims=True)) + lam_init)


def _diff_finish(acc_ref, l_ref, base, lam, sg, lam_init):
    o = acc_ref[base] / l_ref[base] - lam * (acc_ref[base + 1] / l_ref[base + 1])
    return _rms(o, sg, SUBLN_EPS) * (1.0 - lam_init)


def _diff_prompt_kernel(q_ref, k_ref, v_ref, lq1, lk1, lq2, lk2, sg_ref, o_ref,
                        kb_ref, vb_ref, m_ref, l_ref, acc_ref, *, tq, dk, lam_init, scale):
    qi = pl.program_id(2)

    @pl.when(qi == 0)
    def _():
        kb_ref[...] = k_ref[...].astype(BF16)
        vb_ref[...] = v_ref[...].astype(BF16)

    m_ref[...] = jnp.full(m_ref.shape, NEG_INF, F32)
    l_ref[...] = jnp.zeros(l_ref.shape, F32)
    acc_ref[...] = jnp.zeros(acc_ref.shape, F32)
    q = q_ref[...].astype(BF16)

    def block(kk, causal):
        k0 = pl.multiple_of(kk * tq, tq)
        kblk = kb_ref[pl.ds(k0, tq), :]
        vblk = vb_ref[pl.ds(k0, tq), :]
        for m in range(2):
            s = _dot_nt(q[:, m * dk:(m + 1) * dk], kblk[:, m * dk:(m + 1) * dk]) * scale
            if causal:
                row = lax.broadcasted_iota(jnp.int32, s.shape, 0)
                col = lax.broadcasted_iota(jnp.int32, s.shape, 1)
                s = jnp.where(col <= row, s, NEG_INF)
            _softmax_step(s, vblk, m_ref, l_ref, acc_ref, m)

    def body(kk, carry):
        block(kk, False)
        return carry

    lax.fori_loop(0, qi, body, 0)
    block(qi, True)

    lam = _diff_lambda(lq1, lk1, lq2, lk2, lam_init)
    o_ref[...] = _diff_finish(acc_ref, l_ref, 0, lam, sg_ref[...], lam_init).astype(o_ref.dtype)


def _diff_prompt_call(proj, lams, sg, *, batch, seq, heads, dk, dv, tq, k_col, v_col, lam_init):
    nq = seq // tq
    n_rows = batch * seq
    lam_spec = pl.BlockSpec((1, dk), lambda b, h, q: (0, 0))
    kern = functools.partial(_diff_prompt_kernel, tq=tq, dk=dk, lam_init=lam_init, scale=dk ** -0.5)
    return pl.pallas_call(
        kern,
        grid=(batch, heads, nq),
        in_specs=[
            pl.BlockSpec((tq, 2 * dk), lambda b, h, q: (b * nq + q, h)),
            pl.BlockSpec((seq, 2 * dk), lambda b, h, q: (b, k_col // (2 * dk) + h)),
            pl.BlockSpec((seq, dv), lambda b, h, q: (b, v_col // dv + h)),
            lam_spec, lam_spec, lam_spec, lam_spec,
            pl.BlockSpec((1, dv), lambda b, h, q: (0, 0)),
        ],
        out_specs=pl.BlockSpec((tq, dv), lambda b, h, q: (b * nq + q, h)),
        out_shape=jax.ShapeDtypeStruct((n_rows, heads * dv), BF16),
        scratch_shapes=[
            pltpu.VMEM((seq, 2 * dk), BF16), pltpu.VMEM((seq, dv), BF16),
            pltpu.VMEM((2, tq, 1), F32), pltpu.VMEM((2, tq, 1), F32), pltpu.VMEM((2, tq, dv), F32),
        ],
        compiler_params=_params(("parallel", "parallel", "arbitrary")),
        name="diff_prompt",
    )(proj, proj, proj, *lams, sg)


def _fox_prompt_kernel(q_ref, k_ref, v_ref, cc_ref, cr_ref, o_ref,
                       kb_ref, vb_ref, m_ref, l_ref, acc_ref, *, tq, scale):
    h = pl.program_id(1)
    qi = pl.program_id(2)

    @pl.when(qi == 0)
    def _():
        kb_ref[...] = k_ref[...].astype(BF16)
        vb_ref[...] = v_ref[...].astype(BF16)

    m_ref[...] = jnp.full(m_ref.shape, NEG_INF, F32)
    l_ref[...] = jnp.zeros(l_ref.shape, F32)
    acc_ref[...] = jnp.zeros(acc_ref.shape, F32)
    q = q_ref[...].astype(BF16)
    cc = cc_ref[...]
    lane = lax.broadcasted_iota(jnp.int32, cc.shape, 1)
    cq = jnp.sum(jnp.where(lane == h, cc, 0.0), axis=-1, keepdims=True)

    def block(kk, causal):
        k0 = pl.multiple_of(kk * tq, tq)
        ck = cr_ref[0, pl.ds(kk, 1), :]
        s = _dot_nt(q, kb_ref[pl.ds(k0, tq), :]) * scale + (cq - ck)
        if causal:
            row = lax.broadcasted_iota(jnp.int32, s.shape, 0)
            col = lax.broadcasted_iota(jnp.int32, s.shape, 1)
            s = jnp.where(col <= row, s, NEG_INF)
        _softmax_step(s, vb_ref[pl.ds(k0, tq), :], m_ref, l_ref, acc_ref, 0)

    def body(kk, carry):
        block(kk, False)
        return carry

    lax.fori_loop(0, qi, body, 0)
    block(qi, True)
    o_ref[...] = (acc_ref[0] / l_ref[0]).astype(o_ref.dtype)


def _fox_prompt_call(proj, c_col, c_row, *, batch, seq, heads, dh, tq, q_col, k_col, v_col):
    nq = seq // tq
    kern = functools.partial(_fox_prompt_kernel, tq=tq, scale=dh ** -0.5)
    return pl.pallas_call(
        kern,
        grid=(batch, heads, nq),
        in_specs=[
            pl.BlockSpec((tq, dh), lambda b, h, q: (b * nq + q, q_col // dh + h)),
            pl.BlockSpec((seq, dh), lambda b, h, q: (b, k_col // dh + h)),
            pl.BlockSpec((seq, dh), lambda b, h, q: (b, v_col // dh + h)),
            pl.BlockSpec((tq, LANES), lambda b, h, q: (b * nq + q, 0)),
            pl.BlockSpec((1, nq, tq), lambda b, h, q: (b * heads + h, 0, 0)),
        ],
        out_specs=pl.BlockSpec((tq, dh), lambda b, h, q: (b * nq + q, h)),
        out_shape=jax.ShapeDtypeStruct((batch * seq, heads * dh), BF16),
        scratch_shapes=[
            pltpu.VMEM((seq, dh), BF16), pltpu.VMEM((seq, dh), BF16),
            pltpu.VMEM((1, tq, 1), F32), pltpu.VMEM((1, tq, 1), F32), pltpu.VMEM((1, tq, dh), F32),
        ],
        compiler_params=_params(("parallel", "parallel", "arbitrary")),
        name="fox_prompt",
    )(proj, proj, proj, c_col, c_row)


def _pad_rows(x, rows):
    return jnp.concatenate([x, jnp.zeros((rows - x.shape[0], x.shape[1]), x.dtype)], axis=0)


def _diff_sample_kernel(pt_ref, q_ref, kn_ref, vn_ref, kc_ref, vc_ref, lq1, lk1, lq2, lk2, sg_ref,
                        o_ref, m_ref, l_ref, acc_ref, *, heads, dk, dv, page, lam_init, scale):
    del pt_ref
    p = pl.program_id(1)
    last = pl.num_programs(1) - 1

    @pl.when(p == 0)
    def _():
        m_ref[...] = jnp.full(m_ref.shape, NEG_INF, F32)
        l_ref[...] = jnp.zeros(l_ref.shape, F32)
        acc_ref[...] = jnp.zeros(acc_ref.shape, F32)

    q = q_ref[...]
    nc = dv // LANES
    for h in range(heads):
        v_h = jnp.concatenate([vc_ref[pl.ds(c * heads + h, page, stride=heads * nc), :] for c in range(nc)],
                              axis=-1).astype(BF16)
        for m in range(2):
            hm = 2 * h + m
            k_hm = kc_ref[pl.ds(hm, page, stride=2 * heads), :].astype(BF16)
            s = _dot_nt(q[:, hm * dk:(hm + 1) * dk].astype(BF16), k_hm) * scale
            _softmax_step(s, v_h, m_ref, l_ref, acc_ref, hm)

    @pl.when(p == last)
    def _():
        lam = _diff_lambda(lq1, lk1, lq2, lk2, lam_init)
        kn = kn_ref[...]
        vn = vn_ref[...]
        for h in range(heads):
            v_h = _pad_rows(vn[:, h * dv:(h + 1) * dv], page).astype(BF16)
            for m in range(2):
                hm = 2 * h + m
                k_hm = _pad_rows(kn[:, hm * dk:(hm + 1) * dk], page).astype(BF16)
                s = _dot_nt(q[:, hm * dk:(hm + 1) * dk].astype(BF16), k_hm) * scale
                row = lax.broadcasted_iota(jnp.int32, s.shape, 0)
                col = lax.broadcasted_iota(jnp.int32, s.shape, 1)
                s = jnp.where(col <= row, s, NEG_INF)
                _softmax_step(s, v_h, m_ref, l_ref, acc_ref, hm)
            o_ref[:, h * dv:(h + 1) * dv] = _diff_finish(acc_ref, l_ref, 2 * h, lam, sg_ref[...], lam_init)


def _diff_sample_call(page_table, proj, kc, vc, lams, sg, layer, *, n_prompt, t_new, heads, dk, dv,
                      page, k_col, v_col, lam_init):
    bs, n_pages = page_table.shape
    w_q = heads * 2 * dk
    row0 = n_prompt // t_new
    lam_spec = pl.BlockSpec((1, dk), lambda b, p, pt: (0, 0))
    kern = functools.partial(_diff_sample_kernel, heads=heads, dk=dk, dv=dv, page=page,
                             lam_init=lam_init, scale=dk ** -0.5)
    grid_spec = pltpu.PrefetchScalarGridSpec(
        num_scalar_prefetch=1,
        grid=(bs, n_pages),
        in_specs=[
            pl.BlockSpec((t_new, w_q), lambda b, p, pt: (row0 + b, 0)),
            pl.BlockSpec((t_new, w_q), lambda b, p, pt: (row0 + b, k_col // w_q)),
            pl.BlockSpec((t_new, heads * dv), lambda b, p, pt: (row0 + b, v_col // (heads * dv))),
            pl.BlockSpec((None, None, page * heads * 2, dk), lambda b, p, pt: (layer, pt[b, p], 0, 0)),
            pl.BlockSpec((None, None, page * heads * dv // LANES, LANES), lambda b, p, pt: (layer, pt[b, p], 0, 0)),
            lam_spec, lam_spec, lam_spec, lam_spec,
            pl.BlockSpec((1, dv), lambda b, p, pt: (0, 0)),
        ],
        out_specs=pl.BlockSpec((t_new, heads * dv), lambda b, p, pt: (b, 0)),
        scratch_shapes=[
            pltpu.VMEM((2 * heads, t_new, 1), F32), pltpu.VMEM((2 * heads, t_new, 1), F32),
            pltpu.VMEM((2 * heads, t_new, dv), F32),
        ],
    )
    return pl.pallas_call(
        kern,
        grid_spec=grid_spec,
        out_shape=jax.ShapeDtypeStruct((bs * t_new, heads * dv), F32),
        compiler_params=_params(("parallel", "arbitrary")),
        name="diff_sample",
    )(page_table, proj, proj, proj, kc, vc, *lams, sg)


def _logf_pages_kernel(pt_ref, lf_ref, mcs_ref, mbc_ref, msel_ref, lst_ref, cp_ref, ct_ref, g_ref,
                       *, n_pages, heads, page):
    b = pl.program_id(0)
    for p in range(n_pages):
        g_ref[pl.ds(p, 1), :] = lf_ref[pl.ds(pt_ref[b, p], 1), :]
    x = g_ref[...]
    y = _dot_sel(x, mcs_ref[...])
    tot = _dot_sel(y, mbc_ref[...])
    c = y + _sel_dot(lst_ref[...], tot)
    for h in range(heads):
        cp_ref[0, h * n_pages:(h + 1) * n_pages, :] = c[:, h * page:(h + 1) * page]
    ct_ref[0] = _dot_sel(c, msel_ref[...])[n_pages - 1:n_pages, :]


def _logf_pages_call(page_table, lf2, mats, *, heads, page):
    bs, n_pages = page_table.shape
    n_pool, w = lf2.shape
    mcs, mbc, msel, lst = mats
    full = lambda a: pl.BlockSpec(a.shape, lambda b, pt: (0,) * a.ndim)
    kern = functools.partial(_logf_pages_kernel, n_pages=n_pages, heads=heads, page=page)
    grid_spec = pltpu.PrefetchScalarGridSpec(
        num_scalar_prefetch=1,
        grid=(bs,),
        in_specs=[full(lf2), full(mcs), full(mbc), full(msel), full(lst)],
        out_specs=[pl.BlockSpec((1, heads * n_pages, page), lambda b, pt: (b, 0, 0)),
                   pl.BlockSpec((1, 1, LANES), lambda b, pt: (b, 0, 0))],
        scratch_shapes=[pltpu.VMEM((n_pages, w), F32)],
    )
    return pl.pallas_call(
        kern,
        grid_spec=grid_spec,
        out_shape=[jax.ShapeDtypeStruct((bs, heads * n_pages, page), F32),
                   jax.ShapeDtypeStruct((bs, 1, LANES), F32)],
        compiler_params=_params(("arbitrary",)),
        name="logf_pages",
    )(page_table, lf2, mcs, mbc, msel, lst)


def _fox_sample_kernel(pt_ref, q_ref, kn_ref, vn_ref, lfn_ref, kc_ref, vc_ref, cp_ref, ct_ref,
                       o_ref, m_ref, l_ref, acc_ref, cn_ref, *, heads, dh, page, n_pages, scale):
    del pt_ref
    p = pl.program_id(1)
    last = pl.num_programs(1) - 1

    @pl.when(p == 0)
    def _():
        m_ref[...] = jnp.full(m_ref.shape, NEG_INF, F32)
        l_ref[...] = jnp.zeros(l_ref.shape, F32)
        acc_ref[...] = jnp.zeros(acc_ref.shape, F32)
        x = lfn_ref[...]
        row = lax.broadcasted_iota(jnp.int32, x.shape, 0)
        k = 1
        while k < x.shape[0]:
            x = x + jnp.where(row >= k, pltpu.roll(x, k, 0), 0.0)
            k *= 2
        cn_ref[...] = ct_ref[0] + x

    q = q_ref[...]
    cn = cn_ref[...]
    for h in range(heads):
        k_h = kc_ref[pl.ds(h, page, stride=heads), :].astype(BF16)
        v_h = vc_ref[pl.ds(h, page, stride=heads), :].astype(BF16)
        ck = cp_ref[0, pl.ds(h * n_pages + p, 1), :]
        s = _dot_nt(q[:, h * dh:(h + 1) * dh].astype(BF16), k_h) * scale + (cn[:, h:h + 1] - ck)
        _softmax_step(s, v_h, m_ref, l_ref, acc_ref, h)

    @pl.when(p == last)
    def _():
        kn = kn_ref[...]
        vn = vn_ref[...]
        for h in range(heads):
            k_h = _pad_rows(kn[:, h * dh:(h + 1) * dh], page).astype(BF16)
            v_h = _pad_rows(vn[:, h * dh:(h + 1) * dh], page).astype(BF16)
            s = _dot_nt(q[:, h * dh:(h + 1) * dh].astype(BF16), k_h) * scale
            row = lax.broadcasted_iota(jnp.int32, s.shape, 0)
            col = lax.broadcasted_iota(jnp.int32, s.shape, 1)
            cq = cn[:, h:h + 1]
            ckn = jnp.sum(jnp.where(row == col, cq, 0.0), axis=0, keepdims=True)
            s = jnp.where(col <= row, s + (cq - ckn), NEG_INF)
            _softmax_step(s, v_h, m_ref, l_ref, acc_ref, h)
            o_ref[:, h * dh:(h + 1) * dh] = acc_ref[h] / l_ref[h]


def _fox_sample_call(page_table, proj, logf, kc, vc, cpast, ctot, layer, *, n_prompt, t_new, heads, dh,
                     page, q_col, k_col, v_col):
    bs, n_pages = page_table.shape
    w = heads * dh
    row0 = n_prompt // t_new
    kern = functools.partial(_fox_sample_kernel, heads=heads, dh=dh, page=page, n_pages=n_pages,
                             scale=dh ** -0.5)
    grid_spec = pltpu.PrefetchScalarGridSpec(
        num_scalar_prefetch=1,
        grid=(bs, n_pages),
        in_specs=[
            pl.BlockSpec((t_new, w), lambda b, p, pt: (row0 + b, q_col // w)),
            pl.BlockSpec((t_new, w), lambda b, p, pt: (row0 + b, k_col // w)),
            pl.BlockSpec((t_new, w), lambda b, p, pt: (row0 + b, v_col // w)),
            pl.BlockSpec((t_new, LANES), lambda b, p, pt: (row0 + b, 0)),
            pl.BlockSpec((None, None, page * heads, dh), lambda b, p, pt: (layer, pt[b, p], 0, 0)),
            pl.BlockSpec((None, None, page * heads, dh), lambda b, p, pt: (layer, pt[b, p], 0, 0)),
            pl.BlockSpec((1, heads * n_pages, page), lambda b, p, pt: (b, 0, 0)),
            pl.BlockSpec((1, 1, LANES), lambda b, p, pt: (b, 0, 0)),
        ],
        out_specs=pl.BlockSpec((t_new, w), lambda b, p, pt: (b, 0)),
        scratch_shapes=[
            pltpu.VMEM((heads, t_new, 1), F32), pltpu.VMEM((heads, t_new, 1), F32),
            pltpu.VMEM((heads, t_new, dh), F32), pltpu.VMEM((t_new, LANES), F32),
        ],
    )
    return pl.pallas_call(
        kern,
        grid_spec=grid_spec,
        out_shape=jax.ShapeDtypeStruct((bs * t_new, w), F32),
        compiler_params=_params(("parallel", "arbitrary")),
        name="fox_sample",
    )(page_table, proj, proj, proj, logf, kc, vc, cpast, ctot)


def _lru_coeffs(x, xprev_ext, gate_w, cw_ref, cb_ref, wga_ref, bga_ref, wgx_ref, bgx_ref, lam_ref, conv_w):
    rows = x.shape[0]
    nb = wga_ref.shape[0]
    bw = wga_ref.shape[1]
    u = xprev_ext(conv_w - 1) * cw_ref[0:1, :]
    for k in range(1, conv_w - 1):
        u = u + xprev_ext(conv_w - 1 - k) * cw_ref[k:k + 1, :]
    u = u + x * cw_ref[conv_w - 1:conv_w, :]
    u = u + cb_ref[...]
    ub = u.astype(BF16)
    r_parts, i_parts = [], []
    for n in range(nb):
        un = ub[:, n * bw:(n + 1) * bw]
        r_parts.append(jnp.dot(un, wga_ref[n], preferred_element_type=F32))
        i_parts.append(jnp.dot(un, wgx_ref[n], preferred_element_type=F32))
    r = _sigmoid(jnp.concatenate(r_parts, axis=-1) + bga_ref[...])
    i = _sigmoid(jnp.concatenate(i_parts, axis=-1) + bgx_ref[...])
    log_a = -LRU_C * r * _softplus(-lam_ref[...])
    a = jnp.exp(log_a)
    b = jnp.sqrt(-jnp.tanh(log_a) * (a * a + 1.0)) * (i * u)
    del rows, gate_w
    return a, b


def _group_scan(a, b):
    row = lax.broadcasted_iota(jnp.int32, a.shape, 0) % SUBLANES
    k = 1
    while k < SUBLANES:
        keep = row >= k
        a_sh = jnp.where(keep, pltpu.roll(a, k, 0), 1.0)
        b_sh = jnp.where(keep, pltpu.roll(b, k, 0), 0.0)
        b = a * b_sh + b
        a = a * a_sh
        k *= 2
    return a, b


def _gelu_tanh(x):
    return 0.5 * x * (1.0 + jnp.tanh(math.sqrt(2.0 / math.pi) * (x + 0.044715 * (x * x * x))))


def _lru_prompt_kernel(x_ref, g_ref, cw_ref, cb_ref, wga_ref, bga_ref, wgx_ref, bgx_ref, lam_ref,
                       y_ref, ht_ref, tail_ref, hl_ref, a_ref, b_ref, hs_ref, *, conv_w):
    ti = pl.program_id(1)

    @pl.when(ti == 0)
    def _():
        tail_ref[...] = jnp.zeros(tail_ref.shape, F32)
        hl_ref[...] = jnp.zeros(hl_ref.shape, F32)

    x = x_ref[...]
    tt = x.shape[0]
    xe = jnp.concatenate([tail_ref[...], x], axis=0)
    delayed = lambda k: pltpu.roll(xe, k, 0)[SUBLANES:, :]
    a, b = _lru_coeffs(x, delayed, None, cw_ref, cb_ref, wga_ref, bga_ref, wgx_ref, bgx_ref, lam_ref, conv_w)
    a, b = _group_scan(a, b)
    a_ref[...] = a
    b_ref[...] = b
    tail_ref[...] = x[tt - SUBLANES:, :]

    def body(g, hlast):
        r0 = pl.multiple_of(g * SUBLANES, SUBLANES)
        hg = b_ref[pl.ds(r0, SUBLANES), :] + a_ref[pl.ds(r0, SUBLANES), :] * hlast
        hs_ref[pl.ds(r0, SUBLANES), :] = hg
        return hg[SUBLANES - 1:SUBLANES, :]

    hl = lax.fori_loop(0, tt // SUBLANES, body, hl_ref[...])
    hl_ref[...] = hl
    ht_ref[0] = hl
    y_ref[...] = (_gelu_tanh(g_ref[...]) * hs_ref[...]).astype(y_ref.dtype)


def _lru_prompt_call(proj, wts, *, batch, seq, wc, tt, g_col, x_col):
    cw, cb, wga, bga, wgx, bgx, lam = wts
    nt = seq // tt
    conv_w = cw.shape[0]
    full = lambda a: pl.BlockSpec(a.shape, lambda b, t: (0,) * a.ndim)
    kern = functools.partial(_lru_prompt_kernel, conv_w=conv_w)
    return pl.pallas_call(
        kern,
        grid=(batch, nt),
        in_specs=[
            pl.BlockSpec((tt, wc), lambda b, t: (b * nt + t, x_col // wc)),
            pl.BlockSpec((tt, wc), lambda b, t: (b * nt + t, g_col // wc)),
            full(cw), full(cb), full(wga), full(bga), full(wgx), full(bgx), full(lam),
        ],
        out_specs=[pl.BlockSpec((tt, wc), lambda b, t: (b * nt + t, 0)),
                   pl.BlockSpec((1, 1, wc), lambda b, t: (b, 0, 0))],
        out_shape=[jax.ShapeDtypeStruct((batch * seq, wc), BF16),
                   jax.ShapeDtypeStruct((batch, 1, wc), F32)],
        scratch_shapes=[pltpu.VMEM((SUBLANES, wc), F32), pltpu.VMEM((1, wc), F32),
                        pltpu.VMEM((tt, wc), F32), pltpu.VMEM((tt, wc), F32), pltpu.VMEM((tt, wc), F32)],
        compiler_params=_params(("parallel", "arbitrary")),
        name="lru_prompt",
    )(proj, proj, cw, cb, wga, bga, wgx, bgx, lam)


def _lru_sample_kernel(x_ref, g_ref, prev_ref, h0_ref, cw_ref, cb_ref, wga_ref, bga_ref, wgx_ref, bgx_ref,
                       lam_ref, y_ref, hs_ref, *, conv_w):
    x = x_ref[...]
    rows = x.shape[0]
    row = lax.broadcasted_iota(jnp.int32, x.shape, 0) % SUBLANES
    prev = prev_ref[...]

    def delayed(k):
        return jnp.where(row >= k, pltpu.roll(x, k, 0), pltpu.roll(prev, (k - SUBLANES) % rows, 0))

    a, b = _lru_coeffs(x, delayed, None, cw_ref, cb_ref, wga_ref, bga_ref, wgx_ref, bgx_ref, lam_ref, conv_w)
    a, b = _group_scan(a, b)
    hs = b + a * h0_ref[...]
    hs_ref[...] = hs
    y_ref[...] = _gelu_tanh(g_ref[...]) * hs


def _lru_sample_call(proj, prev8, h0rep, wts, *, n_prompt, n_sample, wc, g_col, x_col):
    cw, cb, wga, bga, wgx, bgx, lam = wts
    conv_w = cw.shape[0]
    rb = n_prompt // n_sample
    full = lambda a: pl.BlockSpec(a.shape, lambda i: (0,) * a.ndim)
    kern = functools.partial(_lru_sample_kernel, conv_w=conv_w)
    return pl.pallas_call(
        kern,
        grid=(1,),
        in_specs=[
            pl.BlockSpec((n_sample, wc), lambda i: (rb, x_col // wc)),
            pl.BlockSpec((n_sample, wc), lambda i: (rb, g_col // wc)),
            full(prev8), full(h0rep),
            full(cw), full(cb), full(wga), full(bga), full(wgx), full(bgx), full(lam),
        ],
        out_specs=[pl.BlockSpec((n_sample, wc), lambda i: (0, 0)),
                   pl.BlockSpec((n_sample, wc), lambda i: (0, 0))],
        out_shape=[jax.ShapeDtypeStruct((n_sample, wc), F32), jax.ShapeDtypeStruct((n_sample, wc), F32)],
        compiler_params=_params(("arbitrary",)),
        name="lru_sample",
    )(proj, proj, prev8, h0rep, cw, cb, wga, bga, wgx, bgx, lam)


def _mix_kernel(mix_ref, w_ref, x_ref, g_ref, *rest, n_experts):
    if n_experts:
        wr_ref, x1_ref, hn_ref, rg_ref, re_ref = rest
    else:
        x1_ref, hn_ref = rest
    x1 = x_ref[...] + jnp.dot(mix_ref[...], w_ref[...], preferred_element_type=F32)
    x1_ref[...] = x1
    hn = _rms(x1, g_ref[...], RMS_EPS)
    hn_ref[...] = hn.astype(BF16)
    if n_experts:
        logits = jnp.dot(hn, wr_ref[...], preferred_element_type=F32, precision=lax.Precision.HIGHEST)
        lane = lax.broadcasted_iota(jnp.int32, logits.shape, 1)
        lane_f = lane.astype(F32)
        big = float(LANES)
        lg = jnp.where(lane < n_experts, logits, NEG_INF)
        v1 = jnp.max(lg, axis=-1, keepdims=True)
        i1 = jnp.min(jnp.where(lg == v1, lane_f, big), axis=-1, keepdims=True)
        lg2 = jnp.where(lane_f == i1, NEG_INF, lg)
        v2 = jnp.max(lg2, axis=-1, keepdims=True)
        i2 = jnp.min(jnp.where(lg2 == v2, lane_f, big), axis=-1, keepdims=True)
        e = jnp.exp(v2 - v1)
        g1 = 1.0 / (1.0 + e)
        g2 = e / (1.0 + e)
        rg_ref[...] = jnp.where(lane == 0, g1, jnp.where(lane == 1, g2, 0.0))
        re_ref[...] = jnp.where(lane == 0, i1, jnp.where(lane == 1, i2, 0.0)).astype(jnp.int32)


def _mix_call(mix, w_out, x, g, w_router, *, tm):
    n, d = x.shape
    n_experts = 0 if w_router is None else w_router.shape[1]
    kern = functools.partial(_mix_kernel, n_experts=n_experts)
    row = lambda c: pl.BlockSpec((tm, c), lambda i: (i, 0))
    in_specs = [row(d), pl.BlockSpec((d, d), lambda i: (0, 0)), row(d), pl.BlockSpec((1, d), lambda i: (0, 0))]
    out_specs = [row(d), row(d)]
    out_shape = [jax.ShapeDtypeStruct((n, d), F32), jax.ShapeDtypeStruct((n, d), BF16)]
    args = [mix, w_out, x, g]
    if n_experts:
        wr = jnp.zeros((d, LANES), F32).at[:, :n_experts].set(w_router)
        in_specs.append(pl.BlockSpec((d, LANES), lambda i: (0, 0)))
        out_specs += [row(LANES), row(LANES)]
        out_shape += [jax.ShapeDtypeStruct((n, LANES), F32), jax.ShapeDtypeStruct((n, LANES), jnp.int32)]
        args.append(wr)
    return pl.pallas_call(
        kern,
        grid=(n // tm,),
        in_specs=in_specs,
        out_specs=out_specs,
        out_shape=out_shape,
        compiler_params=_params(("parallel",)),
        name="mix_router" if n_experts else "mix",
    )(*args)


def _swiglu_acc(h, wg, wu, wd):
    g = jnp.dot(h, wg, preferred_element_type=F32)
    u = jnp.dot(h, wu, preferred_element_type=F32)
    a = (g * _sigmoid(g) * u).astype(BF16)
    return jnp.dot(a, wd, preferred_element_type=F32)


def _ffn_kernel(h_ref, wg_ref, wu_ref, wd_ref, x_ref, o_ref):
    j = pl.program_id(1)

    @pl.when(j == 0)
    def _():
        o_ref[...] = x_ref[...]

    o_ref[...] += _swiglu_acc(h_ref[...], wg_ref[...], wu_ref[...], wd_ref[...])


def _ffn_call(hn, wg, wu, wd, x1, *, tm, tf):
    n, d = x1.shape
    f = wg.shape[1]
    return pl.pallas_call(
        _ffn_kernel,
        grid=(n // tm, f // tf),
        in_specs=[
            pl.BlockSpec((tm, d), lambda i, j: (i, 0)),
            pl.BlockSpec((d, tf), lambda i, j: (0, j)),
            pl.BlockSpec((d, tf), lambda i, j: (0, j)),
            pl.BlockSpec((tf, d), lambda i, j: (j, 0)),
            pl.BlockSpec((tm, d), lambda i, j: (i, 0)),
        ],
        out_specs=pl.BlockSpec((tm, d), lambda i, j: (i, 0)),
        out_shape=jax.ShapeDtypeStruct((n, d), F32),
        compiler_params=_params(("parallel", "arbitrary")),
        name="ffn_dense",
    )(hn, wg, wu, wd, x1)


def _gather_norm_kernel(rt_ref, nv_ref, x_hbm, g_ref, o_ref, buf_ref, sem, *, tg):
    t = pl.program_id(0)

    def row_copy(r, tok):
        return pltpu.make_async_copy(x_hbm.at[pl.ds(tok, 1), :], buf_ref.at[pl.ds(r, 1), :], sem)

    @pl.when(t < nv_ref[0])
    def _():
        def issue(r, carry):
            row_copy(r, rt_ref[t * tg + r]).start()
            return carry

        def wait(r, carry):
            row_copy(r, 0).wait()
            return carry

        lax.fori_loop(0, tg, issue, 0)
        lax.fori_loop(0, tg, wait, 0)
        o_ref[...] = _rms(buf_ref[...], g_ref[...], RMS_EPS).astype(BF16)

    @pl.when(t >= nv_ref[0])
    def _():
        o_ref[...] = jnp.zeros(o_ref.shape, o_ref.dtype)


def _gather_norm_call(row_token, n_valid, x1, g, *, tg):
    r = row_token.shape[0]
    d = x1.shape[1]
    grid_spec = pltpu.PrefetchScalarGridSpec(
        num_scalar_prefetch=2,
        grid=(r // tg,),
        in_specs=[pl.BlockSpec(memory_space=pl.ANY), pl.BlockSpec((1, d), lambda t, rt, nv: (0, 0))],
        out_specs=pl.BlockSpec((tg, d), lambda t, rt, nv: (t, 0)),
        scratch_shapes=[pltpu.VMEM((tg, d), F32), pltpu.SemaphoreType.DMA(())],
    )
    return pl.pallas_call(
        functools.partial(_gather_norm_kernel, tg=tg),
        grid_spec=grid_spec,
        out_shape=jax.ShapeDtypeStruct((r, d), BF16),
        compiler_params=_params(("arbitrary",)),
        name="moe_gather",
    )(row_token, n_valid, x1, g)


def _moe_ffn_kernel(te_ref, nv_ref, h_ref, wg_ref, wu_ref, wd_ref, o_ref):
    del te_ref
    t = pl.program_id(0)
    j = pl.program_id(1)

    @pl.when(j == 0)
    def _():
        o_ref[...] = jnp.zeros(o_ref.shape, F32)

    @pl.when(t < nv_ref[0])
    def _():
        o_ref[...] += _swiglu_acc(h_ref[...], wg_ref[...], wu_ref[...], wd_ref[...])


def _moe_ffn_call(tile_expert, n_valid, xs, wg, wu, wd, *, tm, tf):
    r, d = xs.shape
    f = wg.shape[2]
    nj = f // tf

    def tile(t, nv):
        return jnp.minimum(t, nv[0] - 1)

    def fblk(t, j, nv):
        return jnp.where(t < nv[0], j, nj - 1)

    grid_spec = pltpu.PrefetchScalarGridSpec(
        num_scalar_prefetch=2,
        grid=(r // tm, nj),
        in_specs=[
            pl.BlockSpec((tm, d), lambda t, j, te, nv: (tile(t, nv), 0)),
            pl.BlockSpec((None, d, tf), lambda t, j, te, nv: (te[t], 0, fblk(t, j, nv))),
            pl.BlockSpec((None, d, tf), lambda t, j, te, nv: (te[t], 0, fblk(t, j, nv))),
            pl.BlockSpec((None, tf, d), lambda t, j, te, nv: (te[t], fblk(t, j, nv), 0)),
        ],
        out_specs=pl.BlockSpec((tm, d), lambda t, j, te, nv: (t, 0)),
    )
    return pl.pallas_call(
        _moe_ffn_kernel,
        grid_spec=grid_spec,
        out_shape=jax.ShapeDtypeStruct((r, d), F32),
        compiler_params=_params(("arbitrary", "arbitrary")),
        name="moe_ffn",
    )(tile_expert, n_valid, xs, wg, wu, wd)


def _combine_norm_kernel(pos_ref, x_ref, rg_ref, ys_hbm, g_ref, o_ref, buf_ref, sem, *, tc, final_norm):
    i = pl.program_id(0)

    def row_copy(k, r, src):
        return pltpu.make_async_copy(ys_hbm.at[pl.ds(src, 1), :], buf_ref.at[k, pl.ds(r, 1), :], sem)

    def issue(r, carry):
        for k in range(TOP_K):
            row_copy(k, r, pos_ref[(i * tc + r) * TOP_K + k]).start()
        return carry

    def wait(r, carry):
        for k in range(TOP_K):
            row_copy(k, r, 0).wait()
        return carry

    lax.fori_loop(0, tc, issue, 0)
    lax.fori_loop(0, tc, wait, 0)
    rg = rg_ref[...]
    moe = rg[:, 0:1] * buf_ref[0]
    for k in range(1, TOP_K):
        moe = moe + rg[:, k:k + 1] * buf_ref[k]
    x2 = x_ref[...] + moe
    o_ref[...] = _rms(x2, g_ref[...], RMS_EPS) if final_norm else x2


def _combine_norm_call(pos, x1, rg, ys, g, *, tc, final_norm):
    n, d = x1.shape
    grid_spec = pltpu.PrefetchScalarGridSpec(
        num_scalar_prefetch=1,
        grid=(n // tc,),
        in_specs=[
            pl.BlockSpec((tc, d), lambda i, ps: (i, 0)),
            pl.BlockSpec((tc, LANES), lambda i, ps: (i, 0)),
            pl.BlockSpec(memory_space=pl.ANY),
            pl.BlockSpec((1, d), lambda i, ps: (0, 0)),
        ],
        out_specs=pl.BlockSpec((tc, d), lambda i, ps: (i, 0)),
        scratch_shapes=[pltpu.VMEM((TOP_K, tc, d), F32), pltpu.SemaphoreType.DMA(())],
    )
    return pl.pallas_call(
        functools.partial(_combine_norm_kernel, tc=tc, final_norm=final_norm),
        grid_spec=grid_spec,
        out_shape=jax.ShapeDtypeStruct((n, d), F32),
        compiler_params=_params(("arbitrary",)),
        name="moe_combine",
    )(pos, x1, rg, ys, g)


def _norm_kernel(x_ref, g_ref, o_ref):
    o_ref[...] = _rms(x_ref[...], g_ref[...], RMS_EPS)


def _norm_call(x, g, *, tm):
    n, d = x.shape
    return pl.pallas_call(
        _norm_kernel,
        grid=(n // tm,),
        in_specs=[pl.BlockSpec((tm, d), lambda i: (i, 0)), pl.BlockSpec((1, d), lambda i: (0, 0))],
        out_specs=pl.BlockSpec((tm, d), lambda i: (i, 0)),
        out_shape=jax.ShapeDtypeStruct((n, d), F32),
        compiler_params=_params(("parallel",)),
        name="final_norm",
    )(x, g)


def _route_plan(re, n_experts, tm):
    n = re.shape[0]
    e_flat = re.reshape(-1)
    oh = (e_flat[:, None] == jnp.arange(n_experts, dtype=jnp.int32)[None, :]).astype(jnp.int32)
    csum = jnp.cumsum(oh, axis=0)
    rank = jnp.sum((csum - oh) * oh, axis=1)
    counts = csum[-1]
    padded = ((counts + tm - 1) // tm) * tm
    ends = jnp.cumsum(padded)
    offs = ends - padded
    pos = jnp.sum(oh * offs[None, :], axis=1) + rank
    n_rows = ((n * TOP_K + tm - 1) // tm) * tm + n_experts * tm
    row_token = jnp.zeros((n_rows,), jnp.int32).at[pos].set(jnp.arange(n * TOP_K, dtype=jnp.int32) // TOP_K)
    n_tiles = n_rows // tm
    n_valid = (ends[-1] // tm).astype(jnp.int32)
    tile_start = jnp.arange(n_tiles, dtype=jnp.int32) * tm
    tile_e = jnp.sum((tile_start[:, None] >= ends[None, :]).astype(jnp.int32), axis=1)
    last_e = jnp.sum((tile_start[n_valid - 1] >= ends).astype(jnp.int32))
    tile_e = jnp.where(jnp.arange(n_tiles) < n_valid, tile_e, last_e).astype(jnp.int32)
    return pos.astype(jnp.int32), row_token, tile_e, n_valid.reshape(1)


def _tile(pref, *dims):
    t = pref
    for dim in dims:
        t = math.gcd(t, dim)
    return t


def _rope_tables(seq, n_past, t_new, bs, dk, rot_dim):
    half = rot_dim // 2
    inv = ROPE_THETA ** (-(jnp.arange(half, dtype=F32) * 2.0) / rot_dim)
    pos = jnp.concatenate([jnp.arange(seq), jnp.tile(n_past + jnp.arange(t_new), bs)])
    ang = pos.astype(F32)[:, None] * inv[None, :]
    cos, sin = jnp.cos(ang), jnp.sin(ang)
    n = pos.shape[0]
    ones = jnp.ones((n, dk - rot_dim), F32)
    zeros_h = jnp.zeros((n, half), F32)
    zeros_r = jnp.zeros((n, dk - rot_dim), F32)
    cos_t = jnp.concatenate([cos, cos, ones], axis=1)
    sa_t = jnp.concatenate([zeros_h, sin, zeros_r], axis=1)
    sb_t = jnp.concatenate([-sin, zeros_h, zeros_r], axis=1)
    return cos_t, sa_t, sb_t


def _page_mats(heads, page, n_pages):
    w = heads * page
    src = jnp.arange(w)
    s_slot, s_head = src // heads, src % heads
    dst = jnp.arange(w)
    d_head, d_slot = dst // page, dst % page
    same = s_head[:, None] == d_head[None, :]
    mcs = (same & (s_slot[:, None] <= d_slot[None, :])).astype(BF16)
    mbc = ((d_head[:, None] == d_head[None, :]) & (d_slot[:, None] == page - 1)).astype(BF16)
    msel = ((d_slot[:, None] == page - 1) & (d_head[:, None] == jnp.arange(LANES)[None, :])).astype(BF16)
    pg = jnp.arange(n_pages)
    lst = (pg[None, :] < pg[:, None]).astype(BF16)
    return mcs, mbc, msel, lst


def kernel(x_prompt, x_sample, cache_k_diff, cache_v_diff, cache_k_fox, cache_v_fox, cache_logf_fox, state_rglru_h, state_rglru_conv, page_table, norm_mix_g, w_in, b_fgate, lambda_q1, lambda_k1, lambda_q2, lambda_k2, subln_g, conv_w, conv_b, w_gate_a, b_gate_a, w_gate_x, b_gate_x, lru_lambda, w_out, norm_ffn_g, w_dense_gate, w_dense_up, w_dense_down, w_router, w_moe_gate, w_moe_up, w_moe_down, norm_final_g):
    bp, seq, d = x_prompt.shape
    bs, t_new, _ = x_sample.shape
    depth = w_in.shape[0]
    n_pool, page, h_a = cache_k_diff.shape[1], cache_k_diff.shape[2], cache_k_diff.shape[3]
    dk_a = cache_k_diff.shape[5]
    dv_a = cache_v_diff.shape[4]
    h_b, dh_b = cache_k_fox.shape[3], cache_k_fox.shape[4]
    w_c = state_rglru_h.shape[2]
    cw_len = conv_w.shape[1]
    n_pages = page_table.shape[1]
    n_past = n_pages * page
    rot_dim = dk_a // 4
    n_experts = w_router.shape[2]
    n_p, n_s = bp * seq, bs * t_new
    n = n_p + n_s
    assert t_new == SUBLANES and page == LANES and dv_a == 2 * dk_a and dk_a == LANES and dh_b == LANES

    w_qa = h_a * 2 * dk_a
    w_va = h_a * dv_a
    w_b = h_b * dh_b
    col_qa, col_ka, col_va = 0, w_qa, 2 * w_qa
    col_qb = col_va + w_va
    col_kb, col_vb = col_qb + w_b, col_qb + 2 * w_b
    col_g = col_vb + w_b
    col_x = col_g + w_c
    n_main = col_x + w_c
    src_fl = 2 * w_qa + w_va + 3 * w_b

    tm = _tile(TM_TOKENS, seq, n_s)
    tn = _tile(TN_PROJ, w_qa, n_main)
    tq = _tile(TQ_ATTN, seq)
    tt = _tile(TT_LRU, seq)
    tm_mix = _tile(TM_MIX, tm)
    tf = _tile(TF_FFN, w_dense_gate.shape[2])
    tm_moe = _tile(TM_MOE, n)
    tc = _tile(TC_COMBINE, n)

    rope = _rope_tables(seq, n_past, t_new, bs, dk_a, rot_dim)
    tri = jnp.tril(jnp.ones((seq, seq), BF16))
    page_mats = _page_mats(h_b, page, n_pages)

    kc_d = cache_k_diff.reshape(depth, n_pool, page * h_a * 2, dk_a)
    nc_a = dv_a // LANES
    vc_d = jnp.swapaxes(cache_v_diff.reshape(depth, n_pool, page, h_a, nc_a, LANES), 3, 4).reshape(
        depth, n_pool, page * nc_a * h_a, LANES)
    kc_f = cache_k_fox.reshape(depth, n_pool, page * h_b, dh_b)
    vc_f = cache_v_fox.reshape(depth, n_pool, page * h_b, dh_b)
    lf_pages = cache_logf_fox.reshape(depth, n_pool, page * h_b)

    x = jnp.concatenate([x_prompt.reshape(n_p, d), x_sample.reshape(n_s, d)], axis=0)
    row2 = lambda v: v.reshape(1, -1).astype(F32)
    st = {k: [] for k in ('kd', 'vd', 'kf', 'vf', 'lf', 'h_p', 'h_s', 'cv')}
    y_all = None

    for l in range(depth):
        lam_init = 0.8 - 0.6 * math.exp(-0.3 * l)
        w_l = w_in[l]
        w_main = jnp.concatenate([w_l[:, :src_fl], w_l[:, src_fl + h_b:]], axis=1).astype(BF16)
        w_fl = jnp.zeros((d, LANES), F32).at[:, :h_b].set(w_l[:, src_fl:src_fl + h_b]).astype(BF16)
        b_fl = jnp.zeros((1, LANES), F32).at[0, :h_b].set(b_fgate[l])
        proj, logf = _proj_call(
            x, row2(norm_mix_g[l]), w_main, w_fl, b_fl, rope, tm=tm, tn=tn, n_rope_cols=2 * w_qa,
            n_prompt_tiles=n_p // tm, rope_prompt_tiles=seq // tm, rot_half=rot_dim // 2)

        lams = (row2(lambda_q1[l]), row2(lambda_k1[l]), row2(lambda_q2[l]), row2(lambda_k2[l]))
        sg = row2(subln_g[l])

        oa_p = _diff_prompt_call(proj, lams, sg, batch=bp, seq=seq, heads=h_a, dk=dk_a, dv=dv_a, tq=tq,
                                 k_col=col_ka, v_col=col_va, lam_init=lam_init)
        c_col = _cumsum_call(logf, tri, bp, seq)
        c_row = jnp.swapaxes(c_col[:, :h_b].reshape(bp, seq, h_b), 1, 2).reshape(bp * h_b, seq // tq, tq)
        ob_p = _fox_prompt_call(proj, c_col, c_row, batch=bp, seq=seq, heads=h_b, dh=dh_b, tq=tq,
                                q_col=col_qb, k_col=col_kb, v_col=col_vb)
        lru_w = (conv_w[l], row2(conv_b[l]), w_gate_a[l].astype(BF16), row2(b_gate_a[l]),
                 w_gate_x[l].astype(BF16), row2(b_gate_x[l]), row2(lru_lambda[l]))
        oc_p, ht_p = _lru_prompt_call(proj, lru_w, batch=bp, seq=seq, wc=w_c, tt=tt, g_col=col_g, x_col=col_x)

        oa_s = _diff_sample_call(page_table, proj, kc_d, vc_d, lams, sg, l, n_prompt=n_p, t_new=t_new,
                                 heads=h_a, dk=dk_a, dv=dv_a, page=page, k_col=col_ka, v_col=col_va,
                                 lam_init=lam_init)
        cpast, ctot = _logf_pages_call(page_table, lf_pages[l], page_mats, heads=h_b, page=page)
        ob_s = _fox_sample_call(page_table, proj, logf, kc_f, vc_f, cpast, ctot, l, n_prompt=n_p,
                                t_new=t_new, heads=h_b, dh=dh_b, page=page, q_col=col_qb, k_col=col_kb,
                                v_col=col_vb)
        prev8 = jnp.pad(state_rglru_conv[l], ((0, 0), (SUBLANES - (cw_len - 1), 0), (0, 0))).reshape(n_s, w_c)
        h0rep = jnp.repeat(state_rglru_h[l], t_new, axis=0)
        oc_s, hs_s = _lru_sample_call(proj, prev8, h0rep, lru_w, n_prompt=n_p, n_sample=n_s, wc=w_c,
                                      g_col=col_g, x_col=col_x)

        mix = jnp.concatenate([
            jnp.concatenate([oa_p, ob_p, oc_p], axis=1),
            jnp.concatenate([oa_s, ob_s, oc_s], axis=1).astype(BF16)], axis=0)
        g_ffn = row2(norm_ffn_g[l])
        m = l // 2
        if l % 2 == 0:
            x1, hn = _mix_call(mix, w_out[l].astype(BF16), x, g_ffn, None, tm=tm_mix)
            x = _ffn_call(hn, w_dense_gate[m].astype(BF16), w_dense_up[m].astype(BF16),
                          w_dense_down[m].astype(BF16), x1, tm=tm, tf=tf)
            if l == depth - 1:
                y_all = _norm_call(x, row2(norm_final_g), tm=tm_mix)
        else:
            x1, _, rg, re = _mix_call(mix, w_out[l].astype(BF16), x, g_ffn, w_router[m], tm=tm_mix)
            pos, row_token, tile_e, n_valid = _route_plan(re[:, :TOP_K], n_experts, tm_moe)
            xs = _gather_norm_call(row_token, n_valid, x1, g_ffn, tg=tm_moe)
            ys = _moe_ffn_call(tile_e, n_valid, xs, w_moe_gate[m].astype(BF16), w_moe_up[m].astype(BF16),
                               w_moe_down[m].astype(BF16), tm=tm_moe, tf=tf)
            final = l == depth - 1
            x = _combine_norm_call(pos, x1, rg, ys, row2(norm_final_g), tc=tc, final_norm=final)
            if final:
                y_all = x

        st['kd'].append(proj[:, col_ka:col_ka + w_qa])
        st['vd'].append(proj[:, col_va:col_va + w_va])
        st['kf'].append(proj[:, col_kb:col_kb + w_b])
        st['vf'].append(proj[:, col_vb:col_vb + w_b])
        st['lf'].append(logf[:, :h_b])
        st['h_p'].append(ht_p.reshape(bp, w_c))
        st['h_s'].append(hs_s.reshape(bs, t_new, w_c)[:, t_new - 1])
        st['cv'].append(proj[:, col_x:col_x + w_c])

    def both(name, shape_tail):
        a = jnp.stack(st[name])
        return (a[:, :n_p].reshape((depth, bp, seq) + shape_tail),
                a[:, n_p:].reshape((depth, bs, t_new) + shape_tail))

    kd_p, kd_s = both('kd', (h_a, 2, dk_a))
    vd_p, vd_s = both('vd', (h_a, dv_a))
    kf_p, kf_s = both('kf', (h_b, dh_b))
    vf_p, vf_s = both('vf', (h_b, dh_b))
    lf_p, lf_s = both('lf', (h_b,))
    cv_p, cv_s = both('cv', (w_c,))
    keep = cw_len - 1
    return (y_all[:n_p].reshape(bp, seq, d), y_all[n_p:].reshape(bs, t_new, d),
            kd_p, vd_p, kf_p, vf_p, lf_p, jnp.stack(st['h_p']), cv_p[:, :, seq - keep:],
            kd_s, vd_s, kf_s, vf_s, lf_s, jnp.stack(st['h_s']), cv_s[:, :, t_new - keep:])
```

```python
import functools
import math

import jax
import jax.numpy as jnp
from jax import lax
from jax.experimental import pallas as pl
from jax.experimental.pallas import tpu as pltpu

F32 = jnp.float32
BF16 = jnp.bfloat16

LANES = 128
SUBLANES = 8
VMEM_LIMIT = 56 * 1024 * 1024

TM_TOKENS = 1024
TN_PROJ = 512
TQ_ATTN = 512
TT_LRU = 256
TM_MIX = 512
TF_FFN = 256
TF_MOE = 512
TM_MOE = 512
TC_COMBINE = 256
PAGES_PER_STEP = 8

RMS_EPS = 1e-6
SUBLN_EPS = 1e-5
LRU_C = 8.0
ROPE_THETA = 500000.0
TOP_K = 2
NEG_INF = float("-inf")


def _params(sem, vmem=VMEM_LIMIT):
    return pltpu.CompilerParams(dimension_semantics=sem, vmem_limit_bytes=vmem)


def _split3(x):
    hi = x.astype(BF16)
    r = x - hi.astype(F32)
    mid = r.astype(BF16)
    lo = (r - mid.astype(F32)).astype(BF16)
    return hi, mid, lo


def _dot_sel(x, m):
    hi, mid, lo = _split3(x)
    d = lambda a: jnp.dot(a, m, preferred_element_type=F32)
    return d(hi) + d(mid) + d(lo)


def _sel_dot(m, x):
    hi, mid, lo = _split3(x)
    d = lambda a: jnp.dot(m, a, preferred_element_type=F32)
    return d(hi) + d(mid) + d(lo)


def _dot_nt(a, b):
    return lax.dot_general(a, b, (((1,), (1,)), ((), ())), preferred_element_type=F32)


def _sigmoid(x):
    return 1.0 / (1.0 + jnp.exp(-x))


def _softplus(x):
    return jnp.maximum(x, 0.0) + jnp.log1p(jnp.exp(-jnp.abs(x)))


def _rms(x, g, eps):
    ms = jnp.mean(x * x, axis=-1, keepdims=True)
    return x * lax.rsqrt(ms + eps) * g


def _proj_kernel(x_ref, g_ref, wa_ref, wb_ref, wfl_ref, bf_ref, cos_ref, sa_ref, sb_ref,
                 proj_ref, logf_ref, hn_ref, *, n_rope_tiles, n_a_tiles, tn, rot_half):
    j = pl.program_id(1)

    @pl.when(j == 0)
    def _():
        hn = _rms(x_ref[...], g_ref[...], RMS_EPS).astype(BF16)
        hn_ref[...] = hn
        z = jnp.dot(hn, wfl_ref[...].astype(BF16), preferred_element_type=F32) + bf_ref[...]
        logf_ref[...] = jnp.minimum(z, 0.0) - jnp.log1p(jnp.exp(-jnp.abs(z)))

    @pl.when(j < n_rope_tiles)
    def _():
        y = jnp.dot(hn_ref[...], wa_ref[...].astype(BF16), preferred_element_type=F32)
        c, sa, sb = cos_ref[...], sa_ref[...], sb_ref[...]
        for k in range(tn // LANES):
            yc = y[:, k * LANES:(k + 1) * LANES]
            proj_ref[:, k * LANES:(k + 1) * LANES] = (
                yc * c + pltpu.roll(yc, rot_half, 1) * sa + pltpu.roll(yc, LANES - rot_half, 1) * sb)

    @pl.when((j >= n_rope_tiles) & (j < n_a_tiles))
    def _():
        proj_ref[...] = jnp.dot(hn_ref[...], wa_ref[...].astype(BF16), preferred_element_type=F32)

    @pl.when(j >= n_a_tiles)
    def _():
        proj_ref[...] = jnp.dot(hn_ref[...], wb_ref[...], preferred_element_type=F32)


def _proj_call(x, g, w_in, layer, fl_col, wb, bfl, rope, *, tm, tn, n_rope_cols, n_prompt_tiles,
               rope_prompt_tiles, rot_half):
    n, d = x.shape
    na, nb = fl_col // tn, wb.shape[1] // tn
    n_main = (na + nb) * tn
    cos_t, sa_t, sb_t = rope
    rope_map = lambda i, j: (jnp.where(i < n_prompt_tiles, i % rope_prompt_tiles,
                                       rope_prompt_tiles + i - n_prompt_tiles), 0)
    kern = functools.partial(_proj_kernel, n_rope_tiles=n_rope_cols // tn, n_a_tiles=na, tn=tn,
                             rot_half=rot_half)
    return pl.pallas_call(
        kern,
        grid=(n // tm, n_main // tn),
        in_specs=[
            pl.BlockSpec((tm, d), lambda i, j: (i, 0)),
            pl.BlockSpec((1, d), lambda i, j: (0, 0)),
            pl.BlockSpec((None, d, tn), lambda i, j: (layer, 0, jnp.minimum(j, na - 1))),
            pl.BlockSpec((d, tn), lambda i, j: (0, jnp.maximum(j - na, 0))),
            pl.BlockSpec((None, d, LANES), lambda i, j: (layer, 0, fl_col // LANES)),
            pl.BlockSpec((1, LANES), lambda i, j: (0, 0)),
            pl.BlockSpec((tm, LANES), rope_map),
            pl.BlockSpec((tm, LANES), rope_map),
            pl.BlockSpec((tm, LANES), rope_map),
        ],
        out_specs=[
            pl.BlockSpec((tm, tn), lambda i, j: (i, j)),
            pl.BlockSpec((tm, LANES), lambda i, j: (i, 0)),
        ],
        out_shape=[jax.ShapeDtypeStruct((n, n_main), F32), jax.ShapeDtypeStruct((n, LANES), F32)],
        scratch_shapes=[pltpu.VMEM((tm, d), BF16)],
        compiler_params=_params(("parallel", "arbitrary")),
        name="proj",
    )(x, g, w_in, wb, w_in, bfl, cos_t, sa_t, sb_t)


def _cumsum_kernel(x_ref, tri_ref, o_ref):
    o_ref[...] = _sel_dot(tri_ref[...], x_ref[...])


def _cumsum_call(logf_p, tri, batch, seq):
    return pl.pallas_call(
        _cumsum_kernel,
        grid=(batch,),
        in_specs=[pl.BlockSpec((seq, LANES), lambda b: (b, 0)),
                  pl.BlockSpec((seq, seq), lambda b: (0, 0))],
        out_specs=pl.BlockSpec((seq, LANES), lambda b: (b, 0)),
        out_shape=jax.ShapeDtypeStruct((batch * seq, LANES), F32),
        compiler_params=_params(("parallel",)),
        name="cumsum_prompt",
    )(logf_p, tri)


def _softmax_step(s, v, m_ref, l_ref, acc_ref, idx):
    m_old = m_ref[idx]
    m_new = jnp.maximum(m_old, jnp.max(s, axis=-1, keepdims=True))
    p = jnp.exp(s - m_new)
    alpha = jnp.exp(m_old - m_new)
    l_ref[idx] = alpha * l_ref[idx] + jnp.sum(p, axis=-1, keepdims=True)
    acc_ref[idx] = alpha * acc_ref[idx] + jnp.dot(p.astype(BF16), v, preferred_element_type=F32)
    m_ref[idx] = m_new


def _softmax_step_stacked(s, v_heads, m_ref, l_ref, acc_ref, rows_per_head):
    m_old = m_ref[...]
    m_new = jnp.maximum(m_old, jnp.max(s, axis=-1, keepdims=True))
    p = jnp.exp(s - m_new)
    alpha = jnp.exp(m_old - m_new)
    l_ref[...] = alpha * l_ref[...] + jnp.sum(p, axis=-1, keepdims=True)
    pv = [jnp.dot(p[h * rows_per_head:(h + 1) * rows_per_head, :].astype(BF16), v,
                  preferred_element_type=F32) for h, v in enumerate(v_heads)]
    acc_ref[...] = alpha * acc_ref[...] + jnp.concatenate(pv, axis=0)
    m_ref[...] = m_new


def _diff_lambda(lq1, lk1, lq2, lk2, lam_init):
    return (jnp.exp(jnp.sum(lq1[...] * lk1[...], axis=-1, keepdims=True))
            - jnp.exp(jnp.sum(lq2[...] * lk2[...], axis=-1, keepdims=True)) + lam_init)


def _diff_finish(acc_ref, l_ref, base, lam, sg, lam_init):
    o = acc_ref[base] / l_ref[base] - lam * (acc_ref[base + 1] / l_ref[base + 1])
    return _rms(o, sg, SUBLN_EPS) * (1.0 - lam_init)


def _diff_prompt_kernel(q_ref, k_ref, v_ref, lq1, lk1, lq2, lk2, sg_ref, o_ref,
                        kb_ref, vb_ref, m_ref, l_ref, acc_ref, *, tq, dk, lam_init, scale):
    qi = pl.program_id(2)

    @pl.when(qi == 0)
    def _():
        kb_ref[...] = k_ref[...].astype(BF16)
        vb_ref[...] = v_ref[...].astype(BF16)

    m_ref[...] = jnp.full(m_ref.shape, NEG_INF, F32)
    l_ref[...] = jnp.zeros(l_ref.shape, F32)
    acc_ref[...] = jnp.zeros(acc_ref.shape, F32)
    q = q_ref[...].astype(BF16)

    def block(kk, causal):
        k0 = pl.multiple_of(kk * tq, tq)
        kblk = kb_ref[pl.ds(k0, tq), :]
        vblk = vb_ref[pl.ds(k0, tq), :]
        for m in range(2):
            s = _dot_nt(q[:, m * dk:(m + 1) * dk], kblk[:, m * dk:(m + 1) * dk]) * scale
            if causal:
                row = lax.broadcasted_iota(jnp.int32, s.shape, 0)
                col = lax.broadcasted_iota(jnp.int32, s.shape, 1)
                s = jnp.where(col <= row, s, NEG_INF)
            _softmax_step(s, vblk, m_ref, l_ref, acc_ref, m)

    def body(kk, carry):
        block(kk, False)
        return carry

    lax.fori_loop(0, qi, body, 0)
    block(qi, True)

    lam = _diff_lambda(lq1, lk1, lq2, lk2, lam_init)
    o_ref[...] = _diff_finish(acc_ref, l_ref, 0, lam, sg_ref[...], lam_init).astype(o_ref.dtype)


def _diff_prompt_call(proj, lams, sg, *, batch, seq, heads, dk, dv, tq, k_col, v_col, lam_init):
    nq = seq // tq
    n_rows = batch * seq
    lam_spec = pl.BlockSpec((1, dk), lambda b, h, q: (0, 0))
    kern = functools.partial(_diff_prompt_kernel, tq=tq, dk=dk, lam_init=lam_init, scale=dk ** -0.5)
    return pl.pallas_call(
        kern,
        grid=(batch, heads, nq),
        in_specs=[
            pl.BlockSpec((tq, 2 * dk), lambda b, h, q: (b * nq + q, h)),
            pl.BlockSpec((seq, 2 * dk), lambda b, h, q: (b, k_col // (2 * dk) + h)),
            pl.BlockSpec((seq, dv), lambda b, h, q: (b, v_col // dv + h)),
            lam_spec, lam_spec, lam_spec, lam_spec,
            pl.BlockSpec((1, dv), lambda b, h, q: (0, 0)),
        ],
        out_specs=pl.BlockSpec((tq, dv), lambda b, h, q: (b * nq + q, h)),
        out_shape=jax.ShapeDtypeStruct((n_rows, heads * dv), BF16),
        scratch_shapes=[
            pltpu.VMEM((seq, 2 * dk), BF16), pltpu.VMEM((seq, dv), BF16),
            pltpu.VMEM((2, tq, 1), F32), pltpu.VMEM((2, tq, 1), F32), pltpu.VMEM((2, tq, dv), F32),
        ],
        compiler_params=_params(("parallel", "parallel", "arbitrary")),
        name="diff_prompt",
    )(proj, proj, proj, *lams, sg)


def _fox_prompt_kernel(q_ref, k_ref, v_ref, cc_ref, cr_ref, o_ref,
                       kb_ref, vb_ref, m_ref, l_ref, acc_ref, *, tq, scale):
    h = pl.program_id(1)
    qi = pl.program_id(2)

    @pl.when(qi == 0)
    def _():
        kb_ref[...] = k_ref[...].astype(BF16)
        vb_ref[...] = v_ref[...].astype(BF16)

    m_ref[...] = jnp.full(m_ref.shape, NEG_INF, F32)
    l_ref[...] = jnp.zeros(l_ref.shape, F32)
    acc_ref[...] = jnp.zeros(acc_ref.shape, F32)
    q = q_ref[...].astype(BF16)
    cc = cc_ref[...]
    lane = lax.broadcasted_iota(jnp.int32, cc.shape, 1)
    cq = jnp.sum(jnp.where(lane == h, cc, 0.0), axis=-1, keepdims=True)

    def block(kk, causal):
        k0 = pl.multiple_of(kk * tq, tq)
        ck = cr_ref[0, pl.ds(kk, 1), :]
        s = _dot_nt(q, kb_ref[pl.ds(k0, tq), :]) * scale + (cq - ck)
        if causal:
            row = lax.broadcasted_iota(jnp.int32, s.shape, 0)
            col = lax.broadcasted_iota(jnp.int32, s.shape, 1)
            s = jnp.where(col <= row, s, NEG_INF)
        _softmax_step(s, vb_ref[pl.ds(k0, tq), :], m_ref, l_ref, acc_ref, 0)

    def body(kk, carry):
        block(kk, False)
        return carry

    lax.fori_loop(0, qi, body, 0)
    block(qi, True)
    o_ref[...] = (acc_ref[0] / l_ref[0]).astype(o_ref.dtype)


def _fox_prompt_call(proj, c_col, c_row, *, batch, seq, heads, dh, tq, q_col, k_col, v_col):
    nq = seq // tq
    kern = functools.partial(_fox_prompt_kernel, tq=tq, scale=dh ** -0.5)
    return pl.pallas_call(
        kern,
        grid=(batch, heads, nq),
        in_specs=[
            pl.BlockSpec((tq, dh), lambda b, h, q: (b * nq + q, q_col // dh + h)),
            pl.BlockSpec((seq, dh), lambda b, h, q: (b, k_col // dh + h)),
            pl.BlockSpec((seq, dh), lambda b, h, q: (b, v_col // dh + h)),
            pl.BlockSpec((tq, LANES), lambda b, h, q: (b * nq + q, 0)),
            pl.BlockSpec((1, nq, tq), lambda b, h, q: (b * heads + h, 0, 0)),
        ],
        out_specs=pl.BlockSpec((tq, dh), lambda b, h, q: (b * nq + q, h)),
        out_shape=jax.ShapeDtypeStruct((batch * seq, heads * dh), BF16),
        scratch_shapes=[
            pltpu.VMEM((seq, dh), BF16), pltpu.VMEM((seq, dh), BF16),
            pltpu.VMEM((1, tq, 1), F32), pltpu.VMEM((1, tq, 1), F32), pltpu.VMEM((1, tq, dh), F32),
        ],
        compiler_params=_params(("parallel", "parallel", "arbitrary")),
        name="fox_prompt",
    )(proj, proj, proj, c_col, c_row)


def _pad_rows(x, rows):
    return jnp.concatenate([x, jnp.zeros((rows - x.shape[0], x.shape[1]), x.dtype)], axis=0)


def _causal_new(s, t_new):
    row = lax.broadcasted_iota(jnp.int32, s.shape, 0) % t_new
    col = lax.broadcasted_iota(jnp.int32, s.shape, 1)
    return jnp.where(col <= row, s, NEG_INF)


def _diff_sample_kernel(pt_ref, q_ref, kn_ref, vn_ref, *refs, heads, dk, dv, page, n_group, lam_init, scale):
    del pt_ref
    kc_refs, vc_refs = refs[:n_group], refs[n_group:2 * n_group]
    lq1, lk1, lq2, lk2, sg_ref, o_ref, m_ref, l_ref, acc_ref = refs[2 * n_group:]
    p = pl.program_id(1)
    last = pl.num_programs(1) - 1
    t_new = q_ref.shape[0]
    n_maps = 2 * heads
    nc = dv // LANES

    @pl.when(p == 0)
    def _():
        m_ref[...] = jnp.full(m_ref.shape, NEG_INF, F32)
        l_ref[...] = jnp.zeros(l_ref.shape, F32)
        acc_ref[...] = jnp.zeros(acc_ref.shape, F32)

    q = q_ref[...]
    qb = [q[:, hm * dk:(hm + 1) * dk].astype(BF16) for hm in range(n_maps)]

    s = jnp.concatenate([
        jnp.concatenate([_dot_nt(qb[hm], kc[pl.ds(hm, page, stride=n_maps), :].astype(BF16)) for kc in kc_refs],
                        axis=1) for hm in range(n_maps)], axis=0) * scale
    v_heads = [jnp.concatenate([
        jnp.concatenate([vc[pl.ds(c * heads + h, page, stride=heads * nc), :] for c in range(nc)], axis=1)
        for vc in vc_refs], axis=0).astype(BF16) for h in range(heads)]
    _softmax_step_stacked(s, v_heads, m_ref, l_ref, acc_ref, 2 * t_new)

    @pl.when(p == last)
    def _():
        kn = kn_ref[...]
        vn = vn_ref[...]
        s_new = jnp.concatenate([
            _dot_nt(qb[hm], _pad_rows(kn[:, hm * dk:(hm + 1) * dk], page).astype(BF16))
            for hm in range(n_maps)], axis=0) * scale
        v_new = [_pad_rows(vn[:, h * dv:(h + 1) * dv], page).astype(BF16) for h in range(heads)]
        _softmax_step_stacked(_causal_new(s_new, t_new), v_new, m_ref, l_ref, acc_ref, 2 * t_new)
        lam = _diff_lambda(lq1, lk1, lq2, lk2, lam_init)
        o = acc_ref[...] / l_ref[...]
        for h in range(heads):
            r0 = 2 * h * t_new
            oh = o[r0:r0 + t_new, :] - lam * o[r0 + t_new:r0 + 2 * t_new, :]
            o_ref[:, h * dv:(h + 1) * dv] = _rms(oh, sg_ref[...], SUBLN_EPS) * (1.0 - lam_init)


def _diff_sample_call(page_table, proj, kc, vc, lams, sg, layer, *, n_prompt, t_new, heads, dk, dv,
                      page, k_col, v_col, lam_init, n_group):
    bs, n_pages = page_table.shape
    w_q = heads * 2 * dk
    row0 = n_prompt // t_new
    lam_spec = pl.BlockSpec((1, dk), lambda b, p, pt: (0, 0))
    kern = functools.partial(_diff_sample_kernel, heads=heads, dk=dk, dv=dv, page=page, n_group=n_group,
                             lam_init=lam_init, scale=dk ** -0.5)

    def page_spec(rows, g):
        return pl.BlockSpec((None, None, rows, LANES), lambda b, p, pt: (layer, pt[b, p * n_group + g], 0, 0))

    grid_spec = pltpu.PrefetchScalarGridSpec(
        num_scalar_prefetch=1,
        grid=(bs, n_pages // n_group),
        in_specs=[
            pl.BlockSpec((t_new, w_q), lambda b, p, pt: (row0 + b, 0)),
            pl.BlockSpec((t_new, w_q), lambda b, p, pt: (row0 + b, k_col // w_q)),
            pl.BlockSpec((t_new, heads * dv), lambda b, p, pt: (row0 + b, v_col // (heads * dv))),
            *[page_spec(page * heads * 2, g) for g in range(n_group)],
            *[page_spec(page * heads * dv // LANES, g) for g in range(n_group)],
            lam_spec, lam_spec, lam_spec, lam_spec,
            pl.BlockSpec((1, dv), lambda b, p, pt: (0, 0)),
        ],
        out_specs=pl.BlockSpec((t_new, heads * dv), lambda b, p, pt: (b, 0)),
        scratch_shapes=[
            pltpu.VMEM((2 * heads * t_new, 1), F32), pltpu.VMEM((2 * heads * t_new, 1), F32),
            pltpu.VMEM((2 * heads * t_new, dv), F32),
        ],
    )
    return pl.pallas_call(
        kern,
        grid_spec=grid_spec,
        out_shape=jax.ShapeDtypeStruct((bs * t_new, heads * dv), F32),
        compiler_params=_params(("parallel", "arbitrary")),
        name="diff_sample",
    )(page_table, proj, proj, proj, *([kc] * n_group), *([vc] * n_group), *lams, sg)


def _logf_pages_kernel(pt_ref, lf_ref, mcs_ref, mbc_ref, msel_ref, lst_ref, cp_ref, ct_ref, g_ref,
                       *, n_pages, heads, page):
    b = pl.program_id(0)
    for p in range(n_pages):
        g_ref[pl.ds(p, 1), :] = lf_ref[pl.ds(pt_ref[b, p], 1), :]
    x = g_ref[...]
    y = _dot_sel(x, mcs_ref[...])
    tot = _dot_sel(y, mbc_ref[...])
    c = y + _sel_dot(lst_ref[...], tot)
    for h in range(heads):
        cp_ref[0, h * n_pages:(h + 1) * n_pages, :] = c[:, h * page:(h + 1) * page]
    ct_ref[0] = _dot_sel(c, msel_ref[...])[n_pages - 1:n_pages, :]


def _logf_pages_call(page_table, lf2, mats, *, heads, page):
    bs, n_pages = page_table.shape
    n_pool, w = lf2.shape
    mcs, mbc, msel, lst = mats
    full = lambda a: pl.BlockSpec(a.shape, lambda b, pt: (0,) * a.ndim)
    kern = functools.partial(_logf_pages_kernel, n_pages=n_pages, heads=heads, page=page)
    grid_spec = pltpu.PrefetchScalarGridSpec(
        num_scalar_prefetch=1,
        grid=(bs,),
        in_specs=[full(lf2), full(mcs), full(mbc), full(msel), full(lst)],
        out_specs=[pl.BlockSpec((1, heads * n_pages, page), lambda b, pt: (b, 0, 0)),
                   pl.BlockSpec((1, 1, LANES), lambda b, pt: (b, 0, 0))],
        scratch_shapes=[pltpu.VMEM((n_pages, w), F32)],
    )
    return pl.pallas_call(
        kern,
        grid_spec=grid_spec,
        out_shape=[jax.ShapeDtypeStruct((bs, heads * n_pages, page), F32),
                   jax.ShapeDtypeStruct((bs, 1, LANES), F32)],
        compiler_params=_params(("arbitrary",)),
        name="logf_pages",
    )(page_table, lf2, mcs, mbc, msel, lst)


def _fox_sample_kernel(pt_ref, q_ref, kn_ref, vn_ref, lfn_ref, *refs, heads, dh, page, n_pages, n_group, scale):
    del pt_ref
    kc_refs, vc_refs = refs[:n_group], refs[n_group:2 * n_group]
    cp_ref, ct_ref, o_ref, m_ref, l_ref, acc_ref, cn_ref = refs[2 * n_group:]
    p = pl.program_id(1)
    last = pl.num_programs(1) - 1
    t_new = q_ref.shape[0]

    @pl.when(p == 0)
    def _():
        m_ref[...] = jnp.full(m_ref.shape, NEG_INF, F32)
        l_ref[...] = jnp.zeros(l_ref.shape, F32)
        acc_ref[...] = jnp.zeros(acc_ref.shape, F32)
        x = lfn_ref[...]
        row = lax.broadcasted_iota(jnp.int32, x.shape, 0)
        k = 1
        while k < x.shape[0]:
            x = x + jnp.where(row >= k, pltpu.roll(x, k, 0), 0.0)
            k *= 2
        cn_ref[...] = ct_ref[0] + x

    q = q_ref[...]
    cn = cn_ref[...]
    qb = [q[:, h * dh:(h + 1) * dh].astype(BF16) for h in range(heads)]
    cq = [cn[:, h:h + 1] for h in range(heads)]

    s = jnp.concatenate([
        jnp.concatenate([_dot_nt(qb[h], kc[pl.ds(h, page, stride=heads), :].astype(BF16)) for kc in kc_refs],
                        axis=1) for h in range(heads)], axis=0) * scale
    bias = jnp.concatenate([
        cq[h] - jnp.concatenate([cp_ref[0, pl.ds(h * n_pages + p * n_group + g, 1), :] for g in range(n_group)],
                                axis=1) for h in range(heads)], axis=0)
    v_heads = [jnp.concatenate([vc[pl.ds(h, page, stride=heads), :] for vc in vc_refs], axis=0).astype(BF16)
               for h in range(heads)]
    _softmax_step_stacked(s + bias, v_heads, m_ref, l_ref, acc_ref, t_new)

    @pl.when(p == last)
    def _():
        kn = kn_ref[...]
        vn = vn_ref[...]
        s_new = jnp.concatenate([
            _dot_nt(qb[h], _pad_rows(kn[:, h * dh:(h + 1) * dh], page).astype(BF16)) for h in range(heads)],
            axis=0) * scale
        row = lax.broadcasted_iota(jnp.int32, (t_new, page), 0)
        col = lax.broadcasted_iota(jnp.int32, (t_new, page), 1)
        bias_new = jnp.concatenate([
            cq[h] - jnp.sum(jnp.where(row == col, cq[h], 0.0), axis=0, keepdims=True) for h in range(heads)],
            axis=0)
        v_new = [_pad_rows(vn[:, h * dh:(h + 1) * dh], page).astype(BF16) for h in range(heads)]
        _softmax_step_stacked(_causal_new(s_new + bias_new, t_new), v_new, m_ref, l_ref, acc_ref, t_new)
        o = acc_ref[...] / l_ref[...]
        for h in range(heads):
            o_ref[:, h * dh:(h + 1) * dh] = o[h * t_new:(h + 1) * t_new, :]


def _fox_sample_call(page_table, proj, logf, kc, vc, cpast, ctot, layer, *, n_prompt, t_new, heads, dh,
                     page, q_col, k_col, v_col, n_group):
    bs, n_pages = page_table.shape
    w = heads * dh
    row0 = n_prompt // t_new
    kern = functools.partial(_fox_sample_kernel, heads=heads, dh=dh, page=page, n_pages=n_pages,
                             n_group=n_group, scale=dh ** -0.5)

    def page_spec(g):
        return pl.BlockSpec((None, None, page * heads, dh),
                            lambda b, p, pt: (layer, pt[b, p * n_group + g], 0, 0))

    grid_spec = pltpu.PrefetchScalarGridSpec(
        num_scalar_prefetch=1,
        grid=(bs, n_pages // n_group),
        in_specs=[
            pl.BlockSpec((t_new, w), lambda b, p, pt: (row0 + b, q_col // w)),
            pl.BlockSpec((t_new, w), lambda b, p, pt: (row0 + b, k_col // w)),
            pl.BlockSpec((t_new, w), lambda b, p, pt: (row0 + b, v_col // w)),
            pl.BlockSpec((t_new, LANES), lambda b, p, pt: (row0 + b, 0)),
            *[page_spec(g) for g in range(n_group)],
            *[page_spec(g) for g in range(n_group)],
            pl.BlockSpec((1, heads * n_pages, page), lambda b, p, pt: (b, 0, 0)),
            pl.BlockSpec((1, 1, LANES), lambda b, p, pt: (b, 0, 0)),
        ],
        out_specs=pl.BlockSpec((t_new, w), lambda b, p, pt: (b, 0)),
        scratch_shapes=[
            pltpu.VMEM((heads * t_new, 1), F32), pltpu.VMEM((heads * t_new, 1), F32),
            pltpu.VMEM((heads * t_new, dh), F32), pltpu.VMEM((t_new, LANES), F32),
        ],
    )
    return pl.pallas_call(
        kern,
        grid_spec=grid_spec,
        out_shape=jax.ShapeDtypeStruct((bs * t_new, w), F32),
        compiler_params=_params(("parallel", "arbitrary")),
        name="fox_sample",
    )(page_table, proj, proj, proj, logf, *([kc] * n_group), *([vc] * n_group), cpast, ctot)


def _lru_coeffs(x, xprev_ext, gate_w, cw_ref, cb_ref, wga_ref, bga_ref, wgx_ref, bgx_ref, lam_ref, conv_w):
    rows = x.shape[0]
    nb = wga_ref.shape[0]
    bw = wga_ref.shape[1]
    u = xprev_ext(conv_w - 1) * cw_ref[0:1, :]
    for k in range(1, conv_w - 1):
        u = u + xprev_ext(conv_w - 1 - k) * cw_ref[k:k + 1, :]
    u = u + x * cw_ref[conv_w - 1:conv_w, :]
    u = u + cb_ref[...]
    ub = u.astype(BF16)
    r_parts, i_parts = [], []
    for n in range(nb):
        un = ub[:, n * bw:(n + 1) * bw]
        r_parts.append(jnp.dot(un, wga_ref[n], preferred_element_type=F32))
        i_parts.append(jnp.dot(un, wgx_ref[n], preferred_element_type=F32))
    r = _sigmoid(jnp.concatenate(r_parts, axis=-1) + bga_ref[...])
    i = _sigmoid(jnp.concatenate(i_parts, axis=-1) + bgx_ref[...])
    log_a = -LRU_C * r * _softplus(-lam_ref[...])
    a = jnp.exp(log_a)
    b = jnp.sqrt(-jnp.tanh(log_a) * (a * a + 1.0)) * (i * u)
    del rows, gate_w
    return a, b


def _group_scan(a, b):
    row = lax.broadcasted_iota(jnp.int32, a.shape, 0) % SUBLANES
    k = 1
    while k < SUBLANES:
        keep = row >= k
        a_sh = jnp.where(keep, pltpu.roll(a, k, 0), 1.0)
        b_sh = jnp.where(keep, pltpu.roll(b, k, 0), 0.0)
        b = a * b_sh + b
        a = a * a_sh
        k *= 2
    return a, b


def _gelu_tanh(x):
    return 0.5 * x * (1.0 + jnp.tanh(math.sqrt(2.0 / math.pi) * (x + 0.044715 * (x * x * x))))


def _lru_prompt_kernel(x_ref, g_ref, cw_ref, cb_ref, wga_ref, bga_ref, wgx_ref, bgx_ref, lam_ref,
                       y_ref, ht_ref, tail_ref, hl_ref, a_ref, b_ref, hs_ref, *, conv_w):
    ti = pl.program_id(1)

    @pl.when(ti == 0)
    def _():
        tail_ref[...] = jnp.zeros(tail_ref.shape, F32)
        hl_ref[...] = jnp.zeros(hl_ref.shape, F32)

    x = x_ref[...]
    tt = x.shape[0]
    xe = jnp.concatenate([tail_ref[...], x], axis=0)
    delayed = lambda k: pltpu.roll(xe, k, 0)[SUBLANES:, :]
    a, b = _lru_coeffs(x, delayed, None, cw_ref, cb_ref, wga_ref, bga_ref, wgx_ref, bgx_ref, lam_ref, conv_w)
    a, b = _group_scan(a, b)
    a_ref[...] = a
    b_ref[...] = b
    tail_ref[...] = x[tt - SUBLANES:, :]

    def body(g, hlast):
        r0 = pl.multiple_of(g * SUBLANES, SUBLANES)
        hg = b_ref[pl.ds(r0, SUBLANES), :] + a_ref[pl.ds(r0, SUBLANES), :] * hlast
        hs_ref[pl.ds(r0, SUBLANES), :] = hg
        return hg[SUBLANES - 1:SUBLANES, :]

    hl = lax.fori_loop(0, tt // SUBLANES, body, hl_ref[...])
    hl_ref[...] = hl
    ht_ref[0] = hl
    y_ref[...] = (_gelu_tanh(g_ref[...]) * hs_ref[...]).astype(y_ref.dtype)


def _lru_prompt_call(proj, wts, *, batch, seq, wc, tt, g_col, x_col):
    cw, cb, wga, bga, wgx, bgx, lam = wts
    nt = seq // tt
    conv_w = cw.shape[0]
    full = lambda a: pl.BlockSpec(a.shape, lambda b, t: (0,) * a.ndim)
    kern = functools.partial(_lru_prompt_kernel, conv_w=conv_w)
    return pl.pallas_call(
        kern,
        grid=(batch, nt),
        in_specs=[
            pl.BlockSpec((tt, wc), lambda b, t: (b * nt + t, x_col // wc)),
            pl.BlockSpec((tt, wc), lambda b, t: (b * nt + t, g_col // wc)),
            full(cw), full(cb), full(wga), full(bga), full(wgx), full(bgx), full(lam),
        ],
        out_specs=[pl.BlockSpec((tt, wc), lambda b, t: (b * nt + t, 0)),
                   pl.BlockSpec((1, 1, wc), lambda b, t: (b, 0, 0))],
        out_shape=[jax.ShapeDtypeStruct((batch * seq, wc), BF16),
                   jax.ShapeDtypeStruct((batch, 1, wc), F32)],
        scratch_shapes=[pltpu.VMEM((SUBLANES, wc), F32), pltpu.VMEM((1, wc), F32),
                        pltpu.VMEM((tt, wc), F32), pltpu.VMEM((tt, wc), F32), pltpu.VMEM((tt, wc), F32)],
        compiler_params=_params(("parallel", "arbitrary")),
        name="lru_prompt",
    )(proj, proj, cw, cb, wga, bga, wgx, bgx, lam)


def _lru_sample_kernel(x_ref, g_ref, prev_ref, h0_ref, cw_ref, cb_ref, wga_ref, bga_ref, wgx_ref, bgx_ref,
                       lam_ref, y_ref, hs_ref, *, conv_w):
    x = x_ref[...]
    rows = x.shape[0]
    row = lax.broadcasted_iota(jnp.int32, x.shape, 0) % SUBLANES
    prev = prev_ref[...]

    def delayed(k):
        return jnp.where(row >= k, pltpu.roll(x, k, 0), pltpu.roll(prev, (k - SUBLANES) % rows, 0))

    a, b = _lru_coeffs(x, delayed, None, cw_ref, cb_ref, wga_ref, bga_ref, wgx_ref, bgx_ref, lam_ref, conv_w)
    a, b = _group_scan(a, b)
    hs = b + a * h0_ref[...]
    hs_ref[...] = hs
    y_ref[...] = _gelu_tanh(g_ref[...]) * hs


def _lru_sample_call(proj, prev8, h0rep, wts, *, n_prompt, n_sample, wc, g_col, x_col):
    cw, cb, wga, bga, wgx, bgx, lam = wts
    conv_w = cw.shape[0]
    rb = n_prompt // n_sample
    full = lambda a: pl.BlockSpec(a.shape, lambda i: (0,) * a.ndim)
    kern = functools.partial(_lru_sample_kernel, conv_w=conv_w)
    return pl.pallas_call(
        kern,
        grid=(1,),
        in_specs=[
            pl.BlockSpec((n_sample, wc), lambda i: (rb, x_col // wc)),
            pl.BlockSpec((n_sample, wc), lambda i: (rb, g_col // wc)),
            full(prev8), full(h0rep),
            full(cw), full(cb), full(wga), full(bga), full(wgx), full(bgx), full(lam),
        ],
        out_specs=[pl.BlockSpec((n_sample, wc), lambda i: (0, 0)),
                   pl.BlockSpec((n_sample, wc), lambda i: (0, 0))],
        out_shape=[jax.ShapeDtypeStruct((n_sample, wc), F32), jax.ShapeDtypeStruct((n_sample, wc), F32)],
        compiler_params=_params(("arbitrary",)),
        name="lru_sample",
    )(proj, proj, prev8, h0rep, cw, cb, wga, bga, wgx, bgx, lam)


def _mix_kernel(ap_ref, bp_ref, cp_ref, as_ref, bs_ref, cs_ref, w_ref, x_ref, g_ref, *rest,
                n_experts, n_prompt_tiles):
    if n_experts:
        wrh_ref, wrl_ref, x1_ref, rg_ref, re_ref = rest
    else:
        x1_ref, hn_ref = rest
    i = pl.program_id(0)
    wa, wb = ap_ref.shape[1], bp_ref.shape[1]

    def project(a_ref, b_ref, c_ref):
        d = lambda v, lo, hi: jnp.dot(v[...].astype(BF16), w_ref[lo:hi, :], preferred_element_type=F32)
        return d(a_ref, 0, wa) + d(b_ref, wa, wa + wb) + d(c_ref, wa + wb, w_ref.shape[0])

    @pl.when(i < n_prompt_tiles)
    def _():
        x1_ref[...] = x_ref[...] + project(ap_ref, bp_ref, cp_ref)

    @pl.when(i >= n_prompt_tiles)
    def _():
        x1_ref[...] = x_ref[...] + project(as_ref, bs_ref, cs_ref)

    hn = _rms(x1_ref[...], g_ref[...], RMS_EPS)
    if not n_experts:
        hn_ref[...] = hn.astype(BF16)
    else:
        h_hi = hn.astype(BF16)
        h_lo = (hn - h_hi.astype(F32)).astype(BF16)
        d = lambda a, b: jnp.dot(a, b[...], preferred_element_type=F32)
        logits = d(h_hi, wrh_ref) + (d(h_lo, wrh_ref) + d(h_hi, wrl_ref))
        lane = lax.broadcasted_iota(jnp.int32, logits.shape, 1)
        lane_f = lane.astype(F32)
        big = float(LANES)
        lg = jnp.where(lane < n_experts, logits, NEG_INF)
        v1 = jnp.max(lg, axis=-1, keepdims=True)
        i1 = jnp.min(jnp.where(lg == v1, lane_f, big), axis=-1, keepdims=True)
        lg2 = jnp.where(lane_f == i1, NEG_INF, lg)
        v2 = jnp.max(lg2, axis=-1, keepdims=True)
        i2 = jnp.min(jnp.where(lg2 == v2, lane_f, big), axis=-1, keepdims=True)
        e = jnp.exp(v2 - v1)
        g1 = 1.0 / (1.0 + e)
        g2 = e / (1.0 + e)
        rg_ref[...] = jnp.where(lane == 0, g1, jnp.where(lane == 1, g2, 0.0))
        re_ref[...] = jnp.where(lane == 0, i1, jnp.where(lane == 1, i2, 0.0)).astype(jnp.int32)


def _mix_call(branches_p, branches_s, w_out, x, g, w_router, *, tm):
    n, d = x.shape
    npt = branches_p[0].shape[0] // tm
    n_experts = 0 if w_router is None else w_router.shape[1]
    kern = functools.partial(_mix_kernel, n_experts=n_experts, n_prompt_tiles=npt)
    row = lambda c: pl.BlockSpec((tm, c), lambda i: (i, 0))
    row_p = lambda a: pl.BlockSpec((tm, a.shape[1]), lambda i: (jnp.minimum(i, npt - 1), 0))
    row_s = lambda a: pl.BlockSpec((tm, a.shape[1]), lambda i: (jnp.maximum(i - npt, 0), 0))
    in_specs = ([row_p(a) for a in branches_p] + [row_s(a) for a in branches_s]
                + [pl.BlockSpec((d, d), lambda i: (0, 0)), row(d), pl.BlockSpec((1, d), lambda i: (0, 0))])
    args = [*branches_p, *branches_s, w_out, x, g]
    if n_experts:
        wr = jnp.zeros((d, LANES), F32).at[:, :n_experts].set(w_router)
        wr_hi = wr.astype(BF16)
        wr_lo = (wr - wr_hi.astype(F32)).astype(BF16)
        in_specs += [pl.BlockSpec((d, LANES), lambda i: (0, 0))] * 2
        out_specs = [row(d), row(LANES), row(LANES)]
        out_shape = [jax.ShapeDtypeStruct((n, d), F32), jax.ShapeDtypeStruct((n, LANES), F32),
                     jax.ShapeDtypeStruct((n, LANES), jnp.int32)]
        args += [wr_hi, wr_lo]
    else:
        out_specs = [row(d), row(d)]
        out_shape = [jax.ShapeDtypeStruct((n, d), F32), jax.ShapeDtypeStruct((n, d), BF16)]
    return pl.pallas_call(
        kern,
        grid=(n // tm,),
        in_specs=in_specs,
        out_specs=out_specs,
        out_shape=out_shape,
        compiler_params=_params(("parallel",)),
        name="mix_router" if n_experts else "mix",
    )(*args)


def _swiglu_acc(h, wg, wu, wd):
    g = jnp.dot(h, wg, preferred_element_type=F32)
    u = jnp.dot(h, wu, preferred_element_type=F32)
    a = (g * _sigmoid(g) * u).astype(BF16)
    return jnp.dot(a, wd, preferred_element_type=F32)


def _ffn_kernel(h_ref, wg_ref, wu_ref, wd_ref, x_ref, o_ref):
    j = pl.program_id(1)

    @pl.when(j == 0)
    def _():
        o_ref[...] = x_ref[...]

    o_ref[...] += _swiglu_acc(h_ref[...], wg_ref[...], wu_ref[...], wd_ref[...])


def _ffn_call(hn, wg, wu, wd, x1, *, tm, tf):
    n, d = x1.shape
    f = wg.shape[1]
    return pl.pallas_call(
        _ffn_kernel,
        grid=(n // tm, f // tf),
        in_specs=[
            pl.BlockSpec((tm, d), lambda i, j: (i, 0)),
            pl.BlockSpec((d, tf), lambda i, j: (0, j)),
            pl.BlockSpec((d, tf), lambda i, j: (0, j)),
            pl.BlockSpec((tf, d), lambda i, j: (j, 0)),
            pl.BlockSpec((tm, d), lambda i, j: (i, 0)),
        ],
        out_specs=pl.BlockSpec((tm, d), lambda i, j: (i, 0)),
        out_shape=jax.ShapeDtypeStruct((n, d), F32),
        compiler_params=_params(("parallel", "arbitrary")),
        name="ffn_dense",
    )(hn, wg, wu, wd, x1)


def _gather_norm_kernel(rt_ref, nv_ref, x_hbm, g_ref, o_ref, buf_ref, sem, *, tg):
    t = pl.program_id(0)

    def row_copy(r, tok):
        return pltpu.make_async_copy(x_hbm.at[pl.ds(tok, 1), :], buf_ref.at[pl.ds(r, 1), :], sem)

    @pl.when(t < nv_ref[0])
    def _():
        def issue(r, carry):
            row_copy(r, rt_ref[t * tg + r]).start()
            return carry

        lax.fori_loop(0, tg, issue, 0, unroll=8)
        pltpu.make_async_copy(buf_ref, buf_ref, sem).wait()
        o_ref[...] = _rms(buf_ref[...], g_ref[...], RMS_EPS).astype(BF16)

    @pl.when(t >= nv_ref[0])
    def _():
        o_ref[...] = jnp.zeros(o_ref.shape, o_ref.dtype)


def _gather_norm_call(row_token, n_valid, x1, g, *, tg):
    r = row_token.shape[0]
    d = x1.shape[1]
    grid_spec = pltpu.PrefetchScalarGridSpec(
        num_scalar_prefetch=2,
        grid=(r // tg,),
        in_specs=[pl.BlockSpec(memory_space=pl.ANY), pl.BlockSpec((1, d), lambda t, rt, nv: (0, 0))],
        out_specs=pl.BlockSpec((tg, d), lambda t, rt, nv: (t, 0)),
        scratch_shapes=[pltpu.VMEM((tg, d), F32), pltpu.SemaphoreType.DMA(())],
    )
    return pl.pallas_call(
        functools.partial(_gather_norm_kernel, tg=tg),
        grid_spec=grid_spec,
        out_shape=jax.ShapeDtypeStruct((r, d), BF16),
        compiler_params=_params(("arbitrary",)),
        name="moe_gather",
    )(row_token, n_valid, x1, g)


def _moe_ffn_kernel(te_ref, nv_ref, h_ref, wg_ref, wu_ref, wd_ref, o_ref):
    del te_ref
    t = pl.program_id(0)
    j = pl.program_id(1)

    @pl.when(j == 0)
    def _():
        o_ref[...] = jnp.zeros(o_ref.shape, F32)

    @pl.when(t < nv_ref[0])
    def _():
        o_ref[...] += _swiglu_acc(h_ref[...], wg_ref[...], wu_ref[...], wd_ref[...])


def _moe_ffn_call(tile_expert, n_valid, xs, wg, wu, wd, *, tm, tf):
    r, d = xs.shape
    f = wg.shape[2]
    nj = f // tf

    def tile(t, nv):
        return jnp.minimum(t, nv[0] - 1)

    def fblk(t, j, nv):
        return jnp.where(t < nv[0], j, nj - 1)

    grid_spec = pltpu.PrefetchScalarGridSpec(
        num_scalar_prefetch=2,
        grid=(r // tm, nj),
        in_specs=[
            pl.BlockSpec((tm, d), lambda t, j, te, nv: (tile(t, nv), 0)),
            pl.BlockSpec((None, d, tf), lambda t, j, te, nv: (te[t], 0, fblk(t, j, nv))),
            pl.BlockSpec((None, d, tf), lambda t, j, te, nv: (te[t], 0, fblk(t, j, nv))),
            pl.BlockSpec((None, tf, d), lambda t, j, te, nv: (te[t], fblk(t, j, nv), 0)),
        ],
        out_specs=pl.BlockSpec((tm, d), lambda t, j, te, nv: (t, 0)),
    )
    return pl.pallas_call(
        _moe_ffn_kernel,
        grid_spec=grid_spec,
        out_shape=jax.ShapeDtypeStruct((r, d), F32),
        compiler_params=_params(("arbitrary", "arbitrary")),
        name="moe_ffn",
    )(tile_expert, n_valid, xs, wg, wu, wd)


def _combine_norm_kernel(pos_ref, x_ref, rg_ref, ys_hbm, g_ref, o_ref, buf_ref, sem, *, tc, final_norm):
    i = pl.program_id(0)

    def row_copy(k, r, src):
        return pltpu.make_async_copy(ys_hbm.at[pl.ds(src, 1), :], buf_ref.at[k, pl.ds(r, 1), :], sem)

    def issue(r, carry):
        for k in range(TOP_K):
            row_copy(k, r, pos_ref[(i * tc + r) * TOP_K + k]).start()
        return carry

    lax.fori_loop(0, tc, issue, 0, unroll=4)
    pltpu.make_async_copy(buf_ref, buf_ref, sem).wait()
    rg = rg_ref[...]
    moe = rg[:, 0:1] * buf_ref[0]
    for k in range(1, TOP_K):
        moe = moe + rg[:, k:k + 1] * buf_ref[k]
    x2 = x_ref[...] + moe
    o_ref[...] = _rms(x2, g_ref[...], RMS_EPS) if final_norm else x2


def _combine_norm_call(pos, x1, rg, ys, g, *, tc, final_norm):
    n, d = x1.shape
    grid_spec = pltpu.PrefetchScalarGridSpec(
        num_scalar_prefetch=1,
        grid=(n // tc,),
        in_specs=[
            pl.BlockSpec((tc, d), lambda i, ps: (i, 0)),
            pl.BlockSpec((tc, LANES), lambda i, ps: (i, 0)),
            pl.BlockSpec(memory_space=pl.ANY),
            pl.BlockSpec((1, d), lambda i, ps: (0, 0)),
        ],
        out_specs=pl.BlockSpec((tc, d), lambda i, ps: (i, 0)),
        scratch_shapes=[pltpu.VMEM((TOP_K, tc, d), F32), pltpu.SemaphoreType.DMA(())],
    )
    return pl.pallas_call(
        functools.partial(_combine_norm_kernel, tc=tc, final_norm=final_norm),
        grid_spec=grid_spec,
        out_shape=jax.ShapeDtypeStruct((n, d), F32),
        compiler_params=_params(("arbitrary",)),
        name="moe_combine",
    )(pos, x1, rg, ys, g)


def _norm_kernel(x_ref, g_ref, o_ref):
    o_ref[...] = _rms(x_ref[...], g_ref[...], RMS_EPS)


def _norm_call(x, g, *, tm):
    n, d = x.shape
    return pl.pallas_call(
        _norm_kernel,
        grid=(n // tm,),
        in_specs=[pl.BlockSpec((tm, d), lambda i: (i, 0)), pl.BlockSpec((1, d), lambda i: (0, 0))],
        out_specs=pl.BlockSpec((tm, d), lambda i: (i, 0)),
        out_shape=jax.ShapeDtypeStruct((n, d), F32),
        compiler_params=_params(("parallel",)),
        name="final_norm",
    )(x, g)


def _route_plan(re, n_experts, tm):
    n = re.shape[0]
    e_flat = re.reshape(-1)
    oh = (e_flat[:, None] == jnp.arange(n_experts, dtype=jnp.int32)[None, :]).astype(jnp.int32)
    csum = jnp.cumsum(oh, axis=0)
    rank = jnp.sum((csum - oh) * oh, axis=1)
    counts = csum[-1]
    padded = ((counts + tm - 1) // tm) * tm
    ends = jnp.cumsum(padded)
    offs = ends - padded
    pos = jnp.sum(oh * offs[None, :], axis=1) + rank
    n_rows = ((n * TOP_K + tm - 1) // tm) * tm + n_experts * tm
    row_token = jnp.zeros((n_rows,), jnp.int32).at[pos].set(jnp.arange(n * TOP_K, dtype=jnp.int32) // TOP_K)
    n_tiles = n_rows // tm
    n_valid = (ends[-1] // tm).astype(jnp.int32)
    tile_start = jnp.arange(n_tiles, dtype=jnp.int32) * tm
    tile_e = jnp.sum((tile_start[:, None] >= ends[None, :]).astype(jnp.int32), axis=1)
    last_e = jnp.sum((tile_start[n_valid - 1] >= ends).astype(jnp.int32))
    tile_e = jnp.where(jnp.arange(n_tiles) < n_valid, tile_e, last_e).astype(jnp.int32)
    return pos.astype(jnp.int32), row_token, tile_e, n_valid.reshape(1)


def _tile(pref, *dims):
    t = pref
    for dim in dims:
        t = math.gcd(t, dim)
    return t


def _rope_tables(seq, n_past, t_new, bs, dk, rot_dim):
    half = rot_dim // 2
    inv = ROPE_THETA ** (-(jnp.arange(half, dtype=F32) * 2.0) / rot_dim)
    pos = jnp.concatenate([jnp.arange(seq), jnp.tile(n_past + jnp.arange(t_new), bs)])
    ang = pos.astype(F32)[:, None] * inv[None, :]
    cos, sin = jnp.cos(ang), jnp.sin(ang)
    n = pos.shape[0]
    ones = jnp.ones((n, dk - rot_dim), F32)
    zeros_h = jnp.zeros((n, half), F32)
    zeros_r = jnp.zeros((n, dk - rot_dim), F32)
    cos_t = jnp.concatenate([cos, cos, ones], axis=1)
    sa_t = jnp.concatenate([zeros_h, sin, zeros_r], axis=1)
    sb_t = jnp.concatenate([-sin, zeros_h, zeros_r], axis=1)
    return cos_t, sa_t, sb_t


def _page_mats(heads, page, n_pages):
    w = heads * page
    dst = jnp.arange(w)
    d_head, d_slot = dst // page, dst % page
    same = d_head[:, None] == d_head[None, :]
    mcs = (same & (d_slot[:, None] <= d_slot[None, :])).astype(BF16)
    mbc = ((d_head[:, None] == d_head[None, :]) & (d_slot[:, None] == page - 1)).astype(BF16)
    msel = ((d_slot[:, None] == page - 1) & (d_head[:, None] == jnp.arange(LANES)[None, :])).astype(BF16)
    pg = jnp.arange(n_pages)
    lst = (pg[None, :] < pg[:, None]).astype(BF16)
    return mcs, mbc, msel, lst


def kernel(x_prompt, x_sample, cache_k_diff, cache_v_diff, cache_k_fox, cache_v_fox, cache_logf_fox, state_rglru_h, state_rglru_conv, page_table, norm_mix_g, w_in, b_fgate, lambda_q1, lambda_k1, lambda_q2, lambda_k2, subln_g, conv_w, conv_b, w_gate_a, b_gate_a, w_gate_x, b_gate_x, lru_lambda, w_out, norm_ffn_g, w_dense_gate, w_dense_up, w_dense_down, w_router, w_moe_gate, w_moe_up, w_moe_down, norm_final_g):
    bp, seq, d = x_prompt.shape
    bs, t_new, _ = x_sample.shape
    depth = w_in.shape[0]
    n_pool, page, h_a = cache_k_diff.shape[1], cache_k_diff.shape[2], cache_k_diff.shape[3]
    dk_a = cache_k_diff.shape[5]
    dv_a = cache_v_diff.shape[4]
    h_b, dh_b = cache_k_fox.shape[3], cache_k_fox.shape[4]
    w_c = state_rglru_h.shape[2]
    cw_len = conv_w.shape[1]
    n_pages = page_table.shape[1]
    n_past = n_pages * page
    rot_dim = dk_a // 4
    n_experts = w_router.shape[2]
    n_p, n_s = bp * seq, bs * t_new
    n = n_p + n_s
    assert t_new == SUBLANES and page == LANES and dv_a == 2 * dk_a and dk_a == LANES and dh_b == LANES

    w_qa = h_a * 2 * dk_a
    w_va = h_a * dv_a
    w_b = h_b * dh_b
    col_qa, col_ka, col_va = 0, w_qa, 2 * w_qa
    col_qb = col_va + w_va
    col_kb, col_vb = col_qb + w_b, col_qb + 2 * w_b
    col_g = col_vb + w_b
    col_x = col_g + w_c
    n_main = col_x + w_c
    src_fl = 2 * w_qa + w_va + 3 * w_b

    tm = _tile(TM_TOKENS, seq, n_s)
    tn = _tile(TN_PROJ, w_qa, src_fl, n_main - src_fl)
    tq = _tile(TQ_ATTN, seq)
    tt = _tile(TT_LRU, seq)
    tm_mix = _tile(TM_MIX, tm)
    tf = _tile(TF_FFN, w_dense_gate.shape[2])
    tm_moe = _tile(TM_MOE, n)
    tc = _tile(TC_COMBINE, n)

    rope = _rope_tables(seq, n_past, t_new, bs, dk_a, rot_dim)
    tri = jnp.tril(jnp.ones((seq, seq), BF16))
    page_mats = _page_mats(h_b, page, n_pages)

    kc_d = cache_k_diff.reshape(depth, n_pool, page * h_a * 2, dk_a)
    nc_a = dv_a // LANES
    vc_d = jnp.swapaxes(cache_v_diff.reshape(depth, n_pool, page, h_a, nc_a, LANES), 3, 4).reshape(
        depth, n_pool, page * nc_a * h_a, LANES)
    kc_f = cache_k_fox.reshape(depth, n_pool, page * h_b, dh_b)
    vc_f = cache_v_fox.reshape(depth, n_pool, page * h_b, dh_b)
    lf_pages = jnp.swapaxes(cache_logf_fox, 2, 3).reshape(depth, n_pool, h_b * page)
    n_group = _tile(PAGES_PER_STEP, n_pages)

    x = jnp.concatenate([x_prompt.reshape(n_p, d), x_sample.reshape(n_s, d)], axis=0)
    row2 = lambda v: v.reshape(1, -1).astype(F32)
    st = {k: [] for k in ('kd', 'vd', 'kf', 'vf', 'lf', 'h_p', 'h_s', 'cv')}
    y_all = None

    for l in range(depth):
        lam_init = 0.8 - 0.6 * math.exp(-0.3 * l)
        w_l = w_in[l]
        w_b2 = w_l[:, src_fl + h_b:].astype(BF16)
        b_fl = jnp.zeros((1, LANES), F32).at[0, :h_b].set(b_fgate[l])
        proj, logf = _proj_call(
            x, row2(norm_mix_g[l]), w_in, l, src_fl, w_b2, b_fl, rope, tm=tm, tn=tn, n_rope_cols=2 * w_qa,
            n_prompt_tiles=n_p // tm, rope_prompt_tiles=seq // tm, rot_half=rot_dim // 2)

        lams = (row2(lambda_q1[l]), row2(lambda_k1[l]), row2(lambda_q2[l]), row2(lambda_k2[l]))
        sg = row2(subln_g[l])

        oa_p = _diff_prompt_call(proj, lams, sg, batch=bp, seq=seq, heads=h_a, dk=dk_a, dv=dv_a, tq=tq,
                                 k_col=col_ka, v_col=col_va, lam_init=lam_init)
        c_col = _cumsum_call(logf, tri, bp, seq)
        c_row = jnp.swapaxes(c_col[:, :h_b].reshape(bp, seq, h_b), 1, 2).reshape(bp * h_b, seq // tq, tq)
        ob_p = _fox_prompt_call(proj, c_col, c_row, batch=bp, seq=seq, heads=h_b, dh=dh_b, tq=tq,
                                q_col=col_qb, k_col=col_kb, v_col=col_vb)
        lru_w = (conv_w[l], row2(conv_b[l]), w_gate_a[l].astype(BF16), row2(b_gate_a[l]),
                 w_gate_x[l].astype(BF16), row2(b_gate_x[l]), row2(lru_lambda[l]))
        oc_p, ht_p = _lru_prompt_call(proj, lru_w, batch=bp, seq=seq, wc=w_c, tt=tt, g_col=col_g, x_col=col_x)

        oa_s = _diff_sample_call(page_table, proj, kc_d, vc_d, lams, sg, l, n_prompt=n_p, t_new=t_new,
                                 heads=h_a, dk=dk_a, dv=dv_a, page=page, k_col=col_ka, v_col=col_va,
                                 lam_init=lam_init, n_group=n_group)
        cpast, ctot = _logf_pages_call(page_table, lf_pages[l], page_mats, heads=h_b, page=page)
        ob_s = _fox_sample_call(page_table, proj, logf, kc_f, vc_f, cpast, ctot, l, n_prompt=n_p,
                                t_new=t_new, heads=h_b, dh=dh_b, page=page, q_col=col_qb, k_col=col_kb,
                                v_col=col_vb, n_group=n_group)
        prev8 = jnp.pad(state_rglru_conv[l], ((0, 0), (SUBLANES - (cw_len - 1), 0), (0, 0))).reshape(n_s, w_c)
        h0rep = jnp.repeat(state_rglru_h[l], t_new, axis=0)
        oc_s, hs_s = _lru_sample_call(proj, prev8, h0rep, lru_w, n_prompt=n_p, n_sample=n_s, wc=w_c,
                                      g_col=col_g, x_col=col_x)

        br_p, br_s = (oa_p, ob_p, oc_p), (oa_s, ob_s, oc_s)
        g_ffn = row2(norm_ffn_g[l])
        m = l // 2
        if l % 2 == 0:
            x1, hn = _mix_call(br_p, br_s, w_out[l].astype(BF16), x, g_ffn, None, tm=tm_mix)
            x = _ffn_call(hn, w_dense_gate[m].astype(BF16), w_dense_up[m].astype(BF16),
                          w_dense_down[m].astype(BF16), x1, tm=tm, tf=tf)
            if l == depth - 1:
                y_all = _norm_call(x, row2(norm_final_g), tm=tm_mix)
        else:
            x1, rg, re = _mix_call(br_p, br_s, w_out[l].astype(BF16), x, g_ffn, w_router[m], tm=tm_mix)
            pos, row_token, tile_e, n_valid = _route_plan(re[:, :TOP_K], n_experts, tm_moe)
            xs = _gather_norm_call(row_token, n_valid, x1, g_ffn, tg=tm_moe)
            ys = _moe_ffn_call(tile_e, n_valid, xs, w_moe_gate[m].astype(BF16), w_moe_up[m].astype(BF16),
                               w_moe_down[m].astype(BF16), tm=tm_moe,
                               tf=_tile(TF_MOE, w_moe_gate.shape[3]))
            final = l == depth - 1
            x = _combine_norm_call(pos, x1, rg, ys, row2(norm_final_g), tc=tc, final_norm=final)
            if final:
                y_all = x

        st['kd'].append(proj[:, col_ka:col_ka + w_qa])
        st['vd'].append(proj[:, col_va:col_va + w_va])
        st['kf'].append(proj[:, col_kb:col_kb + w_b])
        st['vf'].append(proj[:, col_vb:col_vb + w_b])
        st['lf'].append(logf[:, :h_b])
        st['h_p'].append(ht_p.reshape(bp, w_c))
        st['h_s'].append(hs_s.reshape(bs, t_new, w_c)[:, t_new - 1])
        st['cv'].append(proj[:, col_x:col_x + w_c])

    def both(name, shape_tail):
        a = jnp.stack(st[name])
        return (a[:, :n_p].reshape((depth, bp, seq) + shape_tail),
                a[:, n_p:].reshape((depth, bs, t_new) + shape_tail))

    kd_p, kd_s = both('kd', (h_a, 2, dk_a))
    vd_p, vd_s = both('vd', (h_a, dv_a))
    kf_p, kf_s = both('kf', (h_b, dh_b))
    vf_p, vf_s = both('vf', (h_b, dh_b))
    lf_p, lf_s = both('lf', (h_b,))
    cv_p, cv_s = both('cv', (w_c,))
    keep = cw_len - 1
    return (y_all[:n_p].reshape(bp, seq, d), y_all[n_p:].reshape(bs, t_new, d),
            kd_p, vd_p, kf_p, vf_p, lf_p, jnp.stack(st['h_p']), cv_p[:, :, seq - keep:],
            kd_s, vd_s, kf_s, vf_s, lf_s, jnp.stack(st['h_s']), cv_s[:, :, t_new - keep:])
```

```python
import functools
import math

import jax
import jax.numpy as jnp
from jax import lax
from jax.experimental import pallas as pl
from jax.experimental.pallas import tpu as pltpu

F32 = jnp.float32
BF16 = jnp.bfloat16

LANES = 128
SUBLANES = 8
VMEM_LIMIT = 56 * 1024 * 1024

TM_TOKENS = 1024
TN_PROJ = 512
TQ_ATTN = 512
TT_LRU = 256
TM_MIX = 512
TF_FFN = 256
TF_MOE = 512
TM_MOE = 512
TC_COMBINE = 256
PAGES_PER_STEP = 8

RMS_EPS = 1e-6
SUBLN_EPS = 1e-5
LRU_C = 8.0
ROPE_THETA = 500000.0
TOP_K = 2
NEG_INF = float("-inf")
LOG2E = 1.4426950408889634


def _params(sem, vmem=VMEM_LIMIT):
    return pltpu.CompilerParams(dimension_semantics=sem, vmem_limit_bytes=vmem)


def _split3(x):
    hi = x.astype(BF16)
    r = x - hi.astype(F32)
    mid = r.astype(BF16)
    lo = (r - mid.astype(F32)).astype(BF16)
    return hi, mid, lo


def _dot_sel(x, m):
    hi, mid, lo = _split3(x)
    d = lambda a: jnp.dot(a, m, preferred_element_type=F32)
    return d(hi) + d(mid) + d(lo)


def _sel_dot(m, x):
    hi, mid, lo = _split3(x)
    d = lambda a: jnp.dot(m, a, preferred_element_type=F32)
    return d(hi) + d(mid) + d(lo)


def _dot_nt(a, b):
    return lax.dot_general(a, b, (((1,), (1,)), ((), ())), preferred_element_type=F32)


def _sigmoid(x):
    return 1.0 / (1.0 + jnp.exp(-x))


def _softplus(x):
    return jnp.maximum(x, 0.0) + jnp.log1p(jnp.exp(-jnp.abs(x)))


def _rms(x, g, eps):
    ms = jnp.mean(x * x, axis=-1, keepdims=True)
    return x * lax.rsqrt(ms + eps) * g


def _proj_kernel(x_ref, g_ref, wa_ref, wb_ref, wfl_ref, bf_ref, cos_ref, sa_ref, sb_ref,
                 proj_ref, logf_ref, hn_ref, *, n_rope_tiles, n_a_tiles, tn, rot_half):
    j = pl.program_id(1)

    @pl.when(j == 0)
    def _():
        hn = _rms(x_ref[...], g_ref[...], RMS_EPS).astype(BF16)
        hn_ref[...] = hn
        z = jnp.dot(hn, wfl_ref[...].astype(BF16), preferred_element_type=F32) + bf_ref[...]
        logf_ref[...] = jnp.minimum(z, 0.0) - jnp.log1p(jnp.exp(-jnp.abs(z)))

    @pl.when(j < n_rope_tiles)
    def _():
        y = jnp.dot(hn_ref[...], wa_ref[...].astype(BF16), preferred_element_type=F32)
        c, sa, sb = cos_ref[...], sa_ref[...], sb_ref[...]
        for k in range(tn // LANES):
            yc = y[:, k * LANES:(k + 1) * LANES]
            proj_ref[:, k * LANES:(k + 1) * LANES] = (
                yc * c + pltpu.roll(yc, rot_half, 1) * sa + pltpu.roll(yc, LANES - rot_half, 1) * sb)

    @pl.when((j >= n_rope_tiles) & (j < n_a_tiles))
    def _():
        proj_ref[...] = jnp.dot(hn_ref[...], wa_ref[...].astype(BF16), preferred_element_type=F32)

    @pl.when(j >= n_a_tiles)
    def _():
        proj_ref[...] = jnp.dot(hn_ref[...], wb_ref[...], preferred_element_type=F32)


def _proj_call(x, g, w_in, layer, fl_col, wb, bfl, rope, *, tm, tn, n_rope_cols, n_prompt_tiles,
               rope_prompt_tiles, rot_half):
    n, d = x.shape
    na, nb = fl_col // tn, wb.shape[1] // tn
    n_main = (na + nb) * tn
    cos_t, sa_t, sb_t = rope
    rope_map = lambda i, j: (jnp.where(i < n_prompt_tiles, i % rope_prompt_tiles,
                                       rope_prompt_tiles + i - n_prompt_tiles), 0)
    kern = functools.partial(_proj_kernel, n_rope_tiles=n_rope_cols // tn, n_a_tiles=na, tn=tn,
                             rot_half=rot_half)
    return pl.pallas_call(
        kern,
        grid=(n // tm, n_main // tn),
        in_specs=[
            pl.BlockSpec((tm, d), lambda i, j: (i, 0)),
            pl.BlockSpec((1, d), lambda i, j: (0, 0)),
            pl.BlockSpec((None, d, tn), lambda i, j: (layer, 0, jnp.minimum(j, na - 1))),
            pl.BlockSpec((d, tn), lambda i, j: (0, jnp.maximum(j - na, 0))),
            pl.BlockSpec((None, d, LANES), lambda i, j: (layer, 0, fl_col // LANES)),
            pl.BlockSpec((1, LANES), lambda i, j: (0, 0)),
            pl.BlockSpec((tm, LANES), rope_map),
            pl.BlockSpec((tm, LANES), rope_map),
            pl.BlockSpec((tm, LANES), rope_map),
        ],
        out_specs=[
            pl.BlockSpec((tm, tn), lambda i, j: (i, j)),
            pl.BlockSpec((tm, LANES), lambda i, j: (i, 0)),
        ],
        out_shape=[jax.ShapeDtypeStruct((n, n_main), F32), jax.ShapeDtypeStruct((n, LANES), F32)],
        scratch_shapes=[pltpu.VMEM((tm, d), BF16)],
        compiler_params=_params(("parallel", "arbitrary")),
        name="proj",
    )(x, g, w_in, wb, w_in, bfl, cos_t, sa_t, sb_t)


def _cumsum_kernel(x_ref, tri_ref, o_ref):
    o_ref[...] = _sel_dot(tri_ref[...], x_ref[...])


def _cumsum_call(logf_p, tri, batch, seq):
    return pl.pallas_call(
        _cumsum_kernel,
        grid=(batch,),
        in_specs=[pl.BlockSpec((seq, LANES), lambda b: (b, 0)),
                  pl.BlockSpec((seq, seq), lambda b: (0, 0))],
        out_specs=pl.BlockSpec((seq, LANES), lambda b: (b, 0)),
        out_shape=jax.ShapeDtypeStruct((batch * seq, LANES), F32),
        compiler_params=_params(("parallel",)),
        name="cumsum_prompt",
    )(logf_p, tri)


def _lane_chunks(s):
    return [s[:, c * LANES:(c + 1) * LANES] for c in range(s.shape[1] // LANES)]


def _rowmax(s):
    return jnp.max(functools.reduce(jnp.maximum, _lane_chunks(s)), axis=-1, keepdims=True)


def _rowsum(s):
    return jnp.sum(functools.reduce(jnp.add, _lane_chunks(s)), axis=-1, keepdims=True)


def _softmax_step(s, v, m_ref, l_ref, acc_ref, idx):
    m_old = m_ref[idx]
    m_new = jnp.maximum(m_old, _rowmax(s))
    alpha = jnp.exp2(m_old - m_new)
    p_chunks = [jnp.exp2(c - m_new) for c in _lane_chunks(s)]
    l_ref[idx] = alpha * l_ref[idx] + functools.reduce(jnp.add, p_chunks)
    p = jnp.concatenate(p_chunks, axis=1).astype(BF16)
    pv = jnp.dot(p, v, preferred_element_type=F32)
    acc_ref[idx] = jnp.concatenate([alpha] * (pv.shape[1] // LANES), axis=1) * acc_ref[idx] + pv
    m_ref[idx] = m_new


def _softmax_denominator(l_ref, idx):
    return jnp.sum(l_ref[idx], axis=-1, keepdims=True)


def _softmax_step_stacked(s, v_heads, m_ref, l_ref, acc_ref, rows_per_head):
    m_old = m_ref[...]
    m_new = jnp.maximum(m_old, _rowmax(s))
    p = jnp.exp(s - m_new)
    alpha = jnp.exp(m_old - m_new)
    l_ref[...] = alpha * l_ref[...] + _rowsum(p)
    pv = [jnp.dot(p[h * rows_per_head:(h + 1) * rows_per_head, :].astype(BF16), v,
                  preferred_element_type=F32) for h, v in enumerate(v_heads)]
    acc_ref[...] = alpha * acc_ref[...] + jnp.concatenate(pv, axis=0)
    m_ref[...] = m_new


def _diff_lambda(lq1, lk1, lq2, lk2, lam_init):
    return (jnp.exp(jnp.sum(lq1[...] * lk1[...], axis=-1, keepdims=True))
            - jnp.exp(jnp.sum(lq2[...] * lk2[...], axis=-1, keepdims=True)) + lam_init)


def _diff_finish(acc_ref, l_ref, base, lam, sg, lam_init):
    o = (acc_ref[base] / _softmax_denominator(l_ref, base)
         - lam * (acc_ref[base + 1] / _softmax_denominator(l_ref, base + 1)))
    return _rms(o, sg, SUBLN_EPS) * (1.0 - lam_init)


def _diff_prompt_kernel(q_ref, k_ref, v_ref, lq1, lk1, lq2, lk2, sg_ref, o_ref,
                        kb_ref, vb_ref, m_ref, l_ref, acc_ref, *, tq, dk, lam_init, scale):
    qi = pl.program_id(2)

    @pl.when(qi == 0)
    def _():
        k = k_ref[...]
        for m in range(2):
            kb_ref[m] = k[:, m * dk:(m + 1) * dk].T.astype(BF16)
        vb_ref[...] = v_ref[...].astype(BF16)

    m_ref[...] = jnp.full(m_ref.shape, NEG_INF, F32)
    l_ref[...] = jnp.zeros(l_ref.shape, F32)
    acc_ref[...] = jnp.zeros(acc_ref.shape, F32)
    q = (q_ref[...] * (scale * LOG2E)).astype(BF16)

    def block(kk, causal):
        k0 = pl.multiple_of(kk * tq, tq)
        vblk = vb_ref[pl.ds(k0, tq), :]
        for m in range(2):
            s = jnp.dot(q[:, m * dk:(m + 1) * dk], kb_ref[m, :, pl.ds(k0, tq)],
                        preferred_element_type=F32)
            if causal:
                row = lax.broadcasted_iota(jnp.int32, s.shape, 0)
                col = lax.broadcasted_iota(jnp.int32, s.shape, 1)
                s = jnp.where(col <= row, s, NEG_INF)
            _softmax_step(s, vblk, m_ref, l_ref, acc_ref, m)

    def body(kk, carry):
        block(kk, False)
        return carry

    lax.fori_loop(0, qi, body, 0)
    block(qi, True)

    lam = _diff_lambda(lq1, lk1, lq2, lk2, lam_init)
    o_ref[...] = _diff_finish(acc_ref, l_ref, 0, lam, sg_ref[...], lam_init).astype(o_ref.dtype)


def _diff_prompt_call(proj, lams, sg, *, batch, seq, heads, dk, dv, tq, k_col, v_col, lam_init):
    nq = seq // tq
    n_rows = batch * seq
    lam_spec = pl.BlockSpec((1, dk), lambda b, h, q: (0, 0))
    kern = functools.partial(_diff_prompt_kernel, tq=tq, dk=dk, lam_init=lam_init, scale=dk ** -0.5)
    return pl.pallas_call(
        kern,
        grid=(batch, heads, nq),
        in_specs=[
            pl.BlockSpec((tq, 2 * dk), lambda b, h, q: (b * nq + q, h)),
            pl.BlockSpec((seq, 2 * dk), lambda b, h, q: (b, k_col // (2 * dk) + h)),
            pl.BlockSpec((seq, dv), lambda b, h, q: (b, v_col // dv + h)),
            lam_spec, lam_spec, lam_spec, lam_spec,
            pl.BlockSpec((1, dv), lambda b, h, q: (0, 0)),
        ],
        out_specs=pl.BlockSpec((tq, dv), lambda b, h, q: (b * nq + q, h)),
        out_shape=jax.ShapeDtypeStruct((n_rows, heads * dv), BF16),
        scratch_shapes=[
            pltpu.VMEM((2, dk, seq), BF16), pltpu.VMEM((seq, dv), BF16),
            pltpu.VMEM((2, tq, LANES), F32), pltpu.VMEM((2, tq, LANES), F32), pltpu.VMEM((2, tq, dv), F32),
        ],
        compiler_params=_params(("parallel", "parallel", "arbitrary")),
        name="diff_prompt",
    )(proj, proj, proj, *lams, sg)


def _fox_prompt_kernel(q_ref, k_ref, v_ref, cc_ref, cr_ref, o_ref,
                       kb_ref, vb_ref, m_ref, l_ref, acc_ref, *, tq, scale):
    h = pl.program_id(1)
    qi = pl.program_id(2)

    @pl.when(qi == 0)
    def _():
        kb_ref[...] = k_ref[...].T.astype(BF16)
        vb_ref[...] = v_ref[...].astype(BF16)

    m_ref[...] = jnp.full(m_ref.shape, NEG_INF, F32)
    l_ref[...] = jnp.zeros(l_ref.shape, F32)
    acc_ref[...] = jnp.zeros(acc_ref.shape, F32)
    q = (q_ref[...] * (scale * LOG2E)).astype(BF16)
    cc = cc_ref[...]
    lane = lax.broadcasted_iota(jnp.int32, cc.shape, 1)
    cq = jnp.sum(jnp.where(lane == h, cc, 0.0), axis=-1, keepdims=True) * LOG2E

    def block(kk, causal):
        k0 = pl.multiple_of(kk * tq, tq)
        ck = cr_ref[0, pl.ds(kk, 1), :] * LOG2E
        s = jnp.dot(q, kb_ref[:, pl.ds(k0, tq)], preferred_element_type=F32) + (cq - ck)
        if causal:
            row = lax.broadcasted_iota(jnp.int32, s.shape, 0)
            col = lax.broadcasted_iota(jnp.int32, s.shape, 1)
            s = jnp.where(col <= row, s, NEG_INF)
        _softmax_step(s, vb_ref[pl.ds(k0, tq), :], m_ref, l_ref, acc_ref, 0)

    def body(kk, carry):
        block(kk, False)
        return carry

    lax.fori_loop(0, qi, body, 0)
    block(qi, True)
    o_ref[...] = (acc_ref[0] / _softmax_denominator(l_ref, 0)).astype(o_ref.dtype)


def _fox_prompt_call(proj, c_col, c_row, *, batch, seq, heads, dh, tq, q_col, k_col, v_col):
    nq = seq // tq
    kern = functools.partial(_fox_prompt_kernel, tq=tq, scale=dh ** -0.5)
    return pl.pallas_call(
        kern,
        grid=(batch, heads, nq),
        in_specs=[
            pl.BlockSpec((tq, dh), lambda b, h, q: (b * nq + q, q_col // dh + h)),
            pl.BlockSpec((seq, dh), lambda b, h, q: (b, k_col // dh + h)),
            pl.BlockSpec((seq, dh), lambda b, h, q: (b, v_col // dh + h)),
            pl.BlockSpec((tq, LANES), lambda b, h, q: (b * nq + q, 0)),
            pl.BlockSpec((1, nq, tq), lambda b, h, q: (b * heads + h, 0, 0)),
        ],
        out_specs=pl.BlockSpec((tq, dh), lambda b, h, q: (b * nq + q, h)),
        out_shape=jax.ShapeDtypeStruct((batch * seq, heads * dh), BF16),
        scratch_shapes=[
            pltpu.VMEM((dh, seq), BF16), pltpu.VMEM((seq, dh), BF16),
            pltpu.VMEM((1, tq, LANES), F32), pltpu.VMEM((1, tq, LANES), F32), pltpu.VMEM((1, tq, dh), F32),
        ],
        compiler_params=_params(("parallel", "parallel", "arbitrary")),
        name="fox_prompt",
    )(proj, proj, proj, c_col, c_row)


def _pad_rows(x, rows):
    return jnp.concatenate([x, jnp.zeros((rows - x.shape[0], x.shape[1]), x.dtype)], axis=0)


def _causal_new(s, t_new):
    row = lax.broadcasted_iota(jnp.int32, s.shape, 0) % t_new
    col = lax.broadcasted_iota(jnp.int32, s.shape, 1)
    return jnp.where(col <= row, s, NEG_INF)


def _diff_sample_kernel(pt_ref, q_ref, kn_ref, vn_ref, *refs, heads, dk, dv, page, n_group, lam_init, scale):
    del pt_ref
    kc_refs, vc_refs = refs[:n_group], refs[n_group:2 * n_group]
    lq1, lk1, lq2, lk2, sg_ref, o_ref, m_ref, l_ref, acc_ref = refs[2 * n_group:]
    p = pl.program_id(1)
    last = pl.num_programs(1) - 1
    t_new = q_ref.shape[0]
    n_maps = 2 * heads
    nc = dv // LANES

    @pl.when(p == 0)
    def _():
        m_ref[...] = jnp.full(m_ref.shape, NEG_INF, F32)
        l_ref[...] = jnp.zeros(l_ref.shape, F32)
        acc_ref[...] = jnp.zeros(acc_ref.shape, F32)

    q = q_ref[...]
    qb = [q[:, hm * dk:(hm + 1) * dk].astype(BF16) for hm in range(n_maps)]

    s = jnp.concatenate([
        jnp.concatenate([_dot_nt(qb[hm], kc[pl.ds(hm, page, stride=n_maps), :].astype(BF16)) for kc in kc_refs],
                        axis=1) for hm in range(n_maps)], axis=0) * scale
    v_heads = [jnp.concatenate([
        jnp.concatenate([vc[pl.ds(c * heads + h, page, stride=heads * nc), :] for c in range(nc)], axis=1)
        for vc in vc_refs], axis=0).astype(BF16) for h in range(heads)]
    _softmax_step_stacked(s, v_heads, m_ref, l_ref, acc_ref, 2 * t_new)

    @pl.when(p == last)
    def _():
        kn = kn_ref[...]
        vn = vn_ref[...]
        s_new = jnp.concatenate([
            _dot_nt(qb[hm], _pad_rows(kn[:, hm * dk:(hm + 1) * dk], page).astype(BF16))
            for hm in range(n_maps)], axis=0) * scale
        v_new = [_pad_rows(vn[:, h * dv:(h + 1) * dv], page).astype(BF16) for h in range(heads)]
        _softmax_step_stacked(_causal_new(s_new, t_new), v_new, m_ref, l_ref, acc_ref, 2 * t_new)
        lam = _diff_lambda(lq1, lk1, lq2, lk2, lam_init)
        o = acc_ref[...] / l_ref[...]
        for h in range(heads):
            r0 = 2 * h * t_new
            oh = o[r0:r0 + t_new, :] - lam * o[r0 + t_new:r0 + 2 * t_new, :]
            o_ref[:, h * dv:(h + 1) * dv] = _rms(oh, sg_ref[...], SUBLN_EPS) * (1.0 - lam_init)


def _diff_sample_call(page_table, proj, kc, vc, lams, sg, layer, *, n_prompt, t_new, heads, dk, dv,
                      page, k_col, v_col, lam_init, n_group):
    bs, n_pages = page_table.shape
    w_q = heads * 2 * dk
    row0 = n_prompt // t_new
    lam_spec = pl.BlockSpec((1, dk), lambda b, p, pt: (0, 0))
    kern = functools.partial(_diff_sample_kernel, heads=heads, dk=dk, dv=dv, page=page, n_group=n_group,
                             lam_init=lam_init, scale=dk ** -0.5)

    def page_spec(rows, g):
        return pl.BlockSpec((None, None, rows, LANES), lambda b, p, pt: (layer, pt[b, p * n_group + g], 0, 0))

    grid_spec = pltpu.PrefetchScalarGridSpec(
        num_scalar_prefetch=1,
        grid=(bs, n_pages // n_group),
        in_specs=[
            pl.BlockSpec((t_new, w_q), lambda b, p, pt: (row0 + b, 0)),
            pl.BlockSpec((t_new, w_q), lambda b, p, pt: (row0 + b, k_col // w_q)),
            pl.BlockSpec((t_new, heads * dv), lambda b, p, pt: (row0 + b, v_col // (heads * dv))),
            *[page_spec(page * heads * 2, g) for g in range(n_group)],
            *[page_spec(page * heads * dv // LANES, g) for g in range(n_group)],
            lam_spec, lam_spec, lam_spec, lam_spec,
            pl.BlockSpec((1, dv), lambda b, p, pt: (0, 0)),
        ],
        out_specs=pl.BlockSpec((t_new, heads * dv), lambda b, p, pt: (b, 0)),
        scratch_shapes=[
            pltpu.VMEM((2 * heads * t_new, 1), F32), pltpu.VMEM((2 * heads * t_new, 1), F32),
            pltpu.VMEM((2 * heads * t_new, dv), F32),
        ],
    )
    return pl.pallas_call(
        kern,
        grid_spec=grid_spec,
        out_shape=jax.ShapeDtypeStruct((bs * t_new, heads * dv), F32),
        compiler_params=_params(("parallel", "arbitrary")),
        name="diff_sample",
    )(page_table, proj, proj, proj, *([kc] * n_group), *([vc] * n_group), *lams, sg)


def _logf_pages_kernel(pt_ref, lf_ref, mcs_ref, mbc_ref, msel_ref, lst_ref, cp_ref, ct_ref, g_ref,
                       *, n_pages, heads, page):
    b = pl.program_id(0)
    for p in range(n_pages):
        g_ref[pl.ds(p, 1), :] = lf_ref[pl.ds(pt_ref[b, p], 1), :]
    x = g_ref[...]
    y = _dot_sel(x, mcs_ref[...])
    tot = _dot_sel(y, mbc_ref[...])
    c = y + _sel_dot(lst_ref[...], tot)
    for h in range(heads):
        cp_ref[0, h * n_pages:(h + 1) * n_pages, :] = c[:, h * page:(h + 1) * page]
    ct_ref[0] = _dot_sel(c, msel_ref[...])[n_pages - 1:n_pages, :]


def _logf_pages_call(page_table, lf2, mats, *, heads, page):
    bs, n_pages = page_table.shape
    n_pool, w = lf2.shape
    mcs, mbc, msel, lst = mats
    full = lambda a: pl.BlockSpec(a.shape, lambda b, pt: (0,) * a.ndim)
    kern = functools.partial(_logf_pages_kernel, n_pages=n_pages, heads=heads, page=page)
    grid_spec = pltpu.PrefetchScalarGridSpec(
        num_scalar_prefetch=1,
        grid=(bs,),
        in_specs=[full(lf2), full(mcs), full(mbc), full(msel), full(lst)],
        out_specs=[pl.BlockSpec((1, heads * n_pages, page), lambda b, pt: (b, 0, 0)),
                   pl.BlockSpec((1, 1, LANES), lambda b, pt: (b, 0, 0))],
        scratch_shapes=[pltpu.VMEM((n_pages, w), F32)],
    )
    return pl.pallas_call(
        kern,
        grid_spec=grid_spec,
        out_shape=[jax.ShapeDtypeStruct((bs, heads * n_pages, page), F32),
                   jax.ShapeDtypeStruct((bs, 1, LANES), F32)],
        compiler_params=_params(("arbitrary",)),
        name="logf_pages",
    )(page_table, lf2, mcs, mbc, msel, lst)


def _fox_sample_kernel(pt_ref, q_ref, kn_ref, vn_ref, lfn_ref, *refs, heads, dh, page, n_pages, n_group, scale):
    del pt_ref
    kc_refs, vc_refs = refs[:n_group], refs[n_group:2 * n_group]
    cp_ref, ct_ref, o_ref, m_ref, l_ref, acc_ref, cn_ref = refs[2 * n_group:]
    p = pl.program_id(1)
    last = pl.num_programs(1) - 1
    t_new = q_ref.shape[0]

    @pl.when(p == 0)
    def _():
        m_ref[...] = jnp.full(m_ref.shape, NEG_INF, F32)
        l_ref[...] = jnp.zeros(l_ref.shape, F32)
        acc_ref[...] = jnp.zeros(acc_ref.shape, F32)
        x = lfn_ref[...]
        row = lax.broadcasted_iota(jnp.int32, x.shape, 0)
        k = 1
        while k < x.shape[0]:
            x = x + jnp.where(row >= k, pltpu.roll(x, k, 0), 0.0)
            k *= 2
        cn_ref[...] = ct_ref[0] + x

    q = q_ref[...]
    cn = cn_ref[...]
    qb = [q[:, h * dh:(h + 1) * dh].astype(BF16) for h in range(heads)]
    cq = [cn[:, h:h + 1] for h in range(heads)]

    s = jnp.concatenate([
        jnp.concatenate([_dot_nt(qb[h], kc[pl.ds(h, page, stride=heads), :].astype(BF16)) for kc in kc_refs],
                        axis=1) for h in range(heads)], axis=0) * scale
    bias = jnp.concatenate([
        cq[h] - jnp.concatenate([cp_ref[0, pl.ds(h * n_pages + p * n_group + g, 1), :] for g in range(n_group)],
                                axis=1) for h in range(heads)], axis=0)
    v_heads = [jnp.concatenate([vc[pl.ds(h, page, stride=heads), :] for vc in vc_refs], axis=0).astype(BF16)
               for h in range(heads)]
    _softmax_step_stacked(s + bias, v_heads, m_ref, l_ref, acc_ref, t_new)

    @pl.when(p == last)
    def _():
        kn = kn_ref[...]
        vn = vn_ref[...]
        s_new = jnp.concatenate([
            _dot_nt(qb[h], _pad_rows(kn[:, h * dh:(h + 1) * dh], page).astype(BF16)) for h in range(heads)],
            axis=0) * scale
        row = lax.broadcasted_iota(jnp.int32, (t_new, page), 0)
        col = lax.broadcasted_iota(jnp.int32, (t_new, page), 1)
        bias_new = jnp.concatenate([
            cq[h] - jnp.sum(jnp.where(row == col, cq[h], 0.0), axis=0, keepdims=True) for h in range(heads)],
            axis=0)
        v_new = [_pad_rows(vn[:, h * dh:(h + 1) * dh], page).astype(BF16) for h in range(heads)]
        _softmax_step_stacked(_causal_new(s_new + bias_new, t_new), v_new, m_ref, l_ref, acc_ref, t_new)
        o = acc_ref[...] / l_ref[...]
        for h in range(heads):
            o_ref[:, h * dh:(h + 1) * dh] = o[h * t_new:(h + 1) * t_new, :]


def _fox_sample_call(page_table, proj, logf, kc, vc, cpast, ctot, layer, *, n_prompt, t_new, heads, dh,
                     page, q_col, k_col, v_col, n_group):
    bs, n_pages = page_table.shape
    w = heads * dh
    row0 = n_prompt // t_new
    kern = functools.partial(_fox_sample_kernel, heads=heads, dh=dh, page=page, n_pages=n_pages,
                             n_group=n_group, scale=dh ** -0.5)

    def page_spec(g):
        return pl.BlockSpec((None, None, page * heads, dh),
                            lambda b, p, pt: (layer, pt[b, p * n_group + g], 0, 0))

    grid_spec = pltpu.PrefetchScalarGridSpec(
        num_scalar_prefetch=1,
        grid=(bs, n_pages // n_group),
        in_specs=[
            pl.BlockSpec((t_new, w), lambda b, p, pt: (row0 + b, q_col // w)),
            pl.BlockSpec((t_new, w), lambda b, p, pt: (row0 + b, k_col // w)),
            pl.BlockSpec((t_new, w), lambda b, p, pt: (row0 + b, v_col // w)),
            pl.BlockSpec((t_new, LANES), lambda b, p, pt: (row0 + b, 0)),
            *[page_spec(g) for g in range(n_group)],
            *[page_spec(g) for g in range(n_group)],
            pl.BlockSpec((1, heads * n_pages, page), lambda b, p, pt: (b, 0, 0)),
            pl.BlockSpec((1, 1, LANES), lambda b, p, pt: (b, 0, 0)),
        ],
        out_specs=pl.BlockSpec((t_new, w), lambda b, p, pt: (b, 0)),
        scratch_shapes=[
            pltpu.VMEM((heads * t_new, 1), F32), pltpu.VMEM((heads * t_new, 1), F32),
            pltpu.VMEM((heads * t_new, dh), F32), pltpu.VMEM((t_new, LANES), F32),
        ],
    )
    return pl.pallas_call(
        kern,
        grid_spec=grid_spec,
        out_shape=jax.ShapeDtypeStruct((bs * t_new, w), F32),
        compiler_params=_params(("parallel", "arbitrary")),
        name="fox_sample",
    )(page_table, proj, proj, proj, logf, *([kc] * n_group), *([vc] * n_group), cpast, ctot)


def _lru_coeffs(x, xprev_ext, gate_w, cw_ref, cb_ref, wga_ref, bga_ref, wgx_ref, bgx_ref, lam_ref, conv_w):
    rows = x.shape[0]
    nb = wga_ref.shape[0]
    bw = wga_ref.shape[1]
    u = xprev_ext(conv_w - 1) * cw_ref[0:1, :]
    for k in range(1, conv_w - 1):
        u = u + xprev_ext(conv_w - 1 - k) * cw_ref[k:k + 1, :]
    u = u + x * cw_ref[conv_w - 1:conv_w, :]
    u = u + cb_ref[...]
    ub = u.astype(BF16)
    r_parts, i_parts = [], []
    for n in range(nb):
        un = ub[:, n * bw:(n + 1) * bw]
        r_parts.append(jnp.dot(un, wga_ref[n], preferred_element_type=F32))
        i_parts.append(jnp.dot(un, wgx_ref[n], preferred_element_type=F32))
    r = _sigmoid(jnp.concatenate(r_parts, axis=-1) + bga_ref[...])
    i = _sigmoid(jnp.concatenate(i_parts, axis=-1) + bgx_ref[...])
    log_a = -LRU_C * r * _softplus(-lam_ref[...])
    a = jnp.exp(log_a)
    b = jnp.sqrt(-jnp.tanh(log_a) * (a * a + 1.0)) * (i * u)
    del rows, gate_w
    return a, b


def _group_scan(a, b):
    row = lax.broadcasted_iota(jnp.int32, a.shape, 0) % SUBLANES
    k = 1
    while k < SUBLANES:
        keep = row >= k
        a_sh = jnp.where(keep, pltpu.roll(a, k, 0), 1.0)
        b_sh = jnp.where(keep, pltpu.roll(b, k, 0), 0.0)
        b = a * b_sh + b
        a = a * a_sh
        k *= 2
    return a, b


def _gelu_tanh(x):
    return 0.5 * x * (1.0 + jnp.tanh(math.sqrt(2.0 / math.pi) * (x + 0.044715 * (x * x * x))))


def _lru_prompt_kernel(x_ref, g_ref, cw_ref, cb_ref, wga_ref, bga_ref, wgx_ref, bgx_ref, lam_ref,
                       y_ref, ht_ref, tail_ref, hl_ref, a_ref, b_ref, hs_ref, *, conv_w):
    ti = pl.program_id(1)

    @pl.when(ti == 0)
    def _():
        tail_ref[...] = jnp.zeros(tail_ref.shape, F32)
        hl_ref[...] = jnp.zeros(hl_ref.shape, F32)

    x = x_ref[...]
    tt = x.shape[0]
    xe = jnp.concatenate([tail_ref[...], x], axis=0)
    delayed = lambda k: pltpu.roll(xe, k, 0)[SUBLANES:, :]
    a, b = _lru_coeffs(x, delayed, None, cw_ref, cb_ref, wga_ref, bga_ref, wgx_ref, bgx_ref, lam_ref, conv_w)
    a, b = _group_scan(a, b)
    a_ref[...] = a
    b_ref[...] = b
    tail_ref[...] = x[tt - SUBLANES:, :]

    def body(g, hlast):
        r0 = pl.multiple_of(g * SUBLANES, SUBLANES)
        hg = b_ref[pl.ds(r0, SUBLANES), :] + a_ref[pl.ds(r0, SUBLANES), :] * hlast
        hs_ref[pl.ds(r0, SUBLANES), :] = hg
        return hg[SUBLANES - 1:SUBLANES, :]

    hl = lax.fori_loop(0, tt // SUBLANES, body, hl_ref[...])
    hl_ref[...] = hl
    ht_ref[0] = hl
    y_ref[...] = (_gelu_tanh(g_ref[...]) * hs_ref[...]).astype(y_ref.dtype)


def _lru_prompt_call(proj, wts, *, batch, seq, wc, tt, g_col, x_col):
    cw, cb, wga, bga, wgx, bgx, lam = wts
    nt = seq // tt
    conv_w = cw.shape[0]
    full = lambda a: pl.BlockSpec(a.shape, lambda b, t: (0,) * a.ndim)
    kern = functools.partial(_lru_prompt_kernel, conv_w=conv_w)
    return pl.pallas_call(
        kern,
        grid=(batch, nt),
        in_specs=[
            pl.BlockSpec((tt, wc), lambda b, t: (b * nt + t, x_col // wc)),
            pl.BlockSpec((tt, wc), lambda b, t: (b * nt + t, g_col // wc)),
            full(cw), full(cb), full(wga), full(bga), full(wgx), full(bgx), full(lam),
        ],
        out_specs=[pl.BlockSpec((tt, wc), lambda b, t: (b * nt + t, 0)),
                   pl.BlockSpec((1, 1, wc), lambda b, t: (b, 0, 0))],
        out_shape=[jax.ShapeDtypeStruct((batch * seq, wc), BF16),
                   jax.ShapeDtypeStruct((batch, 1, wc), F32)],
        scratch_shapes=[pltpu.VMEM((SUBLANES, wc), F32), pltpu.VMEM((1, wc), F32),
                        pltpu.VMEM((tt, wc), F32), pltpu.VMEM((tt, wc), F32), pltpu.VMEM((tt, wc), F32)],
        compiler_params=_params(("parallel", "arbitrary")),
        name="lru_prompt",
    )(proj, proj, cw, cb, wga, bga, wgx, bgx, lam)


def _lru_sample_kernel(x_ref, g_ref, prev_ref, h0_ref, cw_ref, cb_ref, wga_ref, bga_ref, wgx_ref, bgx_ref,
                       lam_ref, y_ref, hs_ref, *, conv_w):
    x = x_ref[...]
    rows = x.shape[0]
    row = lax.broadcasted_iota(jnp.int32, x.shape, 0) % SUBLANES
    prev = prev_ref[...]

    def delayed(k):
        return jnp.where(row >= k, pltpu.roll(x, k, 0), pltpu.roll(prev, (k - SUBLANES) % rows, 0))

    a, b = _lru_coeffs(x, delayed, None, cw_ref, cb_ref, wga_ref, bga_ref, wgx_ref, bgx_ref, lam_ref, conv_w)
    a, b = _group_scan(a, b)
    hs = b + a * h0_ref[...]
    hs_ref[...] = hs
    y_ref[...] = _gelu_tanh(g_ref[...]) * hs


def _lru_sample_call(proj, prev8, h0rep, wts, *, n_prompt, n_sample, wc, g_col, x_col):
    cw, cb, wga, bga, wgx, bgx, lam = wts
    conv_w = cw.shape[0]
    rb = n_prompt // n_sample
    full = lambda a: pl.BlockSpec(a.shape, lambda i: (0,) * a.ndim)
    kern = functools.partial(_lru_sample_kernel, conv_w=conv_w)
    return pl.pallas_call(
        kern,
        grid=(1,),
        in_specs=[
            pl.BlockSpec((n_sample, wc), lambda i: (rb, x_col // wc)),
            pl.BlockSpec((n_sample, wc), lambda i: (rb, g_col // wc)),
            full(prev8), full(h0rep),
            full(cw), full(cb), full(wga), full(bga), full(wgx), full(bgx), full(lam),
        ],
        out_specs=[pl.BlockSpec((n_sample, wc), lambda i: (0, 0)),
                   pl.BlockSpec((n_sample, wc), lambda i: (0, 0))],
        out_shape=[jax.ShapeDtypeStruct((n_sample, wc), F32), jax.ShapeDtypeStruct((n_sample, wc), F32)],
        compiler_params=_params(("arbitrary",)),
        name="lru_sample",
    )(proj, proj, prev8, h0rep, cw, cb, wga, bga, wgx, bgx, lam)


def _mix_kernel(ap_ref, bp_ref, cp_ref, as_ref, bs_ref, cs_ref, w_ref, x_ref, g_ref, *rest,
                n_experts, n_prompt_tiles):
    if n_experts:
        wrh_ref, wrl_ref, x1_ref, rg_ref, re_ref = rest
    else:
        x1_ref, hn_ref = rest
    i = pl.program_id(0)
    wa, wb = ap_ref.shape[1], bp_ref.shape[1]

    def project(a_ref, b_ref, c_ref):
        d = lambda v, lo, hi: jnp.dot(v[...].astype(BF16), w_ref[lo:hi, :], preferred_element_type=F32)
        return d(a_ref, 0, wa) + d(b_ref, wa, wa + wb) + d(c_ref, wa + wb, w_ref.shape[0])

    @pl.when(i < n_prompt_tiles)
    def _():
        x1_ref[...] = x_ref[...] + project(ap_ref, bp_ref, cp_ref)

    @pl.when(i >= n_prompt_tiles)
    def _():
        x1_ref[...] = x_ref[...] + project(as_ref, bs_ref, cs_ref)

    hn = _rms(x1_ref[...], g_ref[...], RMS_EPS)
    if not n_experts:
        hn_ref[...] = hn.astype(BF16)
    else:
        h_hi = hn.astype(BF16)
        h_lo = (hn - h_hi.astype(F32)).astype(BF16)
        d = lambda a, b: jnp.dot(a, b[...], preferred_element_type=F32)
        logits = d(h_hi, wrh_ref) + (d(h_lo, wrh_ref) + d(h_hi, wrl_ref))
        lane = lax.broadcasted_iota(jnp.int32, logits.shape, 1)
        lane_f = lane.astype(F32)
        big = float(LANES)
        lg = jnp.where(lane < n_experts, logits, NEG_INF)
        v1 = jnp.max(lg, axis=-1, keepdims=True)
        i1 = jnp.min(jnp.where(lg == v1, lane_f, big), axis=-1, keepdims=True)
        lg2 = jnp.where(lane_f == i1, NEG_INF, lg)
        v2 = jnp.max(lg2, axis=-1, keepdims=True)
        i2 = jnp.min(jnp.where(lg2 == v2, lane_f, big), axis=-1, keepdims=True)
        e = jnp.exp(v2 - v1)
        g1 = 1.0 / (1.0 + e)
        g2 = e / (1.0 + e)
        rg_ref[...] = jnp.where(lane == 0, g1, jnp.where(lane == 1, g2, 0.0))
        re_ref[...] = jnp.where(lane == 0, i1, jnp.where(lane == 1, i2, 0.0)).astype(jnp.int32)


def _mix_call(branches_p, branches_s, w_out, x, g, w_router, *, tm):
    n, d = x.shape
    npt = branches_p[0].shape[0] // tm
    n_experts = 0 if w_router is None else w_router.shape[1]
    kern = functools.partial(_mix_kernel, n_experts=n_experts, n_prompt_tiles=npt)
    row = lambda c: pl.BlockSpec((tm, c), lambda i: (i, 0))
    row_p = lambda a: pl.BlockSpec((tm, a.shape[1]), lambda i: (jnp.minimum(i, npt - 1), 0))
    row_s = lambda a: pl.BlockSpec((tm, a.shape[1]), lambda i: (jnp.maximum(i - npt, 0), 0))
    in_specs = ([row_p(a) for a in branches_p] + [row_s(a) for a in branches_s]
                + [pl.BlockSpec((d, d), lambda i: (0, 0)), row(d), pl.BlockSpec((1, d), lambda i: (0, 0))])
    args = [*branches_p, *branches_s, w_out, x, g]
    if n_experts:
        wr = jnp.zeros((d, LANES), F32).at[:, :n_experts].set(w_router)
        wr_hi = wr.astype(BF16)
        wr_lo = (wr - wr_hi.astype(F32)).astype(BF16)
        in_specs += [pl.BlockSpec((d, LANES), lambda i: (0, 0))] * 2
        out_specs = [row(d), row(LANES), row(LANES)]
        out_shape = [jax.ShapeDtypeStruct((n, d), F32), jax.ShapeDtypeStruct((n, LANES), F32),
                     jax.ShapeDtypeStruct((n, LANES), jnp.int32)]
        args += [wr_hi, wr_lo]
    else:
        out_specs = [row(d), row(d)]
        out_shape = [jax.ShapeDtypeStruct((n, d), F32), jax.ShapeDtypeStruct((n, d), BF16)]
    return pl.pallas_call(
        kern,
        grid=(n // tm,),
        in_specs=in_specs,
        out_specs=out_specs,
        out_shape=out_shape,
        compiler_params=_params(("parallel",)),
        name="mix_router" if n_experts else "mix",
    )(*args)


def _swiglu_acc(h, wg, wu, wd):
    g = jnp.dot(h, wg, preferred_element_type=F32)
    u = jnp.dot(h, wu, preferred_element_type=F32)
    a = (g * _sigmoid(g) * u).astype(BF16)
    return jnp.dot(a, wd, preferred_element_type=F32)


def _ffn_kernel(h_ref, wg_ref, wu_ref, wd_ref, x_ref, o_ref):
    j = pl.program_id(1)

    @pl.when(j == 0)
    def _():
        o_ref[...] = x_ref[...]

    o_ref[...] += _swiglu_acc(h_ref[...], wg_ref[...], wu_ref[...], wd_ref[...])


def _ffn_call(hn, wg, wu, wd, x1, *, tm, tf):
    n, d = x1.shape
    f = wg.shape[1]
    return pl.pallas_call(
        _ffn_kernel,
        grid=(n // tm, f // tf),
        in_specs=[
            pl.BlockSpec((tm, d), lambda i, j: (i, 0)),
            pl.BlockSpec((d, tf), lambda i, j: (0, j)),
            pl.BlockSpec((d, tf), lambda i, j: (0, j)),
            pl.BlockSpec((tf, d), lambda i, j: (j, 0)),
            pl.BlockSpec((tm, d), lambda i, j: (i, 0)),
        ],
        out_specs=pl.BlockSpec((tm, d), lambda i, j: (i, 0)),
        out_shape=jax.ShapeDtypeStruct((n, d), F32),
        compiler_params=_params(("parallel", "arbitrary")),
        name="ffn_dense",
    )(hn, wg, wu, wd, x1)


def _gather_norm_kernel(rt_ref, nv_ref, x_hbm, g_ref, o_ref, buf_ref, sem, *, tg):
    t = pl.program_id(0)

    def row_copy(r, tok):
        return pltpu.make_async_copy(x_hbm.at[pl.ds(tok, 1), :], buf_ref.at[pl.ds(r, 1), :], sem)

    @pl.when(t < nv_ref[0])
    def _():
        def issue(r, carry):
            row_copy(r, rt_ref[t * tg + r]).start()
            return carry

        lax.fori_loop(0, tg, issue, 0, unroll=8)
        pltpu.make_async_copy(buf_ref, buf_ref, sem).wait()
        o_ref[...] = _rms(buf_ref[...], g_ref[...], RMS_EPS).astype(BF16)

    @pl.when(t >= nv_ref[0])
    def _():
        o_ref[...] = jnp.zeros(o_ref.shape, o_ref.dtype)


def _gather_norm_call(row_token, n_valid, x1, g, *, tg):
    r = row_token.shape[0]
    d = x1.shape[1]
    grid_spec = pltpu.PrefetchScalarGridSpec(
        num_scalar_prefetch=2,
        grid=(r // tg,),
        in_specs=[pl.BlockSpec(memory_space=pl.ANY), pl.BlockSpec((1, d), lambda t, rt, nv: (0, 0))],
        out_specs=pl.BlockSpec((tg, d), lambda t, rt, nv: (t, 0)),
        scratch_shapes=[pltpu.VMEM((tg, d), F32), pltpu.SemaphoreType.DMA(())],
    )
    return pl.pallas_call(
        functools.partial(_gather_norm_kernel, tg=tg),
        grid_spec=grid_spec,
        out_shape=jax.ShapeDtypeStruct((r, d), BF16),
        compiler_params=_params(("arbitrary",)),
        name="moe_gather",
    )(row_token, n_valid, x1, g)


def _moe_ffn_kernel(te_ref, nv_ref, h_ref, wg_ref, wu_ref, wd_ref, o_ref):
    del te_ref
    t = pl.program_id(0)
    j = pl.program_id(1)

    @pl.when(j == 0)
    def _():
        o_ref[...] = jnp.zeros(o_ref.shape, F32)

    @pl.when(t < nv_ref[0])
    def _():
        o_ref[...] += _swiglu_acc(h_ref[...], wg_ref[...], wu_ref[...], wd_ref[...])


def _moe_ffn_call(tile_expert, n_valid, xs, wg, wu, wd, *, tm, tf):
    r, d = xs.shape
    f = wg.shape[2]
    nj = f // tf

    def tile(t, nv):
        return jnp.minimum(t, nv[0] - 1)

    def fblk(t, j, nv):
        return jnp.where(t < nv[0], j, nj - 1)

    grid_spec = pltpu.PrefetchScalarGridSpec(
        num_scalar_prefetch=2,
        grid=(r // tm, nj),
        in_specs=[
            pl.BlockSpec((tm, d), lambda t, j, te, nv: (tile(t, nv), 0)),
            pl.BlockSpec((None, d, tf), lambda t, j, te, nv: (te[t], 0, fblk(t, j, nv))),
            pl.BlockSpec((None, d, tf), lambda t, j, te, nv: (te[t], 0, fblk(t, j, nv))),
            pl.BlockSpec((None, tf, d), lambda t, j, te, nv: (te[t], fblk(t, j, nv), 0)),
        ],
        out_specs=pl.BlockSpec((tm, d), lambda t, j, te, nv: (t, 0)),
    )
    return pl.pallas_call(
        _moe_ffn_kernel,
        grid_spec=grid_spec,
        out_shape=jax.ShapeDtypeStruct((r, d), F32),
        compiler_params=_params(("arbitrary", "arbitrary")),
        name="moe_ffn",
    )(tile_expert, n_valid, xs, wg, wu, wd)


def _combine_norm_kernel(pos_ref, x_ref, rg_ref, ys_hbm, g_ref, *rest, tc, final_norm, n_prompt_tiles):
    if final_norm:
        op_ref, os_ref, buf_ref, sem = rest
    else:
        o_ref, buf_ref, sem = rest
    i = pl.program_id(0)

    def row_copy(k, r, src):
        return pltpu.make_async_copy(ys_hbm.at[pl.ds(src, 1), :], buf_ref.at[k, pl.ds(r, 1), :], sem)

    def issue(r, carry):
        for k in range(TOP_K):
            row_copy(k, r, pos_ref[(i * tc + r) * TOP_K + k]).start()
        return carry

    lax.fori_loop(0, tc, issue, 0, unroll=4)
    pltpu.make_async_copy(buf_ref, buf_ref, sem).wait()
    rg = rg_ref[...]
    moe = rg[:, 0:1] * buf_ref[0]
    for k in range(1, TOP_K):
        moe = moe + rg[:, k:k + 1] * buf_ref[k]
    x2 = x_ref[...] + moe
    if final_norm:
        y = _rms(x2, g_ref[...], RMS_EPS)

        @pl.when(i < n_prompt_tiles)
        def _():
            op_ref[...] = y

        @pl.when(i >= n_prompt_tiles)
        def _():
            os_ref[...] = y
    else:
        o_ref[...] = x2


def _combine_norm_call(pos, x1, rg, ys, g, *, tc, final_norm, n_prompt):
    n, d = x1.shape
    npt = n_prompt // tc
    if final_norm:
        out_specs = [pl.BlockSpec((tc, d), lambda i, ps: (jnp.minimum(i, npt - 1), 0)),
                     pl.BlockSpec((tc, d), lambda i, ps: (jnp.maximum(i - npt, 0), 0))]
        out_shape = [jax.ShapeDtypeStruct((n_prompt, d), F32), jax.ShapeDtypeStruct((n - n_prompt, d), F32)]
    else:
        out_specs = pl.BlockSpec((tc, d), lambda i, ps: (i, 0))
        out_shape = jax.ShapeDtypeStruct((n, d), F32)
    grid_spec = pltpu.PrefetchScalarGridSpec(
        num_scalar_prefetch=1,
        grid=(n // tc,),
        in_specs=[
            pl.BlockSpec((tc, d), lambda i, ps: (i, 0)),
            pl.BlockSpec((tc, LANES), lambda i, ps: (i, 0)),
            pl.BlockSpec(memory_space=pl.ANY),
            pl.BlockSpec((1, d), lambda i, ps: (0, 0)),
        ],
        out_specs=out_specs,
        scratch_shapes=[pltpu.VMEM((TOP_K, tc, d), F32), pltpu.SemaphoreType.DMA(())],
    )
    return pl.pallas_call(
        functools.partial(_combine_norm_kernel, tc=tc, final_norm=final_norm, n_prompt_tiles=npt),
        grid_spec=grid_spec,
        out_shape=out_shape,
        compiler_params=_params(("arbitrary",)),
        name="moe_combine",
    )(pos, x1, rg, ys, g)


def _norm_kernel(x_ref, g_ref, o_ref):
    o_ref[...] = _rms(x_ref[...], g_ref[...], RMS_EPS)


def _norm_call(x, g, *, tm):
    n, d = x.shape
    return pl.pallas_call(
        _norm_kernel,
        grid=(n // tm,),
        in_specs=[pl.BlockSpec((tm, d), lambda i: (i, 0)), pl.BlockSpec((1, d), lambda i: (0, 0))],
        out_specs=pl.BlockSpec((tm, d), lambda i: (i, 0)),
        out_shape=jax.ShapeDtypeStruct((n, d), F32),
        compiler_params=_params(("parallel",)),
        name="final_norm",
    )(x, g)


def _route_plan(re, n_experts, tm):
    n = re.shape[0]
    e_flat = re.reshape(-1)
    oh = (e_flat[:, None] == jnp.arange(n_experts, dtype=jnp.int32)[None, :]).astype(jnp.int32)
    csum = jnp.cumsum(oh, axis=0)
    rank = jnp.sum((csum - oh) * oh, axis=1)
    counts = csum[-1]
    padded = ((counts + tm - 1) // tm) * tm
    ends = jnp.cumsum(padded)
    offs = ends - padded
    pos = jnp.sum(oh * offs[None, :], axis=1) + rank
    n_rows = ((n * TOP_K + tm - 1) // tm) * tm + n_experts * tm
    row_token = jnp.zeros((n_rows,), jnp.int32).at[pos].set(jnp.arange(n * TOP_K, dtype=jnp.int32) // TOP_K)
    n_tiles = n_rows // tm
    n_valid = (ends[-1] // tm).astype(jnp.int32)
    tile_start = jnp.arange(n_tiles, dtype=jnp.int32) * tm
    tile_e = jnp.sum((tile_start[:, None] >= ends[None, :]).astype(jnp.int32), axis=1)
    last_e = jnp.sum((tile_start[n_valid - 1] >= ends).astype(jnp.int32))
    tile_e = jnp.where(jnp.arange(n_tiles) < n_valid, tile_e, last_e).astype(jnp.int32)
    return pos.astype(jnp.int32), row_token, tile_e, n_valid.reshape(1)


def _tile(pref, *dims):
    t = pref
    for dim in dims:
        t = math.gcd(t, dim)
    return t


def _rope_tables(seq, n_past, t_new, bs, dk, rot_dim):
    half = rot_dim // 2
    inv = ROPE_THETA ** (-(jnp.arange(half, dtype=F32) * 2.0) / rot_dim)
    pos = jnp.concatenate([jnp.arange(seq), jnp.tile(n_past + jnp.arange(t_new), bs)])
    ang = pos.astype(F32)[:, None] * inv[None, :]
    cos, sin = jnp.cos(ang), jnp.sin(ang)
    n = pos.shape[0]
    ones = jnp.ones((n, dk - rot_dim), F32)
    zeros_h = jnp.zeros((n, half), F32)
    zeros_r = jnp.zeros((n, dk - rot_dim), F32)
    cos_t = jnp.concatenate([cos, cos, ones], axis=1)
    sa_t = jnp.concatenate([zeros_h, sin, zeros_r], axis=1)
    sb_t = jnp.concatenate([-sin, zeros_h, zeros_r], axis=1)
    return cos_t, sa_t, sb_t


def _page_mats(heads, page, n_pages):
    w = heads * page
    dst = jnp.arange(w)
    d_head, d_slot = dst // page, dst % page
    same = d_head[:, None] == d_head[None, :]
    mcs = (same & (d_slot[:, None] <= d_slot[None, :])).astype(BF16)
    mbc = ((d_head[:, None] == d_head[None, :]) & (d_slot[:, None] == page - 1)).astype(BF16)
    msel = ((d_slot[:, None] == page - 1) & (d_head[:, None] == jnp.arange(LANES)[None, :])).astype(BF16)
    pg = jnp.arange(n_pages)
    lst = (pg[None, :] < pg[:, None]).astype(BF16)
    return mcs, mbc, msel, lst


def kernel(x_prompt, x_sample, cache_k_diff, cache_v_diff, cache_k_fox, cache_v_fox, cache_logf_fox, state_rglru_h, state_rglru_conv, page_table, norm_mix_g, w_in, b_fgate, lambda_q1, lambda_k1, lambda_q2, lambda_k2, subln_g, conv_w, conv_b, w_gate_a, b_gate_a, w_gate_x, b_gate_x, lru_lambda, w_out, norm_ffn_g, w_dense_gate, w_dense_up, w_dense_down, w_router, w_moe_gate, w_moe_up, w_moe_down, norm_final_g):
    bp, seq, d = x_prompt.shape
    bs, t_new, _ = x_sample.shape
    depth = w_in.shape[0]
    n_pool, page, h_a = cache_k_diff.shape[1], cache_k_diff.shape[2], cache_k_diff.shape[3]
    dk_a = cache_k_diff.shape[5]
    dv_a = cache_v_diff.shape[4]
    h_b, dh_b = cache_k_fox.shape[3], cache_k_fox.shape[4]
    w_c = state_rglru_h.shape[2]
    cw_len = conv_w.shape[1]
    n_pages = page_table.shape[1]
    n_past = n_pages * page
    rot_dim = dk_a // 4
    n_experts = w_router.shape[2]
    n_p, n_s = bp * seq, bs * t_new
    n = n_p + n_s
    assert t_new == SUBLANES and page == LANES and dv_a == 2 * dk_a and dk_a == LANES and dh_b == LANES

    w_qa = h_a * 2 * dk_a
    w_va = h_a * dv_a
    w_b = h_b * dh_b
    col_qa, col_ka, col_va = 0, w_qa, 2 * w_qa
    col_qb = col_va + w_va
    col_kb, col_vb = col_qb + w_b, col_qb + 2 * w_b
    col_g = col_vb + w_b
    col_x = col_g + w_c
    n_main = col_x + w_c
    src_fl = 2 * w_qa + w_va + 3 * w_b

    tm = _tile(TM_TOKENS, seq, n_s)
    tn = _tile(TN_PROJ, w_qa, src_fl, n_main - src_fl)
    tq = _tile(TQ_ATTN, seq)
    tt = _tile(TT_LRU, seq)
    tm_mix = _tile(TM_MIX, tm)
    tf = _tile(TF_FFN, w_dense_gate.shape[2])
    tm_moe = _tile(TM_MOE, n)
    tc = _tile(TC_COMBINE, n)

    rope = _rope_tables(seq, n_past, t_new, bs, dk_a, rot_dim)
    tri = jnp.tril(jnp.ones((seq, seq), BF16))
    page_mats = _page_mats(h_b, page, n_pages)

    kc_d = cache_k_diff.reshape(depth, n_pool, page * h_a * 2, dk_a)
    nc_a = dv_a // LANES
    vc_d = jnp.swapaxes(cache_v_diff.reshape(depth, n_pool, page, h_a, nc_a, LANES), 3, 4).reshape(
        depth, n_pool, page * nc_a * h_a, LANES)
    kc_f = cache_k_fox.reshape(depth, n_pool, page * h_b, dh_b)
    vc_f = cache_v_fox.reshape(depth, n_pool, page * h_b, dh_b)
    lf_pages = jnp.swapaxes(cache_logf_fox, 2, 3).reshape(depth, n_pool, h_b * page)
    n_group = _tile(PAGES_PER_STEP, n_pages)

    x = jnp.concatenate([x_prompt.reshape(n_p, d), x_sample.reshape(n_s, d)], axis=0)
    row2 = lambda v: v.reshape(1, -1).astype(F32)
    st = {k: [] for k in ('kd', 'vd', 'kf', 'vf', 'lf', 'h_p', 'h_s', 'cv')}
    y_p = y_s = None

    for l in range(depth):
        lam_init = 0.8 - 0.6 * math.exp(-0.3 * l)
        w_l = w_in[l]
        w_b2 = w_l[:, src_fl + h_b:].astype(BF16)
        b_fl = jnp.zeros((1, LANES), F32).at[0, :h_b].set(b_fgate[l])
        proj, logf = _proj_call(
            x, row2(norm_mix_g[l]), w_in, l, src_fl, w_b2, b_fl, rope, tm=tm, tn=tn, n_rope_cols=2 * w_qa,
            n_prompt_tiles=n_p // tm, rope_prompt_tiles=seq // tm, rot_half=rot_dim // 2)

        lams = (row2(lambda_q1[l]), row2(lambda_k1[l]), row2(lambda_q2[l]), row2(lambda_k2[l]))
        sg = row2(subln_g[l])

        oa_p = _diff_prompt_call(proj, lams, sg, batch=bp, seq=seq, heads=h_a, dk=dk_a, dv=dv_a, tq=tq,
                                 k_col=col_ka, v_col=col_va, lam_init=lam_init)
        c_col = _cumsum_call(logf, tri, bp, seq)
        c_row = jnp.swapaxes(c_col[:, :h_b].reshape(bp, seq, h_b), 1, 2).reshape(bp * h_b, seq // tq, tq)
        ob_p = _fox_prompt_call(proj, c_col, c_row, batch=bp, seq=seq, heads=h_b, dh=dh_b, tq=tq,
                                q_col=col_qb, k_col=col_kb, v_col=col_vb)
        lru_w = (conv_w[l], row2(conv_b[l]), w_gate_a[l].astype(BF16), row2(b_gate_a[l]),
                 w_gate_x[l].astype(BF16), row2(b_gate_x[l]), row2(lru_lambda[l]))
        oc_p, ht_p = _lru_prompt_call(proj, lru_w, batch=bp, seq=seq, wc=w_c, tt=tt, g_col=col_g, x_col=col_x)

        oa_s = _diff_sample_call(page_table, proj, kc_d, vc_d, lams, sg, l, n_prompt=n_p, t_new=t_new,
                                 heads=h_a, dk=dk_a, dv=dv_a, page=page, k_col=col_ka, v_col=col_va,
                                 lam_init=lam_init, n_group=n_group)
        cpast, ctot = _logf_pages_call(page_table, lf_pages[l], page_mats, heads=h_b, page=page)
        ob_s = _fox_sample_call(page_table, proj, logf, kc_f, vc_f, cpast, ctot, l, n_prompt=n_p,
                                t_new=t_new, heads=h_b, dh=dh_b, page=page, q_col=col_qb, k_col=col_kb,
                                v_col=col_vb, n_group=n_group)
        prev8 = jnp.pad(state_rglru_conv[l], ((0, 0), (SUBLANES - (cw_len - 1), 0), (0, 0))).reshape(n_s, w_c)
        h0rep = jnp.repeat(state_rglru_h[l], t_new, axis=0)
        oc_s, hs_s = _lru_sample_call(proj, prev8, h0rep, lru_w, n_prompt=n_p, n_sample=n_s, wc=w_c,
                                      g_col=col_g, x_col=col_x)

        br_p, br_s = (oa_p, ob_p, oc_p), (oa_s, ob_s, oc_s)
        g_ffn = row2(norm_ffn_g[l])
        m = l // 2
        if l % 2 == 0:
            x1, hn = _mix_call(br_p, br_s, w_out[l].astype(BF16), x, g_ffn, None, tm=tm_mix)
            x = _ffn_call(hn, w_dense_gate[m].astype(BF16), w_dense_up[m].astype(BF16),
                          w_dense_down[m].astype(BF16), x1, tm=tm, tf=tf)
            if l == depth - 1:
                y_all = _norm_call(x, row2(norm_final_g), tm=tm_mix)
                y_p, y_s = y_all[:n_p], y_all[n_p:]
        else:
            x1, rg, re = _mix_call(br_p, br_s, w_out[l].astype(BF16), x, g_ffn, w_router[m], tm=tm_mix)
            pos, row_token, tile_e, n_valid = _route_plan(re[:, :TOP_K], n_experts, tm_moe)
            xs = _gather_norm_call(row_token, n_valid, x1, g_ffn, tg=tm_moe)
            ys = _moe_ffn_call(tile_e, n_valid, xs, w_moe_gate[m].astype(BF16), w_moe_up[m].astype(BF16),
                               w_moe_down[m].astype(BF16), tm=tm_moe,
                               tf=_tile(TF_MOE, w_moe_gate.shape[3]))
            final = l == depth - 1
            x = _combine_norm_call(pos, x1, rg, ys, row2(norm_final_g), tc=tc, final_norm=final, n_prompt=n_p)
            if final:
                y_p, y_s = x

        st['kd'].append(proj[:, col_ka:col_ka + w_qa])
        st['vd'].append(proj[:, col_va:col_va + w_va])
        st['kf'].append(proj[:, col_kb:col_kb + w_b])
        st['vf'].append(proj[:, col_vb:col_vb + w_b])
        st['lf'].append(logf[:, :h_b])
        st['h_p'].append(ht_p.reshape(bp, w_c))
        st['h_s'].append(hs_s.reshape(bs, t_new, w_c)[:, t_new - 1])
        st['cv'].append(proj[:, col_x:col_x + w_c])

    def both(name, shape_tail):
        a = jnp.stack(st[name])
        return (a[:, :n_p].reshape((depth, bp, seq) + shape_tail),
                a[:, n_p:].reshape((depth, bs, t_new) + shape_tail))

    kd_p, kd_s = both('kd', (h_a, 2, dk_a))
    vd_p, vd_s = both('vd', (h_a, dv_a))
    kf_p, kf_s = both('kf', (h_b, dh_b))
    vf_p, vf_s = both('vf', (h_b, dh_b))
    lf_p, lf_s = both('lf', (h_b,))
    cv_p, cv_s = both('cv', (w_c,))
    keep = cw_len - 1
    return (y_p.reshape(bp, seq, d), y_s.reshape(bs, t_new, d),
            kd_p, vd_p, kf_p, vf_p, lf_p, jnp.stack(st['h_p']), cv_p[:, :, seq - keep:],
            kd_s, vd_s, kf_s, vf_s, lf_s, jnp.stack(st['h_s']), cv_s[:, :, t_new - keep:])
```

```python
import functools
import math

import jax
import jax.numpy as jnp
from jax import lax
from jax.experimental import pallas as pl
from jax.experimental.pallas import tpu as pltpu

F32 = jnp.float32
BF16 = jnp.bfloat16

LANES = 128
SUBLANES = 8
VMEM_LIMIT = 56 * 1024 * 1024

TM_TOKENS = 1024
TN_PROJ = 512
TQ_ATTN = 512
TT_LRU = 256
TM_MIX = 512
TF_FFN = 512
TF_MOE = 1408
TM_MOE = 512
TC_COMBINE = 256
PAGES_PER_STEP = 8

RMS_EPS = 1e-6
SUBLN_EPS = 1e-5
LRU_C = 8.0
ROPE_THETA = 500000.0
TOP_K = 2
NEG_INF = float("-inf")
LOG2E = 1.4426950408889634


def _params(sem, vmem=VMEM_LIMIT):
    return pltpu.CompilerParams(dimension_semantics=sem, vmem_limit_bytes=vmem)


def _split3(x):
    hi = x.astype(BF16)
    r = x - hi.astype(F32)
    mid = r.astype(BF16)
    lo = (r - mid.astype(F32)).astype(BF16)
    return hi, mid, lo


def _dot_sel(x, m):
    hi, mid, lo = _split3(x)
    d = lambda a: jnp.dot(a, m, preferred_element_type=F32)
    return d(hi) + d(mid) + d(lo)


def _sel_dot(m, x):
    hi, mid, lo = _split3(x)
    d = lambda a: jnp.dot(m, a, preferred_element_type=F32)
    return d(hi) + d(mid) + d(lo)


def _dot_nt(a, b):
    return lax.dot_general(a, b, (((1,), (1,)), ((), ())), preferred_element_type=F32)


def _sigmoid(x):
    return 1.0 / (1.0 + jnp.exp(-x))


def _softplus(x):
    return jnp.maximum(x, 0.0) + jnp.log1p(jnp.exp(-jnp.abs(x)))


def _rms(x, g, eps):
    ms = jnp.mean(x * x, axis=-1, keepdims=True)
    return x * lax.rsqrt(ms + eps) * g


def _proj_kernel(x_ref, g_ref, wa_ref, wb_ref, wfl_ref, bf_ref, cos_ref, sa_ref, sb_ref,
                 proj_ref, logf_ref, hn_ref, *, n_rope_tiles, n_a_tiles, tn, rot_half):
    j = pl.program_id(1)

    @pl.when(j == 0)
    def _():
        hn = _rms(x_ref[...], g_ref[...], RMS_EPS).astype(BF16)
        hn_ref[...] = hn
        z = jnp.dot(hn, wfl_ref[...].astype(BF16), preferred_element_type=F32) + bf_ref[...]
        logf_ref[...] = jnp.minimum(z, 0.0) - jnp.log1p(jnp.exp(-jnp.abs(z)))

    @pl.when(j < n_rope_tiles)
    def _():
        y = jnp.dot(hn_ref[...], wa_ref[...].astype(BF16), preferred_element_type=F32)
        c, sa, sb = cos_ref[...], sa_ref[...], sb_ref[...]
        for k in range(tn // LANES):
            yc = y[:, k * LANES:(k + 1) * LANES]
            proj_ref[:, k * LANES:(k + 1) * LANES] = (
                yc * c + pltpu.roll(yc, rot_half, 1) * sa + pltpu.roll(yc, LANES - rot_half, 1) * sb)

    @pl.when((j >= n_rope_tiles) & (j < n_a_tiles))
    def _():
        proj_ref[...] = jnp.dot(hn_ref[...], wa_ref[...].astype(BF16), preferred_element_type=F32)

    @pl.when(j >= n_a_tiles)
    def _():
        proj_ref[...] = jnp.dot(hn_ref[...], wb_ref[...], preferred_element_type=F32)


def _proj_call(x, g, w_in, layer, fl_col, wb, bfl, rope, *, tm, tn, n_rope_cols, n_prompt_tiles,
               rope_prompt_tiles, rot_half):
    n, d = x.shape
    na, nb = fl_col // tn, wb.shape[1] // tn
    n_main = (na + nb) * tn
    cos_t, sa_t, sb_t = rope
    rope_map = lambda i, j: (jnp.where(i < n_prompt_tiles, i % rope_prompt_tiles,
                                       rope_prompt_tiles + i - n_prompt_tiles), 0)
    kern = functools.partial(_proj_kernel, n_rope_tiles=n_rope_cols // tn, n_a_tiles=na, tn=tn,
                             rot_half=rot_half)
    return pl.pallas_call(
        kern,
        grid=(n // tm, n_main // tn),
        in_specs=[
            pl.BlockSpec((tm, d), lambda i, j: (i, 0)),
            pl.BlockSpec((1, d), lambda i, j: (0, 0)),
            pl.BlockSpec((None, d, tn), lambda i, j: (layer, 0, jnp.minimum(j, na - 1))),
            pl.BlockSpec((d, tn), lambda i, j: (0, jnp.maximum(j - na, 0))),
            pl.BlockSpec((None, d, LANES), lambda i, j: (layer, 0, fl_col // LANES)),
            pl.BlockSpec((1, LANES), lambda i, j: (0, 0)),
            pl.BlockSpec((tm, LANES), rope_map),
            pl.BlockSpec((tm, LANES), rope_map),
            pl.BlockSpec((tm, LANES), rope_map),
        ],
        out_specs=[
            pl.BlockSpec((tm, tn), lambda i, j: (i, j)),
            pl.BlockSpec((tm, LANES), lambda i, j: (i, 0)),
        ],
        out_shape=[jax.ShapeDtypeStruct((n, n_main), F32), jax.ShapeDtypeStruct((n, LANES), F32)],
        scratch_shapes=[pltpu.VMEM((tm, d), BF16)],
        compiler_params=_params(("parallel", "arbitrary")),
        name="proj",
    )(x, g, w_in, wb, w_in, bfl, cos_t, sa_t, sb_t)


def _cumsum_kernel(x_ref, tri_ref, o_ref):
    o_ref[...] = _sel_dot(tri_ref[...], x_ref[...])


def _cumsum_call(logf_p, tri, batch, seq):
    return pl.pallas_call(
        _cumsum_kernel,
        grid=(batch,),
        in_specs=[pl.BlockSpec((seq, LANES), lambda b: (b, 0)),
                  pl.BlockSpec((seq, seq), lambda b: (0, 0))],
        out_specs=pl.BlockSpec((seq, LANES), lambda b: (b, 0)),
        out_shape=jax.ShapeDtypeStruct((batch * seq, LANES), F32),
        compiler_params=_params(("parallel",)),
        name="cumsum_prompt",
    )(logf_p, tri)


def _lane_chunks(s):
    return [s[:, c * LANES:(c + 1) * LANES] for c in range(s.shape[1] // LANES)]


def _rowmax(s):
    return jnp.max(functools.reduce(jnp.maximum, _lane_chunks(s)), axis=-1, keepdims=True)


def _rowsum(s):
    return jnp.sum(functools.reduce(jnp.add, _lane_chunks(s)), axis=-1, keepdims=True)


def _softmax_step(s, v, m_ref, l_ref, acc_ref, idx):
    m_old = m_ref[idx]
    m_new = jnp.maximum(m_old, _rowmax(s))
    alpha = jnp.exp2(m_old - m_new)
    p_chunks = [jnp.exp2(c - m_new) for c in _lane_chunks(s)]
    l_ref[idx] = alpha * l_ref[idx] + functools.reduce(jnp.add, p_chunks)
    p = jnp.concatenate(p_chunks, axis=1).astype(BF16)
    pv = jnp.dot(p, v, preferred_element_type=F32)
    acc_ref[idx] = jnp.concatenate([alpha] * (pv.shape[1] // LANES), axis=1) * acc_ref[idx] + pv
    m_ref[idx] = m_new


def _softmax_denominator(l_ref, idx):
    return jnp.sum(l_ref[idx], axis=-1, keepdims=True)


def _softmax_step_stacked(s, v_heads, m_ref, l_ref, acc_ref, rows_per_head):
    m_old = m_ref[...]
    m_new = jnp.maximum(m_old, _rowmax(s))
    p = jnp.exp(s - m_new)
    alpha = jnp.exp(m_old - m_new)
    l_ref[...] = alpha * l_ref[...] + _rowsum(p)
    pv = [jnp.dot(p[h * rows_per_head:(h + 1) * rows_per_head, :].astype(BF16), v,
                  preferred_element_type=F32) for h, v in enumerate(v_heads)]
    acc_ref[...] = alpha * acc_ref[...] + jnp.concatenate(pv, axis=0)
    m_ref[...] = m_new


def _diff_lambda(lq1, lk1, lq2, lk2, lam_init):
    return (jnp.exp(jnp.sum(lq1[...] * lk1[...], axis=-1, keepdims=True))
            - jnp.exp(jnp.sum(lq2[...] * lk2[...], axis=-1, keepdims=True)) + lam_init)


def _diff_finish(acc_ref, l_ref, base, lam, sg, lam_init):
    o = (acc_ref[base] / _softmax_denominator(l_ref, base)
         - lam * (acc_ref[base + 1] / _softmax_denominator(l_ref, base + 1)))
    return _rms(o, sg, SUBLN_EPS) * (1.0 - lam_init)


def _diff_prompt_kernel(q_ref, k_ref, v_ref, lq1, lk1, lq2, lk2, sg_ref, *rest, tq, dk, heads, n_alias,
                        lam_init, scale):
    o_ref, kc_ref, vc_ref, kb_ref, vb_ref, m_ref, l_ref, acc_ref = rest[n_alias:]
    h = pl.program_id(1)
    qi = pl.program_id(2)
    seq = k_ref.shape[0]

    @pl.when(qi == 0)
    def _():
        k = k_ref[...]
        v = v_ref[...]
        for m in range(2):
            kb_ref[m] = k[:, m * dk:(m + 1) * dk].T.astype(BF16)
            kc_ref[pl.ds(2 * h + m, seq, stride=2 * heads), :] = k[:, m * dk:(m + 1) * dk]
        vb_ref[...] = v.astype(BF16)
        nc = v.shape[1] // LANES
        for c in range(nc):
            vc_ref[pl.ds(c * heads + h, seq, stride=nc * heads), :] = v[:, c * LANES:(c + 1) * LANES]

    m_ref[...] = jnp.full(m_ref.shape, NEG_INF, F32)
    l_ref[...] = jnp.zeros(l_ref.shape, F32)
    acc_ref[...] = jnp.zeros(acc_ref.shape, F32)
    q = (q_ref[...] * (scale * LOG2E)).astype(BF16)

    def block(kk, causal):
        k0 = pl.multiple_of(kk * tq, tq)
        vblk = vb_ref[pl.ds(k0, tq), :]
        for m in range(2):
            s = jnp.dot(q[:, m * dk:(m + 1) * dk], kb_ref[m, :, pl.ds(k0, tq)],
                        preferred_element_type=F32)
            if causal:
                row = lax.broadcasted_iota(jnp.int32, s.shape, 0)
                col = lax.broadcasted_iota(jnp.int32, s.shape, 1)
                s = jnp.where(col <= row, s, NEG_INF)
            _softmax_step(s, vblk, m_ref, l_ref, acc_ref, m)

    def body(kk, carry):
        block(kk, False)
        return carry

    lax.fori_loop(0, qi, body, 0)
    block(qi, True)

    lam = _diff_lambda(lq1, lk1, lq2, lk2, lam_init)
    o_ref[...] = _diff_finish(acc_ref, l_ref, 0, lam, sg_ref[...], lam_init).astype(o_ref.dtype)


def _cache_out(prev, depth, rows_total, rows_block, layer):
    out_spec = pl.BlockSpec((None, rows_block, LANES), lambda *idx: (layer, idx[0], 0))
    out_shape = jax.ShapeDtypeStruct((depth, rows_total, LANES), F32)
    return out_spec, out_shape, ([] if prev is None else [prev])


def _diff_prompt_call(proj, lams, sg, prev, layer, depth, *, batch, seq, heads, dk, dv, tq, k_col, v_col,
                      lam_init):
    nq = seq // tq
    n_rows = batch * seq
    lam_spec = pl.BlockSpec((1, dk), lambda b, h, q: (0, 0))
    rows_k, rows_v = seq * 2 * heads, seq * heads * dv // LANES
    ks, kshape, kprev = _cache_out(None if prev is None else prev[0], depth, batch * rows_k, rows_k, layer)
    vs, vshape, vprev = _cache_out(None if prev is None else prev[1], depth, batch * rows_v, rows_v, layer)
    aliased = kprev + vprev
    n_in = 8
    kern = functools.partial(_diff_prompt_kernel, tq=tq, dk=dk, heads=heads, n_alias=len(aliased),
                             lam_init=lam_init, scale=dk ** -0.5)
    return pl.pallas_call(
        kern,
        grid=(batch, heads, nq),
        in_specs=[
            pl.BlockSpec((tq, 2 * dk), lambda b, h, q: (b * nq + q, h)),
            pl.BlockSpec((seq, 2 * dk), lambda b, h, q: (b, k_col // (2 * dk) + h)),
            pl.BlockSpec((seq, dv), lambda b, h, q: (b, v_col // dv + h)),
            lam_spec, lam_spec, lam_spec, lam_spec,
            pl.BlockSpec((1, dv), lambda b, h, q: (0, 0)),
        ] + [pl.BlockSpec(memory_space=pl.ANY)] * len(aliased),
        out_specs=[pl.BlockSpec((tq, dv), lambda b, h, q: (b * nq + q, h)), ks, vs],
        out_shape=[jax.ShapeDtypeStruct((n_rows, heads * dv), BF16), kshape, vshape],
        input_output_aliases={n_in + i: 1 + i for i in range(len(aliased))},
        scratch_shapes=[
            pltpu.VMEM((2, dk, seq), BF16), pltpu.VMEM((seq, dv), BF16),
            pltpu.VMEM((2, tq, LANES), F32), pltpu.VMEM((2, tq, LANES), F32), pltpu.VMEM((2, tq, dv), F32),
        ],
        compiler_params=_params(("parallel", "arbitrary", "arbitrary")),
        name="diff_prompt",
    )(proj, proj, proj, *lams, sg, *aliased)


def _fox_prompt_kernel(q_ref, k_ref, v_ref, cc_ref, cr_ref, *rest, tq, heads, n_alias, scale):
    o_ref, kc_ref, vc_ref, kb_ref, vb_ref, m_ref, l_ref, acc_ref = rest[n_alias:]
    h = pl.program_id(1)
    qi = pl.program_id(2)
    seq = k_ref.shape[0]

    @pl.when(qi == 0)
    def _():
        k = k_ref[...]
        v = v_ref[...]
        kb_ref[...] = k.T.astype(BF16)
        vb_ref[...] = v.astype(BF16)
        kc_ref[pl.ds(h, seq, stride=heads), :] = k
        vc_ref[pl.ds(h, seq, stride=heads), :] = v

    m_ref[...] = jnp.full(m_ref.shape, NEG_INF, F32)
    l_ref[...] = jnp.zeros(l_ref.shape, F32)
    acc_ref[...] = jnp.zeros(acc_ref.shape, F32)
    q = (q_ref[...] * (scale * LOG2E)).astype(BF16)
    cc = cc_ref[...]
    lane = lax.broadcasted_iota(jnp.int32, cc.shape, 1)
    cq = jnp.sum(jnp.where(lane == h, cc, 0.0), axis=-1, keepdims=True) * LOG2E

    def block(kk, causal):
        k0 = pl.multiple_of(kk * tq, tq)
        ck = cr_ref[0, pl.ds(kk, 1), :] * LOG2E
        s = jnp.dot(q, kb_ref[:, pl.ds(k0, tq)], preferred_element_type=F32) + (cq - ck)
        if causal:
            row = lax.broadcasted_iota(jnp.int32, s.shape, 0)
            col = lax.broadcasted_iota(jnp.int32, s.shape, 1)
            s = jnp.where(col <= row, s, NEG_INF)
        _softmax_step(s, vb_ref[pl.ds(k0, tq), :], m_ref, l_ref, acc_ref, 0)

    def body(kk, carry):
        block(kk, False)
        return carry

    lax.fori_loop(0, qi, body, 0)
    block(qi, True)
    o_ref[...] = (acc_ref[0] / _softmax_denominator(l_ref, 0)).astype(o_ref.dtype)


def _fox_prompt_call(proj, c_col, c_row, prev, layer, depth, *, batch, seq, heads, dh, tq, q_col, k_col, v_col):
    nq = seq // tq
    rows = seq * heads
    ks, kshape, kprev = _cache_out(None if prev is None else prev[0], depth, batch * rows, rows, layer)
    vs, vshape, vprev = _cache_out(None if prev is None else prev[1], depth, batch * rows, rows, layer)
    aliased = kprev + vprev
    n_in = 5
    kern = functools.partial(_fox_prompt_kernel, tq=tq, heads=heads, n_alias=len(aliased), scale=dh ** -0.5)
    return pl.pallas_call(
        kern,
        grid=(batch, heads, nq),
        in_specs=[
            pl.BlockSpec((tq, dh), lambda b, h, q: (b * nq + q, q_col // dh + h)),
            pl.BlockSpec((seq, dh), lambda b, h, q: (b, k_col // dh + h)),
            pl.BlockSpec((seq, dh), lambda b, h, q: (b, v_col // dh + h)),
            pl.BlockSpec((tq, LANES), lambda b, h, q: (b * nq + q, 0)),
            pl.BlockSpec((1, nq, tq), lambda b, h, q: (b * heads + h, 0, 0)),
        ] + [pl.BlockSpec(memory_space=pl.ANY)] * len(aliased),
        out_specs=[pl.BlockSpec((tq, dh), lambda b, h, q: (b * nq + q, h)), ks, vs],
        out_shape=[jax.ShapeDtypeStruct((batch * seq, heads * dh), BF16), kshape, vshape],
        input_output_aliases={n_in + i: 1 + i for i in range(len(aliased))},
        scratch_shapes=[
            pltpu.VMEM((dh, seq), BF16), pltpu.VMEM((seq, dh), BF16),
            pltpu.VMEM((1, tq, LANES), F32), pltpu.VMEM((1, tq, LANES), F32), pltpu.VMEM((1, tq, dh), F32),
        ],
        compiler_params=_params(("parallel", "arbitrary", "arbitrary")),
        name="fox_prompt",
    )(proj, proj, proj, c_col, c_row, *aliased)


def _pad_rows(x, rows):
    return jnp.concatenate([x, jnp.zeros((rows - x.shape[0], x.shape[1]), x.dtype)], axis=0)


def _causal_new(s, t_new):
    row = lax.broadcasted_iota(jnp.int32, s.shape, 0) % t_new
    col = lax.broadcasted_iota(jnp.int32, s.shape, 1)
    return jnp.where(col <= row, s, NEG_INF)


def _diff_sample_kernel(pt_ref, q_ref, kn_ref, vn_ref, *refs, heads, dk, dv, page, n_group, lam_init, scale):
    del pt_ref
    kc_refs, vc_refs = refs[:n_group], refs[n_group:2 * n_group]
    lq1, lk1, lq2, lk2, sg_ref, o_ref, m_ref, l_ref, acc_ref = refs[2 * n_group:]
    p = pl.program_id(1)
    last = pl.num_programs(1) - 1
    t_new = q_ref.shape[0]
    n_maps = 2 * heads
    nc = dv // LANES

    @pl.when(p == 0)
    def _():
        m_ref[...] = jnp.full(m_ref.shape, NEG_INF, F32)
        l_ref[...] = jnp.zeros(l_ref.shape, F32)
        acc_ref[...] = jnp.zeros(acc_ref.shape, F32)

    q = q_ref[...]
    qb = [q[:, hm * dk:(hm + 1) * dk].astype(BF16) for hm in range(n_maps)]

    s = jnp.concatenate([
        jnp.concatenate([_dot_nt(qb[hm], kc[pl.ds(hm, page, stride=n_maps), :].astype(BF16)) for kc in kc_refs],
                        axis=1) for hm in range(n_maps)], axis=0) * scale
    v_heads = [jnp.concatenate([
        jnp.concatenate([vc[pl.ds(c * heads + h, page, stride=heads * nc), :] for c in range(nc)], axis=1)
        for vc in vc_refs], axis=0).astype(BF16) for h in range(heads)]
    _softmax_step_stacked(s, v_heads, m_ref, l_ref, acc_ref, 2 * t_new)

    @pl.when(p == last)
    def _():
        kn = kn_ref[...]
        vn = vn_ref[...]
        s_new = jnp.concatenate([
            _dot_nt(qb[hm], _pad_rows(kn[:, hm * dk:(hm + 1) * dk], page).astype(BF16))
            for hm in range(n_maps)], axis=0) * scale
        v_new = [_pad_rows(vn[:, h * dv:(h + 1) * dv], page).astype(BF16) for h in range(heads)]
        _softmax_step_stacked(_causal_new(s_new, t_new), v_new, m_ref, l_ref, acc_ref, 2 * t_new)
        lam = _diff_lambda(lq1, lk1, lq2, lk2, lam_init)
        o = acc_ref[...] / l_ref[...]
        for h in range(heads):
            r0 = 2 * h * t_new
            oh = o[r0:r0 + t_new, :] - lam * o[r0 + t_new:r0 + 2 * t_new, :]
            o_ref[:, h * dv:(h + 1) * dv] = _rms(oh, sg_ref[...], SUBLN_EPS) * (1.0 - lam_init)


def _diff_sample_call(page_table, proj, kc, vc, lams, sg, layer, *, n_prompt, t_new, heads, dk, dv,
                      page, k_col, v_col, lam_init, n_group):
    bs, n_pages = page_table.shape
    w_q = heads * 2 * dk
    row0 = n_prompt // t_new
    lam_spec = pl.BlockSpec((1, dk), lambda b, p, pt: (0, 0))
    kern = functools.partial(_diff_sample_kernel, heads=heads, dk=dk, dv=dv, page=page, n_group=n_group,
                             lam_init=lam_init, scale=dk ** -0.5)

    def page_spec(rows, g):
        return pl.BlockSpec((None, None, rows, LANES), lambda b, p, pt: (layer, pt[b, p * n_group + g], 0, 0))

    grid_spec = pltpu.PrefetchScalarGridSpec(
        num_scalar_prefetch=1,
        grid=(bs, n_pages // n_group),
        in_specs=[
            pl.BlockSpec((t_new, w_q), lambda b, p, pt: (row0 + b, 0)),
            pl.BlockSpec((t_new, w_q), lambda b, p, pt: (row0 + b, k_col // w_q)),
            pl.BlockSpec((t_new, heads * dv), lambda b, p, pt: (row0 + b, v_col // (heads * dv))),
            *[page_spec(page * heads * 2, g) for g in range(n_group)],
            *[page_spec(page * heads * dv // LANES, g) for g in range(n_group)],
            lam_spec, lam_spec, lam_spec, lam_spec,
            pl.BlockSpec((1, dv), lambda b, p, pt: (0, 0)),
        ],
        out_specs=pl.BlockSpec((t_new, heads * dv), lambda b, p, pt: (b, 0)),
        scratch_shapes=[
            pltpu.VMEM((2 * heads * t_new, 1), F32), pltpu.VMEM((2 * heads * t_new, 1), F32),
            pltpu.VMEM((2 * heads * t_new, dv), F32),
        ],
    )
    return pl.pallas_call(
        kern,
        grid_spec=grid_spec,
        out_shape=jax.ShapeDtypeStruct((bs * t_new, heads * dv), F32),
        compiler_params=_params(("parallel", "arbitrary")),
        name="diff_sample",
    )(page_table, proj, proj, proj, *([kc] * n_group), *([vc] * n_group), *lams, sg)


def _logf_pages_kernel(pt_ref, lf_ref, mcs_ref, mbc_ref, msel_ref, lst_ref, cp_ref, ct_ref, g_ref,
                       *, n_pages, heads, page):
    b = pl.program_id(0)
    for p in range(n_pages):
        g_ref[pl.ds(p, 1), :] = lf_ref[pl.ds(pt_ref[b, p], 1), :]
    x = g_ref[...]
    y = _dot_sel(x, mcs_ref[...])
    tot = _dot_sel(y, mbc_ref[...])
    c = y + _sel_dot(lst_ref[...], tot)
    for h in range(heads):
        cp_ref[0, h * n_pages:(h + 1) * n_pages, :] = c[:, h * page:(h + 1) * page]
    ct_ref[0] = _dot_sel(c, msel_ref[...])[n_pages - 1:n_pages, :]


def _logf_pages_call(page_table, lf2, mats, *, heads, page):
    bs, n_pages = page_table.shape
    n_pool, w = lf2.shape
    mcs, mbc, msel, lst = mats
    full = lambda a: pl.BlockSpec(a.shape, lambda b, pt: (0,) * a.ndim)
    kern = functools.partial(_logf_pages_kernel, n_pages=n_pages, heads=heads, page=page)
    grid_spec = pltpu.PrefetchScalarGridSpec(
        num_scalar_prefetch=1,
        grid=(bs,),
        in_specs=[full(lf2), full(mcs), full(mbc), full(msel), full(lst)],
        out_specs=[pl.BlockSpec((1, heads * n_pages, page), lambda b, pt: (b, 0, 0)),
                   pl.BlockSpec((1, 1, LANES), lambda b, pt: (b, 0, 0))],
        scratch_shapes=[pltpu.VMEM((n_pages, w), F32)],
    )
    return pl.pallas_call(
        kern,
        grid_spec=grid_spec,
        out_shape=[jax.ShapeDtypeStruct((bs, heads * n_pages, page), F32),
                   jax.ShapeDtypeStruct((bs, 1, LANES), F32)],
        compiler_params=_params(("arbitrary",)),
        name="logf_pages",
    )(page_table, lf2, mcs, mbc, msel, lst)


def _fox_sample_kernel(pt_ref, q_ref, kn_ref, vn_ref, lfn_ref, *refs, heads, dh, page, n_pages, n_group, scale):
    del pt_ref
    kc_refs, vc_refs = refs[:n_group], refs[n_group:2 * n_group]
    cp_ref, ct_ref, o_ref, m_ref, l_ref, acc_ref, cn_ref = refs[2 * n_group:]
    p = pl.program_id(1)
    last = pl.num_programs(1) - 1
    t_new = q_ref.shape[0]

    @pl.when(p == 0)
    def _():
        m_ref[...] = jnp.full(m_ref.shape, NEG_INF, F32)
        l_ref[...] = jnp.zeros(l_ref.shape, F32)
        acc_ref[...] = jnp.zeros(acc_ref.shape, F32)
        x = lfn_ref[...]
        row = lax.broadcasted_iota(jnp.int32, x.shape, 0)
        k = 1
        while k < x.shape[0]:
            x = x + jnp.where(row >= k, pltpu.roll(x, k, 0), 0.0)
            k *= 2
        cn_ref[...] = ct_ref[0] + x

    q = q_ref[...]
    cn = cn_ref[...]
    qb = [q[:, h * dh:(h + 1) * dh].astype(BF16) for h in range(heads)]
    cq = [cn[:, h:h + 1] for h in range(heads)]

    s = jnp.concatenate([
        jnp.concatenate([_dot_nt(qb[h], kc[pl.ds(h, page, stride=heads), :].astype(BF16)) for kc in kc_refs],
                        axis=1) for h in range(heads)], axis=0) * scale
    bias = jnp.concatenate([
        cq[h] - jnp.concatenate([cp_ref[0, pl.ds(h * n_pages + p * n_group + g, 1), :] for g in range(n_group)],
                                axis=1) for h in range(heads)], axis=0)
    v_heads = [jnp.concatenate([vc[pl.ds(h, page, stride=heads), :] for vc in vc_refs], axis=0).astype(BF16)
               for h in range(heads)]
    _softmax_step_stacked(s + bias, v_heads, m_ref, l_ref, acc_ref, t_new)

    @pl.when(p == last)
    def _():
        kn = kn_ref[...]
        vn = vn_ref[...]
        s_new = jnp.concatenate([
            _dot_nt(qb[h], _pad_rows(kn[:, h * dh:(h + 1) * dh], page).astype(BF16)) for h in range(heads)],
            axis=0) * scale
        row = lax.broadcasted_iota(jnp.int32, (t_new, page), 0)
        col = lax.broadcasted_iota(jnp.int32, (t_new, page), 1)
        bias_new = jnp.concatenate([
            cq[h] - jnp.sum(jnp.where(row == col, cq[h], 0.0), axis=0, keepdims=True) for h in range(heads)],
            axis=0)
        v_new = [_pad_rows(vn[:, h * dh:(h + 1) * dh], page).astype(BF16) for h in range(heads)]
        _softmax_step_stacked(_causal_new(s_new + bias_new, t_new), v_new, m_ref, l_ref, acc_ref, t_new)
        o = acc_ref[...] / l_ref[...]
        for h in range(heads):
            o_ref[:, h * dh:(h + 1) * dh] = o[h * t_new:(h + 1) * t_new, :]


def _fox_sample_call(page_table, proj, logf, kc, vc, cpast, ctot, layer, *, n_prompt, t_new, heads, dh,
                     page, q_col, k_col, v_col, n_group):
    bs, n_pages = page_table.shape
    w = heads * dh
    row0 = n_prompt // t_new
    kern = functools.partial(_fox_sample_kernel, heads=heads, dh=dh, page=page, n_pages=n_pages,
                             n_group=n_group, scale=dh ** -0.5)

    def page_spec(g):
        return pl.BlockSpec((None, None, page * heads, dh),
                            lambda b, p, pt: (layer, pt[b, p * n_group + g], 0, 0))

    grid_spec = pltpu.PrefetchScalarGridSpec(
        num_scalar_prefetch=1,
        grid=(bs, n_pages // n_group),
        in_specs=[
            pl.BlockSpec((t_new, w), lambda b, p, pt: (row0 + b, q_col // w)),
            pl.BlockSpec((t_new, w), lambda b, p, pt: (row0 + b, k_col // w)),
            pl.BlockSpec((t_new, w), lambda b, p, pt: (row0 + b, v_col // w)),
            pl.BlockSpec((t_new, LANES), lambda b, p, pt: (row0 + b, 0)),
            *[page_spec(g) for g in range(n_group)],
            *[page_spec(g) for g in range(n_group)],
            pl.BlockSpec((1, heads * n_pages, page), lambda b, p, pt: (b, 0, 0)),
            pl.BlockSpec((1, 1, LANES), lambda b, p, pt: (b, 0, 0)),
        ],
        out_specs=pl.BlockSpec((t_new, w), lambda b, p, pt: (b, 0)),
        scratch_shapes=[
            pltpu.VMEM((heads * t_new, 1), F32), pltpu.VMEM((heads * t_new, 1), F32),
            pltpu.VMEM((heads * t_new, dh), F32), pltpu.VMEM((t_new, LANES), F32),
        ],
    )
    return pl.pallas_call(
        kern,
        grid_spec=grid_spec,
        out_shape=jax.ShapeDtypeStruct((bs * t_new, w), F32),
        compiler_params=_params(("parallel", "arbitrary")),
        name="fox_sample",
    )(page_table, proj, proj, proj, logf, *([kc] * n_group), *([vc] * n_group), cpast, ctot)


def _lru_coeffs(x, xprev_ext, gate_w, cw_ref, cb_ref, wga_ref, bga_ref, wgx_ref, bgx_ref, lam_ref, conv_w):
    rows = x.shape[0]
    nb = wga_ref.shape[0]
    bw = wga_ref.shape[1]
    u = xprev_ext(conv_w - 1) * cw_ref[0:1, :]
    for k in range(1, conv_w - 1):
        u = u + xprev_ext(conv_w - 1 - k) * cw_ref[k:k + 1, :]
    u = u + x * cw_ref[conv_w - 1:conv_w, :]
    u = u + cb_ref[...]
    ub = u.astype(BF16)
    r_parts, i_parts = [], []
    for n in range(nb):
        un = ub[:, n * bw:(n + 1) * bw]
        r_parts.append(jnp.dot(un, wga_ref[n], preferred_element_type=F32))
        i_parts.append(jnp.dot(un, wgx_ref[n], preferred_element_type=F32))
    r = _sigmoid(jnp.concatenate(r_parts, axis=-1) + bga_ref[...])
    i = _sigmoid(jnp.concatenate(i_parts, axis=-1) + bgx_ref[...])
    log_a = -LRU_C * r * _softplus(-lam_ref[...])
    a = jnp.exp(log_a)
    b = jnp.sqrt(-jnp.tanh(log_a) * (a * a + 1.0)) * (i * u)
    del rows, gate_w
    return a, b


def _group_scan(a, b):
    row = lax.broadcasted_iota(jnp.int32, a.shape, 0) % SUBLANES
    k = 1
    while k < SUBLANES:
        keep = row >= k
        a_sh = jnp.where(keep, pltpu.roll(a, k, 0), 1.0)
        b_sh = jnp.where(keep, pltpu.roll(b, k, 0), 0.0)
        b = a * b_sh + b
        a = a * a_sh
        k *= 2
    return a, b


def _gelu_tanh(x):
    return 0.5 * x * (1.0 + jnp.tanh(math.sqrt(2.0 / math.pi) * (x + 0.044715 * (x * x * x))))


def _lru_prompt_kernel(x_ref, g_ref, cw_ref, cb_ref, wga_ref, bga_ref, wgx_ref, bgx_ref, lam_ref,
                       y_ref, ht_ref, tail_ref, hl_ref, a_ref, b_ref, hs_ref, *, conv_w):
    ti = pl.program_id(1)

    @pl.when(ti == 0)
    def _():
        tail_ref[...] = jnp.zeros(tail_ref.shape, F32)
        hl_ref[...] = jnp.zeros(hl_ref.shape, F32)

    x = x_ref[...]
    tt = x.shape[0]
    xe = jnp.concatenate([tail_ref[...], x], axis=0)
    delayed = lambda k: pltpu.roll(xe, k, 0)[SUBLANES:, :]
    a, b = _lru_coeffs(x, delayed, None, cw_ref, cb_ref, wga_ref, bga_ref, wgx_ref, bgx_ref, lam_ref, conv_w)
    a, b = _group_scan(a, b)
    a_ref[...] = a
    b_ref[...] = b
    tail_ref[...] = x[tt - SUBLANES:, :]

    def body(g, hlast):
        r0 = pl.multiple_of(g * SUBLANES, SUBLANES)
        hg = b_ref[pl.ds(r0, SUBLANES), :] + a_ref[pl.ds(r0, SUBLANES), :] * hlast
        hs_ref[pl.ds(r0, SUBLANES), :] = hg
        return hg[SUBLANES - 1:SUBLANES, :]

    hl = lax.fori_loop(0, tt // SUBLANES, body, hl_ref[...])
    hl_ref[...] = hl
    ht_ref[0] = hl
    y_ref[...] = (_gelu_tanh(g_ref[...]) * hs_ref[...]).astype(y_ref.dtype)


def _lru_prompt_call(proj, wts, *, batch, seq, wc, tt, g_col, x_col):
    cw, cb, wga, bga, wgx, bgx, lam = wts
    nt = seq // tt
    conv_w = cw.shape[0]
    full = lambda a: pl.BlockSpec(a.shape, lambda b, t: (0,) * a.ndim)
    kern = functools.partial(_lru_prompt_kernel, conv_w=conv_w)
    return pl.pallas_call(
        kern,
        grid=(batch, nt),
        in_specs=[
            pl.BlockSpec((tt, wc), lambda b, t: (b * nt + t, x_col // wc)),
            pl.BlockSpec((tt, wc), lambda b, t: (b * nt + t, g_col // wc)),
            full(cw), full(cb), full(wga), full(bga), full(wgx), full(bgx), full(lam),
        ],
        out_specs=[pl.BlockSpec((tt, wc), lambda b, t: (b * nt + t, 0)),
                   pl.BlockSpec((1, 1, wc), lambda b, t: (b, 0, 0))],
        out_shape=[jax.ShapeDtypeStruct((batch * seq, wc), BF16),
                   jax.ShapeDtypeStruct((batch, 1, wc), F32)],
        scratch_shapes=[pltpu.VMEM((SUBLANES, wc), F32), pltpu.VMEM((1, wc), F32),
                        pltpu.VMEM((tt, wc), F32), pltpu.VMEM((tt, wc), F32), pltpu.VMEM((tt, wc), F32)],
        compiler_params=_params(("parallel", "arbitrary")),
        name="lru_prompt",
    )(proj, proj, cw, cb, wga, bga, wgx, bgx, lam)


def _lru_sample_kernel(x_ref, g_ref, prev_ref, h0_ref, cw_ref, cb_ref, wga_ref, bga_ref, wgx_ref, bgx_ref,
                       lam_ref, y_ref, hs_ref, *, conv_w):
    x = x_ref[...]
    rows = x.shape[0]
    row = lax.broadcasted_iota(jnp.int32, x.shape, 0) % SUBLANES
    prev = prev_ref[...]

    def delayed(k):
        return jnp.where(row >= k, pltpu.roll(x, k, 0), pltpu.roll(prev, (k - SUBLANES) % rows, 0))

    a, b = _lru_coeffs(x, delayed, None, cw_ref, cb_ref, wga_ref, bga_ref, wgx_ref, bgx_ref, lam_ref, conv_w)
    a, b = _group_scan(a, b)
    hs = b + a * h0_ref[...]
    hs_ref[...] = hs
    y_ref[...] = _gelu_tanh(g_ref[...]) * hs


def _lru_sample_call(proj, prev8, h0rep, wts, *, n_prompt, n_sample, wc, g_col, x_col):
    cw, cb, wga, bga, wgx, bgx, lam = wts
    conv_w = cw.shape[0]
    rb = n_prompt // n_sample
    full = lambda a: pl.BlockSpec(a.shape, lambda i: (0,) * a.ndim)
    kern = functools.partial(_lru_sample_kernel, conv_w=conv_w)
    return pl.pallas_call(
        kern,
        grid=(1,),
        in_specs=[
            pl.BlockSpec((n_sample, wc), lambda i: (rb, x_col // wc)),
            pl.BlockSpec((n_sample, wc), lambda i: (rb, g_col // wc)),
            full(prev8), full(h0rep),
            full(cw), full(cb), full(wga), full(bga), full(wgx), full(bgx), full(lam),
        ],
        out_specs=[pl.BlockSpec((n_sample, wc), lambda i: (0, 0)),
                   pl.BlockSpec((n_sample, wc), lambda i: (0, 0))],
        out_shape=[jax.ShapeDtypeStruct((n_sample, wc), F32), jax.ShapeDtypeStruct((n_sample, wc), F32)],
        compiler_params=_params(("arbitrary",)),
        name="lru_sample",
    )(proj, proj, prev8, h0rep, cw, cb, wga, bga, wgx, bgx, lam)


def _mix_kernel(ap_ref, bp_ref, cp_ref, as_ref, bs_ref, cs_ref, w_ref, x_ref, g_ref, *rest,
                n_experts, n_prompt_tiles):
    if n_experts:
        wrh_ref, wrl_ref, x1_ref, rg_ref, re_ref = rest
    else:
        x1_ref, hn_ref = rest
    i = pl.program_id(0)
    wa, wb = ap_ref.shape[1], bp_ref.shape[1]

    def project(a_ref, b_ref, c_ref):
        d = lambda v, lo, hi: jnp.dot(v[...].astype(BF16), w_ref[lo:hi, :], preferred_element_type=F32)
        return d(a_ref, 0, wa) + d(b_ref, wa, wa + wb) + d(c_ref, wa + wb, w_ref.shape[0])

    @pl.when(i < n_prompt_tiles)
    def _():
        x1_ref[...] = x_ref[...] + project(ap_ref, bp_ref, cp_ref)

    @pl.when(i >= n_prompt_tiles)
    def _():
        x1_ref[...] = x_ref[...] + project(as_ref, bs_ref, cs_ref)

    hn = _rms(x1_ref[...], g_ref[...], RMS_EPS)
    if not n_experts:
        hn_ref[...] = hn.astype(BF16)
    else:
        h_hi = hn.astype(BF16)
        h_lo = (hn - h_hi.astype(F32)).astype(BF16)
        d = lambda a, b: jnp.dot(a, b[...], preferred_element_type=F32)
        logits = d(h_hi, wrh_ref) + (d(h_lo, wrh_ref) + d(h_hi, wrl_ref))
        lane = lax.broadcasted_iota(jnp.int32, logits.shape, 1)
        lane_f = lane.astype(F32)
        big = float(LANES)
        lg = jnp.where(lane < n_experts, logits, NEG_INF)
        v1 = jnp.max(lg, axis=-1, keepdims=True)
        i1 = jnp.min(jnp.where(lg == v1, lane_f, big), axis=-1, keepdims=True)
        lg2 = jnp.where(lane_f == i1, NEG_INF, lg)
        v2 = jnp.max(lg2, axis=-1, keepdims=True)
        i2 = jnp.min(jnp.where(lg2 == v2, lane_f, big), axis=-1, keepdims=True)
        e = jnp.exp(v2 - v1)
        g1 = 1.0 / (1.0 + e)
        g2 = e / (1.0 + e)
        rg_ref[...] = jnp.where(lane == 0, g1, jnp.where(lane == 1, g2, 0.0))
        re_ref[...] = jnp.where(lane == 0, i1, jnp.where(lane == 1, i2, 0.0)).astype(jnp.int32)


def _mix_call(branches_p, branches_s, w_out, x, g, w_router, *, tm):
    n, d = x.shape
    npt = branches_p[0].shape[0] // tm
    n_experts = 0 if w_router is None else w_router.shape[1]
    kern = functools.partial(_mix_kernel, n_experts=n_experts, n_prompt_tiles=npt)
    row = lambda c: pl.BlockSpec((tm, c), lambda i: (i, 0))
    row_p = lambda a: pl.BlockSpec((tm, a.shape[1]), lambda i: (jnp.minimum(i, npt - 1), 0))
    row_s = lambda a: pl.BlockSpec((tm, a.shape[1]), lambda i: (jnp.maximum(i - npt, 0), 0))
    in_specs = ([row_p(a) for a in branches_p] + [row_s(a) for a in branches_s]
                + [pl.BlockSpec((d, d), lambda i: (0, 0)), row(d), pl.BlockSpec((1, d), lambda i: (0, 0))])
    args = [*branches_p, *branches_s, w_out, x, g]
    if n_experts:
        wr = jnp.zeros((d, LANES), F32).at[:, :n_experts].set(w_router)
        wr_hi = wr.astype(BF16)
        wr_lo = (wr - wr_hi.astype(F32)).astype(BF16)
        in_specs += [pl.BlockSpec((d, LANES), lambda i: (0, 0))] * 2
        out_specs = [row(d), row(LANES), row(LANES)]
        out_shape = [jax.ShapeDtypeStruct((n, d), F32), jax.ShapeDtypeStruct((n, LANES), F32),
                     jax.ShapeDtypeStruct((n, LANES), jnp.int32)]
        args += [wr_hi, wr_lo]
    else:
        out_specs = [row(d), row(d)]
        out_shape = [jax.ShapeDtypeStruct((n, d), F32), jax.ShapeDtypeStruct((n, d), BF16)]
    return pl.pallas_call(
        kern,
        grid=(n // tm,),
        in_specs=in_specs,
        out_specs=out_specs,
        out_shape=out_shape,
        compiler_params=_params(("parallel",)),
        name="mix_router" if n_experts else "mix",
    )(*args)


def _swiglu_acc(h, wg, wu, wd):
    g = jnp.dot(h, wg, preferred_element_type=F32)
    u = jnp.dot(h, wu, preferred_element_type=F32)
    a = (g * _sigmoid(g) * u).astype(BF16)
    return jnp.dot(a, wd, preferred_element_type=F32)


def _ffn_kernel(h_ref, wg_ref, wu_ref, wd_ref, x_ref, o_ref):
    j = pl.program_id(1)

    @pl.when(j == 0)
    def _():
        o_ref[...] = x_ref[...]

    o_ref[...] += _swiglu_acc(h_ref[...], wg_ref[...], wu_ref[...], wd_ref[...])


def _ffn_call(hn, wg, wu, wd, x1, *, tm, tf):
    n, d = x1.shape
    f = wg.shape[1]
    return pl.pallas_call(
        _ffn_kernel,
        grid=(n // tm, f // tf),
        in_specs=[
            pl.BlockSpec((tm, d), lambda i, j: (i, 0)),
            pl.BlockSpec((d, tf), lambda i, j: (0, j)),
            pl.BlockSpec((d, tf), lambda i, j: (0, j)),
            pl.BlockSpec((tf, d), lambda i, j: (j, 0)),
            pl.BlockSpec((tm, d), lambda i, j: (i, 0), pipeline_mode=pl.Buffered(1)),
        ],
        out_specs=pl.BlockSpec((tm, d), lambda i, j: (i, 0)),
        out_shape=jax.ShapeDtypeStruct((n, d), F32),
        compiler_params=_params(("parallel", "arbitrary")),
        name="ffn_dense",
    )(hn, wg, wu, wd, x1)


def _gather_norm_kernel(rt_ref, nv_ref, x_hbm, g_ref, o_ref, buf_ref, sem, *, tg):
    t = pl.program_id(0)

    def row_copy(r, tok):
        return pltpu.make_async_copy(x_hbm.at[pl.ds(tok, 1), :], buf_ref.at[pl.ds(r, 1), :], sem)

    @pl.when(t < nv_ref[0])
    def _():
        def issue(r, carry):
            row_copy(r, rt_ref[t * tg + r]).start()
            return carry

        lax.fori_loop(0, tg, issue, 0, unroll=8)
        pltpu.make_async_copy(buf_ref, buf_ref, sem).wait()
        o_ref[...] = _rms(buf_ref[...], g_ref[...], RMS_EPS).astype(BF16)

    @pl.when(t >= nv_ref[0])
    def _():
        o_ref[...] = jnp.zeros(o_ref.shape, o_ref.dtype)


def _gather_norm_call(row_token, n_valid, x1, g, *, tg):
    r = row_token.shape[0]
    d = x1.shape[1]
    grid_spec = pltpu.PrefetchScalarGridSpec(
        num_scalar_prefetch=2,
        grid=(r // tg,),
        in_specs=[pl.BlockSpec(memory_space=pl.ANY), pl.BlockSpec((1, d), lambda t, rt, nv: (0, 0))],
        out_specs=pl.BlockSpec((tg, d), lambda t, rt, nv: (t, 0)),
        scratch_shapes=[pltpu.VMEM((tg, d), F32), pltpu.SemaphoreType.DMA(())],
    )
    return pl.pallas_call(
        functools.partial(_gather_norm_kernel, tg=tg),
        grid_spec=grid_spec,
        out_shape=jax.ShapeDtypeStruct((r, d), BF16),
        compiler_params=_params(("arbitrary",)),
        name="moe_gather",
    )(row_token, n_valid, x1, g)


def _moe_ffn_kernel(te_ref, nv_ref, h_ref, wg_ref, wu_ref, wd_ref, o_ref):
    del te_ref
    t = pl.program_id(0)
    j = pl.program_id(1)

    @pl.when(j == 0)
    def _():
        o_ref[...] = jnp.zeros(o_ref.shape, F32)

    @pl.when(t < nv_ref[0])
    def _():
        o_ref[...] += _swiglu_acc(h_ref[...], wg_ref[...], wu_ref[...], wd_ref[...])


def _moe_ffn_call(tile_expert, n_valid, xs, wg, wu, wd, *, tm, tf):
    r, d = xs.shape
    f = wg.shape[2]
    nj = f // tf

    def tile(t, nv):
        return jnp.minimum(t, nv[0] - 1)

    def fblk(t, j, nv):
        return jnp.where(t < nv[0], j, nj - 1)

    grid_spec = pltpu.PrefetchScalarGridSpec(
        num_scalar_prefetch=2,
        grid=(r // tm, nj),
        in_specs=[
            pl.BlockSpec((tm, d), lambda t, j, te, nv: (tile(t, nv), 0)),
            pl.BlockSpec((None, d, tf), lambda t, j, te, nv: (te[t], 0, fblk(t, j, nv))),
            pl.BlockSpec((None, d, tf), lambda t, j, te, nv: (te[t], 0, fblk(t, j, nv))),
            pl.BlockSpec((None, tf, d), lambda t, j, te, nv: (te[t], fblk(t, j, nv), 0)),
        ],
        out_specs=pl.BlockSpec((tm, d), lambda t, j, te, nv: (t, 0)),
    )
    return pl.pallas_call(
        _moe_ffn_kernel,
        grid_spec=grid_spec,
        out_shape=jax.ShapeDtypeStruct((r, d), F32),
        compiler_params=_params(("arbitrary", "arbitrary")),
        name="moe_ffn",
    )(tile_expert, n_valid, xs, wg, wu, wd)


def _combine_norm_kernel(pos_ref, x_ref, rg_ref, ys_hbm, g_ref, *rest, tc, final_norm, n_prompt_tiles):
    if final_norm:
        op_ref, os_ref, buf_ref, sem = rest
    else:
        o_ref, buf_ref, sem = rest
    i = pl.program_id(0)

    def row_copy(k, r, src):
        return pltpu.make_async_copy(ys_hbm.at[pl.ds(src, 1), :], buf_ref.at[k, pl.ds(r, 1), :], sem)

    def issue(r, carry):
        for k in range(TOP_K):
            row_copy(k, r, pos_ref[(i * tc + r) * TOP_K + k]).start()
        return carry

    lax.fori_loop(0, tc, issue, 0, unroll=4)
    pltpu.make_async_copy(buf_ref, buf_ref, sem).wait()
    rg = rg_ref[...]
    moe = rg[:, 0:1] * buf_ref[0]
    for k in range(1, TOP_K):
        moe = moe + rg[:, k:k + 1] * buf_ref[k]
    x2 = x_ref[...] + moe
    if final_norm:
        y = _rms(x2, g_ref[...], RMS_EPS)

        @pl.when(i < n_prompt_tiles)
        def _():
            op_ref[...] = y

        @pl.when(i >= n_prompt_tiles)
        def _():
            os_ref[...] = y
    else:
        o_ref[...] = x2


def _combine_norm_call(pos, x1, rg, ys, g, *, tc, final_norm, n_prompt):
    n, d = x1.shape
    npt = n_prompt // tc
    if final_norm:
        out_specs = [pl.BlockSpec((tc, d), lambda i, ps: (jnp.minimum(i, npt - 1), 0)),
                     pl.BlockSpec((tc, d), lambda i, ps: (jnp.maximum(i - npt, 0), 0))]
        out_shape = [jax.ShapeDtypeStruct((n_prompt, d), F32), jax.ShapeDtypeStruct((n - n_prompt, d), F32)]
    else:
        out_specs = pl.BlockSpec((tc, d), lambda i, ps: (i, 0))
        out_shape = jax.ShapeDtypeStruct((n, d), F32)
    grid_spec = pltpu.PrefetchScalarGridSpec(
        num_scalar_prefetch=1,
        grid=(n // tc,),
        in_specs=[
            pl.BlockSpec((tc, d), lambda i, ps: (i, 0)),
            pl.BlockSpec((tc, LANES), lambda i, ps: (i, 0)),
            pl.BlockSpec(memory_space=pl.ANY),
            pl.BlockSpec((1, d), lambda i, ps: (0, 0)),
        ],
        out_specs=out_specs,
        scratch_shapes=[pltpu.VMEM((TOP_K, tc, d), F32), pltpu.SemaphoreType.DMA(())],
    )
    return pl.pallas_call(
        functools.partial(_combine_norm_kernel, tc=tc, final_norm=final_norm, n_prompt_tiles=npt),
        grid_spec=grid_spec,
        out_shape=out_shape,
        compiler_params=_params(("arbitrary",)),
        name="moe_combine",
    )(pos, x1, rg, ys, g)


def _norm_kernel(x_ref, g_ref, o_ref):
    o_ref[...] = _rms(x_ref[...], g_ref[...], RMS_EPS)


def _norm_call(x, g, *, tm):
    n, d = x.shape
    return pl.pallas_call(
        _norm_kernel,
        grid=(n // tm,),
        in_specs=[pl.BlockSpec((tm, d), lambda i: (i, 0)), pl.BlockSpec((1, d), lambda i: (0, 0))],
        out_specs=pl.BlockSpec((tm, d), lambda i: (i, 0)),
        out_shape=jax.ShapeDtypeStruct((n, d), F32),
        compiler_params=_params(("parallel",)),
        name="final_norm",
    )(x, g)


def _route_plan(re, n_experts, tm):
    n = re.shape[0]
    e_flat = re.reshape(-1)
    oh = (e_flat[:, None] == jnp.arange(n_experts, dtype=jnp.int32)[None, :]).astype(jnp.int32)
    csum = jnp.cumsum(oh, axis=0)
    rank = jnp.sum((csum - oh) * oh, axis=1)
    counts = csum[-1]
    padded = ((counts + tm - 1) // tm) * tm
    ends = jnp.cumsum(padded)
    offs = ends - padded
    pos = jnp.sum(oh * offs[None, :], axis=1) + rank
    n_rows = ((n * TOP_K + tm - 1) // tm) * tm + n_experts * tm
    row_token = jnp.zeros((n_rows,), jnp.int32).at[pos].set(jnp.arange(n * TOP_K, dtype=jnp.int32) // TOP_K)
    n_tiles = n_rows // tm
    n_valid = (ends[-1] // tm).astype(jnp.int32)
    tile_start = jnp.arange(n_tiles, dtype=jnp.int32) * tm
    tile_e = jnp.sum((tile_start[:, None] >= ends[None, :]).astype(jnp.int32), axis=1)
    last_e = jnp.sum((tile_start[n_valid - 1] >= ends).astype(jnp.int32))
    tile_e = jnp.where(jnp.arange(n_tiles) < n_valid, tile_e, last_e).astype(jnp.int32)
    return pos.astype(jnp.int32), row_token, tile_e, n_valid.reshape(1)


def _tile(pref, *dims):
    t = pref
    for dim in dims:
        t = math.gcd(t, dim)
    return t


def _rope_tables(seq, n_past, t_new, bs, dk, rot_dim):
    half = rot_dim // 2
    inv = ROPE_THETA ** (-(jnp.arange(half, dtype=F32) * 2.0) / rot_dim)
    pos = jnp.concatenate([jnp.arange(seq), jnp.tile(n_past + jnp.arange(t_new), bs)])
    ang = pos.astype(F32)[:, None] * inv[None, :]
    cos, sin = jnp.cos(ang), jnp.sin(ang)
    n = pos.shape[0]
    ones = jnp.ones((n, dk - rot_dim), F32)
    zeros_h = jnp.zeros((n, half), F32)
    zeros_r = jnp.zeros((n, dk - rot_dim), F32)
    cos_t = jnp.concatenate([cos, cos, ones], axis=1)
    sa_t = jnp.concatenate([zeros_h, sin, zeros_r], axis=1)
    sb_t = jnp.concatenate([-sin, zeros_h, zeros_r], axis=1)
    return cos_t, sa_t, sb_t


def _page_mats(heads, page, n_pages):
    w = heads * page
    dst = jnp.arange(w)
    d_head, d_slot = dst // page, dst % page
    same = d_head[:, None] == d_head[None, :]
    mcs = (same & (d_slot[:, None] <= d_slot[None, :])).astype(BF16)
    mbc = ((d_head[:, None] == d_head[None, :]) & (d_slot[:, None] == page - 1)).astype(BF16)
    msel = ((d_slot[:, None] == page - 1) & (d_head[:, None] == jnp.arange(LANES)[None, :])).astype(BF16)
    pg = jnp.arange(n_pages)
    lst = (pg[None, :] < pg[:, None]).astype(BF16)
    return mcs, mbc, msel, lst


def kernel(x_prompt, x_sample, cache_k_diff, cache_v_diff, cache_k_fox, cache_v_fox, cache_logf_fox, state_rglru_h, state_rglru_conv, page_table, norm_mix_g, w_in, b_fgate, lambda_q1, lambda_k1, lambda_q2, lambda_k2, subln_g, conv_w, conv_b, w_gate_a, b_gate_a, w_gate_x, b_gate_x, lru_lambda, w_out, norm_ffn_g, w_dense_gate, w_dense_up, w_dense_down, w_router, w_moe_gate, w_moe_up, w_moe_down, norm_final_g):
    bp, seq, d = x_prompt.shape
    bs, t_new, _ = x_sample.shape
    depth = w_in.shape[0]
    n_pool, page, h_a = cache_k_diff.shape[1], cache_k_diff.shape[2], cache_k_diff.shape[3]
    dk_a = cache_k_diff.shape[5]
    dv_a = cache_v_diff.shape[4]
    h_b, dh_b = cache_k_fox.shape[3], cache_k_fox.shape[4]
    w_c = state_rglru_h.shape[2]
    cw_len = conv_w.shape[1]
    n_pages = page_table.shape[1]
    n_past = n_pages * page
    rot_dim = dk_a // 4
    n_experts = w_router.shape[2]
    n_p, n_s = bp * seq, bs * t_new
    n = n_p + n_s
    assert t_new == SUBLANES and page == LANES and dv_a == 2 * dk_a and dk_a == LANES and dh_b == LANES

    w_qa = h_a * 2 * dk_a
    w_va = h_a * dv_a
    w_b = h_b * dh_b
    col_qa, col_ka, col_va = 0, w_qa, 2 * w_qa
    col_qb = col_va + w_va
    col_kb, col_vb = col_qb + w_b, col_qb + 2 * w_b
    col_g = col_vb + w_b
    col_x = col_g + w_c
    n_main = col_x + w_c
    src_fl = 2 * w_qa + w_va + 3 * w_b

    tm = _tile(TM_TOKENS, seq, n_s)
    tn = _tile(TN_PROJ, w_qa, src_fl, n_main - src_fl)
    tq = _tile(TQ_ATTN, seq)
    tt = _tile(TT_LRU, seq)
    tm_mix = _tile(TM_MIX, tm)
    tf = _tile(TF_FFN, w_dense_gate.shape[2])
    tm_moe = _tile(TM_MOE, n)
    tc = _tile(TC_COMBINE, n)

    rope = _rope_tables(seq, n_past, t_new, bs, dk_a, rot_dim)
    tri = jnp.tril(jnp.ones((seq, seq), BF16))
    page_mats = _page_mats(h_b, page, n_pages)

    kc_d = cache_k_diff.reshape(depth, n_pool, page * h_a * 2, dk_a)
    nc_a = dv_a // LANES
    vc_d = jnp.swapaxes(cache_v_diff.reshape(depth, n_pool, page, h_a, nc_a, LANES), 3, 4).reshape(
        depth, n_pool, page * nc_a * h_a, LANES)
    kc_f = cache_k_fox.reshape(depth, n_pool, page * h_b, dh_b)
    vc_f = cache_v_fox.reshape(depth, n_pool, page * h_b, dh_b)
    lf_pages = jnp.swapaxes(cache_logf_fox, 2, 3).reshape(depth, n_pool, h_b * page)
    n_group = _tile(PAGES_PER_STEP, n_pages)

    x = jnp.concatenate([x_prompt.reshape(n_p, d), x_sample.reshape(n_s, d)], axis=0)
    row2 = lambda v: v.reshape(1, -1).astype(F32)
    st = {k: [] for k in ('kd', 'vd', 'kf', 'vf', 'lf', 'h_p', 'h_s', 'cv')}
    y_p = y_s = None
    cache_a = cache_b = None

    for l in range(depth):
        lam_init = 0.8 - 0.6 * math.exp(-0.3 * l)
        w_l = w_in[l]
        w_b2 = w_l[:, src_fl + h_b:].astype(BF16)
        b_fl = jnp.zeros((1, LANES), F32).at[0, :h_b].set(b_fgate[l])
        proj, logf = _proj_call(
            x, row2(norm_mix_g[l]), w_in, l, src_fl, w_b2, b_fl, rope, tm=tm, tn=tn, n_rope_cols=2 * w_qa,
            n_prompt_tiles=n_p // tm, rope_prompt_tiles=seq // tm, rot_half=rot_dim // 2)

        lams = (row2(lambda_q1[l]), row2(lambda_k1[l]), row2(lambda_q2[l]), row2(lambda_k2[l]))
        sg = row2(subln_g[l])

        oa_p, *cache_a = _diff_prompt_call(proj, lams, sg, cache_a, l, depth, batch=bp, seq=seq, heads=h_a,
                                           dk=dk_a, dv=dv_a, tq=tq, k_col=col_ka, v_col=col_va,
                                           lam_init=lam_init)
        c_col = _cumsum_call(logf, tri, bp, seq)
        c_row = jnp.swapaxes(c_col[:, :h_b].reshape(bp, seq, h_b), 1, 2).reshape(bp * h_b, seq // tq, tq)
        ob_p, *cache_b = _fox_prompt_call(proj, c_col, c_row, cache_b, l, depth, batch=bp, seq=seq, heads=h_b,
                                          dh=dh_b, tq=tq, q_col=col_qb, k_col=col_kb, v_col=col_vb)
        lru_w = (conv_w[l], row2(conv_b[l]), w_gate_a[l].astype(BF16), row2(b_gate_a[l]),
                 w_gate_x[l].astype(BF16), row2(b_gate_x[l]), row2(lru_lambda[l]))
        oc_p, ht_p = _lru_prompt_call(proj, lru_w, batch=bp, seq=seq, wc=w_c, tt=tt, g_col=col_g, x_col=col_x)

        oa_s = _diff_sample_call(page_table, proj, kc_d, vc_d, lams, sg, l, n_prompt=n_p, t_new=t_new,
                                 heads=h_a, dk=dk_a, dv=dv_a, page=page, k_col=col_ka, v_col=col_va,
                                 lam_init=lam_init, n_group=n_group)
        cpast, ctot = _logf_pages_call(page_table, lf_pages[l], page_mats, heads=h_b, page=page)
        ob_s = _fox_sample_call(page_table, proj, logf, kc_f, vc_f, cpast, ctot, l, n_prompt=n_p,
                                t_new=t_new, heads=h_b, dh=dh_b, page=page, q_col=col_qb, k_col=col_kb,
                                v_col=col_vb, n_group=n_group)
        prev8 = jnp.pad(state_rglru_conv[l], ((0, 0), (SUBLANES - (cw_len - 1), 0), (0, 0))).reshape(n_s, w_c)
        h0rep = jnp.repeat(state_rglru_h[l], t_new, axis=0)
        oc_s, hs_s = _lru_sample_call(proj, prev8, h0rep, lru_w, n_prompt=n_p, n_sample=n_s, wc=w_c,
                                      g_col=col_g, x_col=col_x)

        br_p, br_s = (oa_p, ob_p, oc_p), (oa_s, ob_s, oc_s)
        g_ffn = row2(norm_ffn_g[l])
        m = l // 2
        if l % 2 == 0:
            x1, hn = _mix_call(br_p, br_s, w_out[l].astype(BF16), x, g_ffn, None, tm=tm_mix)
            x = _ffn_call(hn, w_dense_gate[m].astype(BF16), w_dense_up[m].astype(BF16),
                          w_dense_down[m].astype(BF16), x1, tm=tm, tf=tf)
            if l == depth - 1:
                y_all = _norm_call(x, row2(norm_final_g), tm=tm_mix)
                y_p, y_s = y_all[:n_p], y_all[n_p:]
        else:
            x1, rg, re = _mix_call(br_p, br_s, w_out[l].astype(BF16), x, g_ffn, w_router[m], tm=tm_mix)
            pos, row_token, tile_e, n_valid = _route_plan(re[:, :TOP_K], n_experts, tm_moe)
            xs = _gather_norm_call(row_token, n_valid, x1, g_ffn, tg=tm_moe)
            ys = _moe_ffn_call(tile_e, n_valid, xs, w_moe_gate[m].astype(BF16), w_moe_up[m].astype(BF16),
                               w_moe_down[m].astype(BF16), tm=tm_moe,
                               tf=_tile(TF_MOE, w_moe_gate.shape[3]))
            final = l == depth - 1
            x = _combine_norm_call(pos, x1, rg, ys, row2(norm_final_g), tc=tc, final_norm=final, n_prompt=n_p)
            if final:
                y_p, y_s = x

        st['kd'].append(proj[n_p:, col_ka:col_ka + w_qa])
        st['vd'].append(proj[n_p:, col_va:col_va + w_va])
        st['kf'].append(proj[n_p:, col_kb:col_kb + w_b])
        st['vf'].append(proj[n_p:, col_vb:col_vb + w_b])
        st['lf'].append(logf[:, :h_b])
        st['h_p'].append(ht_p.reshape(bp, w_c))
        st['h_s'].append(hs_s.reshape(bs, t_new, w_c)[:, t_new - 1])
        st['cv'].append(proj[:, col_x:col_x + w_c])

    def both(name, shape_tail):
        a = jnp.stack(st[name])
        return (a[:, :n_p].reshape((depth, bp, seq) + shape_tail),
                a[:, n_p:].reshape((depth, bs, t_new) + shape_tail))

    sample = lambda name, tail: jnp.stack(st[name]).reshape((depth, bs, t_new) + tail)
    kd_s, vd_s = sample('kd', (h_a, 2, dk_a)), sample('vd', (h_a, dv_a))
    kf_s, vf_s = sample('kf', (h_b, dh_b)), sample('vf', (h_b, dh_b))
    kd_p = cache_a[0].reshape(depth, bp, seq, h_a, 2, dk_a)
    vd_p = jnp.swapaxes(cache_a[1].reshape(depth, bp, seq, nc_a, h_a, LANES), 3, 4).reshape(
        depth, bp, seq, h_a, dv_a)
    kf_p = cache_b[0].reshape(depth, bp, seq, h_b, dh_b)
    vf_p = cache_b[1].reshape(depth, bp, seq, h_b, dh_b)
    lf_p, lf_s = both('lf', (h_b,))
    cv_p, cv_s = both('cv', (w_c,))
    keep = cw_len - 1
    return (y_p.reshape(bp, seq, d), y_s.reshape(bs, t_new, d),
            kd_p, vd_p, kf_p, vf_p, lf_p, jnp.stack(st['h_p']), cv_p[:, :, seq - keep:],
            kd_s, vd_s, kf_s, vf_s, lf_s, jnp.stack(st['h_s']), cv_s[:, :, t_new - keep:])
```

```python
import functools
import math

import jax
import jax.numpy as jnp
from jax import lax
from jax.experimental import pallas as pl
from jax.experimental.pallas import tpu as pltpu

F32 = jnp.float32
BF16 = jnp.bfloat16

LANES = 128
SUBLANES = 8
VMEM_LIMIT = 56 * 1024 * 1024

TM_TOKENS = 1024
TN_PROJ = 512
TQ_ATTN = 512
TT_LRU = 256
TM_MIX = 512
TF_FFN = 512
TF_MOE = 1408
TM_MOE = 512
TC_COMBINE = 256
PAGES_PER_STEP = 8

RMS_EPS = 1e-6
SUBLN_EPS = 1e-5
LRU_C = 8.0
ROPE_THETA = 500000.0
TOP_K = 2
NEG_INF = float("-inf")
LOG2E = 1.4426950408889634


def _params(sem, vmem=VMEM_LIMIT):
    return pltpu.CompilerParams(dimension_semantics=sem, vmem_limit_bytes=vmem)


def _split3(x):
    hi = x.astype(BF16)
    r = x - hi.astype(F32)
    mid = r.astype(BF16)
    lo = (r - mid.astype(F32)).astype(BF16)
    return hi, mid, lo


def _dot_sel(x, m):
    hi, mid, lo = _split3(x)
    d = lambda a: jnp.dot(a, m, preferred_element_type=F32)
    return d(hi) + d(mid) + d(lo)


def _sel_dot(m, x):
    hi, mid, lo = _split3(x)
    d = lambda a: jnp.dot(m, a, preferred_element_type=F32)
    return d(hi) + d(mid) + d(lo)


def _dot_nt(a, b):
    return lax.dot_general(a, b, (((1,), (1,)), ((), ())), preferred_element_type=F32)


def _sigmoid(x):
    return 1.0 / (1.0 + jnp.exp(-x))


def _softplus(x):
    return jnp.maximum(x, 0.0) + jnp.log1p(jnp.exp(-jnp.abs(x)))


def _rms(x, g, eps):
    ms = jnp.mean(x * x, axis=-1, keepdims=True)
    return x * lax.rsqrt(ms + eps) * g


def _proj_kernel(x_ref, g_ref, wa_ref, wb_ref, wfl_ref, bf_ref, cos_ref, sa_ref, sb_ref,
                 proj_ref, logf_ref, hn_ref, *, n_rope_tiles, n_a_tiles, tn, rot_half):
    j = pl.program_id(1)

    @pl.when(j == 0)
    def _():
        hn = _rms(x_ref[...], g_ref[...], RMS_EPS).astype(BF16)
        hn_ref[...] = hn
        z = jnp.dot(hn, wfl_ref[...].astype(BF16), preferred_element_type=F32) + bf_ref[...]
        logf_ref[...] = jnp.minimum(z, 0.0) - jnp.log1p(jnp.exp(-jnp.abs(z)))

    @pl.when(j < n_rope_tiles)
    def _():
        y = jnp.dot(hn_ref[...], wa_ref[...].astype(BF16), preferred_element_type=F32)
        c, sa, sb = cos_ref[...], sa_ref[...], sb_ref[...]
        for k in range(tn // LANES):
            yc = y[:, k * LANES:(k + 1) * LANES]
            proj_ref[:, k * LANES:(k + 1) * LANES] = (
                yc * c + pltpu.roll(yc, rot_half, 1) * sa + pltpu.roll(yc, LANES - rot_half, 1) * sb)

    @pl.when((j >= n_rope_tiles) & (j < n_a_tiles))
    def _():
        proj_ref[...] = jnp.dot(hn_ref[...], wa_ref[...].astype(BF16), preferred_element_type=F32)

    @pl.when(j >= n_a_tiles)
    def _():
        proj_ref[...] = jnp.dot(hn_ref[...], wb_ref[...], preferred_element_type=F32)


def _proj_call(x, g, w_in, layer, fl_col, wb, bfl, rope, *, tm, tn, n_rope_cols, n_prompt_tiles,
               rope_prompt_tiles, rot_half):
    n, d = x.shape
    na, nb = fl_col // tn, wb.shape[1] // tn
    n_main = (na + nb) * tn
    cos_t, sa_t, sb_t = rope
    rope_map = lambda i, j: (jnp.where(i < n_prompt_tiles, i % rope_prompt_tiles,
                                       rope_prompt_tiles + i - n_prompt_tiles), 0)
    kern = functools.partial(_proj_kernel, n_rope_tiles=n_rope_cols // tn, n_a_tiles=na, tn=tn,
                             rot_half=rot_half)
    return pl.pallas_call(
        kern,
        grid=(n // tm, n_main // tn),
        in_specs=[
            pl.BlockSpec((tm, d), lambda i, j: (i, 0)),
            pl.BlockSpec((1, d), lambda i, j: (0, 0)),
            pl.BlockSpec((None, d, tn), lambda i, j: (layer, 0, jnp.minimum(j, na - 1))),
            pl.BlockSpec((d, tn), lambda i, j: (0, jnp.maximum(j - na, 0))),
            pl.BlockSpec((None, d, LANES), lambda i, j: (layer, 0, fl_col // LANES)),
            pl.BlockSpec((1, LANES), lambda i, j: (0, 0)),
            pl.BlockSpec((tm, LANES), rope_map),
            pl.BlockSpec((tm, LANES), rope_map),
            pl.BlockSpec((tm, LANES), rope_map),
        ],
        out_specs=[
            pl.BlockSpec((tm, tn), lambda i, j: (i, j)),
            pl.BlockSpec((tm, LANES), lambda i, j: (i, 0)),
        ],
        out_shape=[jax.ShapeDtypeStruct((n, n_main), F32), jax.ShapeDtypeStruct((n, LANES), F32)],
        scratch_shapes=[pltpu.VMEM((tm, d), BF16)],
        compiler_params=_params(("parallel", "arbitrary")),
        name="proj",
    )(x, g, w_in, wb, w_in, bfl, cos_t, sa_t, sb_t)


def _cumsum_kernel(x_ref, tri_ref, o_ref):
    o_ref[...] = _sel_dot(tri_ref[...], x_ref[...])


def _cumsum_call(logf_p, tri, batch, seq):
    return pl.pallas_call(
        _cumsum_kernel,
        grid=(batch,),
        in_specs=[pl.BlockSpec((seq, LANES), lambda b: (b, 0)),
                  pl.BlockSpec((seq, seq), lambda b: (0, 0))],
        out_specs=pl.BlockSpec((seq, LANES), lambda b: (b, 0)),
        out_shape=jax.ShapeDtypeStruct((batch * seq, LANES), F32),
        compiler_params=_params(("parallel",)),
        name="cumsum_prompt",
    )(logf_p, tri)


def _lane_chunks(s):
    return [s[:, c * LANES:(c + 1) * LANES] for c in range(s.shape[1] // LANES)]


def _rowmax(s):
    return jnp.max(functools.reduce(jnp.maximum, _lane_chunks(s)), axis=-1, keepdims=True)


def _rowsum(s):
    return jnp.sum(functools.reduce(jnp.add, _lane_chunks(s)), axis=-1, keepdims=True)


def _softmax_step(s, v, m_ref, l_ref, acc_ref, idx):
    m_old = m_ref[idx]
    m_new = jnp.maximum(m_old, _rowmax(s))
    alpha = jnp.exp2(m_old - m_new)
    p_chunks = [jnp.exp2(c - m_new) for c in _lane_chunks(s)]
    l_ref[idx] = alpha * l_ref[idx] + functools.reduce(jnp.add, p_chunks)
    p = jnp.concatenate(p_chunks, axis=1).astype(BF16)
    pv = jnp.dot(p, v, preferred_element_type=F32)
    acc_ref[idx] = jnp.concatenate([alpha] * (pv.shape[1] // LANES), axis=1) * acc_ref[idx] + pv
    m_ref[idx] = m_new


def _softmax_denominator(l_ref, idx):
    return jnp.sum(l_ref[idx], axis=-1, keepdims=True)


def _softmax_step_stacked(s, v_heads, m_ref, l_ref, acc_ref, rows_per_head):
    m_old = m_ref[...]
    m_new = jnp.maximum(m_old, _rowmax(s))
    p = jnp.exp(s - m_new)
    alpha = jnp.exp(m_old - m_new)
    l_ref[...] = alpha * l_ref[...] + _rowsum(p)
    pv = [jnp.dot(p[h * rows_per_head:(h + 1) * rows_per_head, :].astype(BF16), v,
                  preferred_element_type=F32) for h, v in enumerate(v_heads)]
    acc_ref[...] = alpha * acc_ref[...] + jnp.concatenate(pv, axis=0)
    m_ref[...] = m_new


def _diff_lambda(lq1, lk1, lq2, lk2, lam_init):
    return (jnp.exp(jnp.sum(lq1[...] * lk1[...], axis=-1, keepdims=True))
            - jnp.exp(jnp.sum(lq2[...] * lk2[...], axis=-1, keepdims=True)) + lam_init)


def _diff_finish(acc_ref, l_ref, base, lam, sg, lam_init):
    o = (acc_ref[base] / _softmax_denominator(l_ref, base)
         - lam * (acc_ref[base + 1] / _softmax_denominator(l_ref, base + 1)))
    return _rms(o, sg, SUBLN_EPS) * (1.0 - lam_init)


def _diff_prompt_kernel(q_ref, k_ref, v_ref, lq1, lk1, lq2, lk2, sg_ref, *rest, tq, dk, heads, n_alias,
                        lam_init, scale):
    o_ref, kc_ref, vc_ref, kb_ref, vb_ref, m_ref, l_ref, acc_ref = rest[n_alias:]
    h = pl.program_id(1)
    qi = pl.program_id(2)
    seq = k_ref.shape[0]

    @pl.when(qi == 0)
    def _():
        k = k_ref[...]
        v = v_ref[...]
        for m in range(2):
            kb_ref[m] = k[:, m * dk:(m + 1) * dk].T.astype(BF16)
            kc_ref[pl.ds(2 * h + m, seq, stride=2 * heads), :] = k[:, m * dk:(m + 1) * dk]
        vb_ref[...] = v.astype(BF16)
        nc = v.shape[1] // LANES
        for c in range(nc):
            vc_ref[pl.ds(c * heads + h, seq, stride=nc * heads), :] = v[:, c * LANES:(c + 1) * LANES]

    m_ref[...] = jnp.full(m_ref.shape, NEG_INF, F32)
    l_ref[...] = jnp.zeros(l_ref.shape, F32)
    acc_ref[...] = jnp.zeros(acc_ref.shape, F32)
    q = (q_ref[...] * (scale * LOG2E)).astype(BF16)

    def block(kk, causal):
        k0 = pl.multiple_of(kk * tq, tq)
        vblk = vb_ref[pl.ds(k0, tq), :]
        for m in range(2):
            s = jnp.dot(q[:, m * dk:(m + 1) * dk], kb_ref[m, :, pl.ds(k0, tq)],
                        preferred_element_type=F32)
            if causal:
                row = lax.broadcasted_iota(jnp.int32, s.shape, 0)
                col = lax.broadcasted_iota(jnp.int32, s.shape, 1)
                s = jnp.where(col <= row, s, NEG_INF)
            _softmax_step(s, vblk, m_ref, l_ref, acc_ref, m)

    def body(kk, carry):
        block(kk, False)
        return carry

    lax.fori_loop(0, qi, body, 0)
    block(qi, True)

    lam = _diff_lambda(lq1, lk1, lq2, lk2, lam_init)
    o_ref[...] = _diff_finish(acc_ref, l_ref, 0, lam, sg_ref[...], lam_init).astype(o_ref.dtype)


def _cache_out(prev, depth, rows_total, rows_block, layer):
    out_spec = pl.BlockSpec((None, rows_block, LANES), lambda *idx: (layer, idx[0], 0))
    out_shape = jax.ShapeDtypeStruct((depth, rows_total, LANES), F32)
    return out_spec, out_shape, ([] if prev is None else [prev])


def _diff_prompt_call(proj, lams, sg, prev, layer, depth, *, batch, seq, heads, dk, dv, tq, k_col, v_col,
                      lam_init):
    nq = seq // tq
    n_rows = batch * seq
    lam_spec = pl.BlockSpec((1, dk), lambda b, h, q: (0, 0))
    rows_k, rows_v = seq * 2 * heads, seq * heads * dv // LANES
    ks, kshape, kprev = _cache_out(None if prev is None else prev[0], depth, batch * rows_k, rows_k, layer)
    vs, vshape, vprev = _cache_out(None if prev is None else prev[1], depth, batch * rows_v, rows_v, layer)
    aliased = kprev + vprev
    n_in = 8
    kern = functools.partial(_diff_prompt_kernel, tq=tq, dk=dk, heads=heads, n_alias=len(aliased),
                             lam_init=lam_init, scale=dk ** -0.5)
    return pl.pallas_call(
        kern,
        grid=(batch, heads, nq),
        in_specs=[
            pl.BlockSpec((tq, 2 * dk), lambda b, h, q: (b * nq + q, h)),
            pl.BlockSpec((seq, 2 * dk), lambda b, h, q: (b, k_col // (2 * dk) + h)),
            pl.BlockSpec((seq, dv), lambda b, h, q: (b, v_col // dv + h)),
            lam_spec, lam_spec, lam_spec, lam_spec,
            pl.BlockSpec((1, dv), lambda b, h, q: (0, 0)),
        ] + [pl.BlockSpec(memory_space=pl.ANY)] * len(aliased),
        out_specs=[pl.BlockSpec((tq, dv), lambda b, h, q: (b * nq + q, h)), ks, vs],
        out_shape=[jax.ShapeDtypeStruct((n_rows, heads * dv), BF16), kshape, vshape],
        input_output_aliases={n_in + i: 1 + i for i in range(len(aliased))},
        scratch_shapes=[
            pltpu.VMEM((2, dk, seq), BF16), pltpu.VMEM((seq, dv), BF16),
            pltpu.VMEM((2, tq, LANES), F32), pltpu.VMEM((2, tq, LANES), F32), pltpu.VMEM((2, tq, dv), F32),
        ],
        compiler_params=_params(("parallel", "arbitrary", "arbitrary")),
        name="diff_prompt",
    )(proj, proj, proj, *lams, sg, *aliased)


def _fox_prompt_kernel(q_ref, k_ref, v_ref, cc_ref, cr_ref, *rest, tq, heads, n_alias, scale):
    o_ref, kc_ref, vc_ref, kb_ref, vb_ref, m_ref, l_ref, acc_ref = rest[n_alias:]
    h = pl.program_id(1)
    qi = pl.program_id(2)
    seq = k_ref.shape[0]

    @pl.when(qi == 0)
    def _():
        k = k_ref[...]
        v = v_ref[...]
        kb_ref[...] = k.T.astype(BF16)
        vb_ref[...] = v.astype(BF16)
        kc_ref[pl.ds(h, seq, stride=heads), :] = k
        vc_ref[pl.ds(h, seq, stride=heads), :] = v

    m_ref[...] = jnp.full(m_ref.shape, NEG_INF, F32)
    l_ref[...] = jnp.zeros(l_ref.shape, F32)
    acc_ref[...] = jnp.zeros(acc_ref.shape, F32)
    q = (q_ref[...] * (scale * LOG2E)).astype(BF16)
    cc = cc_ref[...]
    lane = lax.broadcasted_iota(jnp.int32, cc.shape, 1)
    cq = jnp.sum(jnp.where(lane == h, cc, 0.0), axis=-1, keepdims=True) * LOG2E

    def block(kk, causal):
        k0 = pl.multiple_of(kk * tq, tq)
        ck = cr_ref[0, pl.ds(kk, 1), :] * LOG2E
        s = jnp.dot(q, kb_ref[:, pl.ds(k0, tq)], preferred_element_type=F32) + (cq - ck)
        if causal:
            row = lax.broadcasted_iota(jnp.int32, s.shape, 0)
            col = lax.broadcasted_iota(jnp.int32, s.shape, 1)
            s = jnp.where(col <= row, s, NEG_INF)
        _softmax_step(s, vb_ref[pl.ds(k0, tq), :], m_ref, l_ref, acc_ref, 0)

    def body(kk, carry):
        block(kk, False)
        return carry

    lax.fori_loop(0, qi, body, 0)
    block(qi, True)
    o_ref[...] = (acc_ref[0] / _softmax_denominator(l_ref, 0)).astype(o_ref.dtype)


def _fox_prompt_call(proj, c_col, c_row, prev, layer, depth, *, batch, seq, heads, dh, tq, q_col, k_col, v_col):
    nq = seq // tq
    rows = seq * heads
    ks, kshape, kprev = _cache_out(None if prev is None else prev[0], depth, batch * rows, rows, layer)
    vs, vshape, vprev = _cache_out(None if prev is None else prev[1], depth, batch * rows, rows, layer)
    aliased = kprev + vprev
    n_in = 5
    kern = functools.partial(_fox_prompt_kernel, tq=tq, heads=heads, n_alias=len(aliased), scale=dh ** -0.5)
    return pl.pallas_call(
        kern,
        grid=(batch, heads, nq),
        in_specs=[
            pl.BlockSpec((tq, dh), lambda b, h, q: (b * nq + q, q_col // dh + h)),
            pl.BlockSpec((seq, dh), lambda b, h, q: (b, k_col // dh + h)),
            pl.BlockSpec((seq, dh), lambda b, h, q: (b, v_col // dh + h)),
            pl.BlockSpec((tq, LANES), lambda b, h, q: (b * nq + q, 0)),
            pl.BlockSpec((1, nq, tq), lambda b, h, q: (b * heads + h, 0, 0)),
        ] + [pl.BlockSpec(memory_space=pl.ANY)] * len(aliased),
        out_specs=[pl.BlockSpec((tq, dh), lambda b, h, q: (b * nq + q, h)), ks, vs],
        out_shape=[jax.ShapeDtypeStruct((batch * seq, heads * dh), BF16), kshape, vshape],
        input_output_aliases={n_in + i: 1 + i for i in range(len(aliased))},
        scratch_shapes=[
            pltpu.VMEM((dh, seq), BF16), pltpu.VMEM((seq, dh), BF16),
            pltpu.VMEM((1, tq, LANES), F32), pltpu.VMEM((1, tq, LANES), F32), pltpu.VMEM((1, tq, dh), F32),
        ],
        compiler_params=_params(("parallel", "arbitrary", "arbitrary")),
        name="fox_prompt",
    )(proj, proj, proj, c_col, c_row, *aliased)


def _pad_rows(x, rows):
    return jnp.concatenate([x, jnp.zeros((rows - x.shape[0], x.shape[1]), x.dtype)], axis=0)


def _causal_new(s, t_new):
    row = lax.broadcasted_iota(jnp.int32, s.shape, 0) % t_new
    col = lax.broadcasted_iota(jnp.int32, s.shape, 1)
    return jnp.where(col <= row, s, NEG_INF)


def _logf_pages_kernel(pt_ref, lf_ref, mcs_ref, mbc_ref, msel_ref, lst_ref, cp_ref, ct_ref, g_ref,
                       *, n_pages, heads, page):
    b = pl.program_id(0)
    for p in range(n_pages):
        g_ref[pl.ds(p, 1), :] = lf_ref[pl.ds(pt_ref[b, p], 1), :]
    x = g_ref[...]
    y = _dot_sel(x, mcs_ref[...])
    tot = _dot_sel(y, mbc_ref[...])
    c = y + _sel_dot(lst_ref[...], tot)
    for h in range(heads):
        cp_ref[0, h * n_pages:(h + 1) * n_pages, :] = c[:, h * page:(h + 1) * page]
    ct_ref[0] = _dot_sel(c, msel_ref[...])[n_pages - 1:n_pages, :]


def _logf_pages_call(page_table, lf2, mats, *, heads, page):
    bs, n_pages = page_table.shape
    n_pool, w = lf2.shape
    mcs, mbc, msel, lst = mats
    full = lambda a: pl.BlockSpec(a.shape, lambda b, pt: (0,) * a.ndim)
    kern = functools.partial(_logf_pages_kernel, n_pages=n_pages, heads=heads, page=page)
    grid_spec = pltpu.PrefetchScalarGridSpec(
        num_scalar_prefetch=1,
        grid=(bs,),
        in_specs=[full(lf2), full(mcs), full(mbc), full(msel), full(lst)],
        out_specs=[pl.BlockSpec((1, heads * n_pages, page), lambda b, pt: (b, 0, 0)),
                   pl.BlockSpec((1, 1, LANES), lambda b, pt: (b, 0, 0))],
        scratch_shapes=[pltpu.VMEM((n_pages, w), F32)],
    )
    return pl.pallas_call(
        kern,
        grid_spec=grid_spec,
        out_shape=[jax.ShapeDtypeStruct((bs, heads * n_pages, page), F32),
                   jax.ShapeDtypeStruct((bs, 1, LANES), F32)],
        compiler_params=_params(("arbitrary",)),
        name="logf_pages",
    )(page_table, lf2, mcs, mbc, msel, lst)


def _fox_sample_kernel(pt_ref, q_ref, kn_ref, vn_ref, lfn_ref, *refs, heads, dh, page, n_pages, n_group, scale):
    del pt_ref
    kc_refs, vc_refs = refs[:n_group], refs[n_group:2 * n_group]
    cp_ref, ct_ref, o_ref, m_ref, l_ref, acc_ref, cn_ref = refs[2 * n_group:]
    p = pl.program_id(1)
    last = pl.num_programs(1) - 1
    t_new = q_ref.shape[0]

    @pl.when(p == 0)
    def _():
        m_ref[...] = jnp.full(m_ref.shape, NEG_INF, F32)
        l_ref[...] = jnp.zeros(l_ref.shape, F32)
        acc_ref[...] = jnp.zeros(acc_ref.shape, F32)
        x = lfn_ref[...]
        row = lax.broadcasted_iota(jnp.int32, x.shape, 0)
        k = 1
        while k < x.shape[0]:
            x = x + jnp.where(row >= k, pltpu.roll(x, k, 0), 0.0)
            k *= 2
        cn_ref[...] = ct_ref[0] + x

    q = q_ref[...]
    cn = cn_ref[...]
    qb = [q[:, h * dh:(h + 1) * dh].astype(BF16) for h in range(heads)]
    cq = [cn[:, h:h + 1] for h in range(heads)]

    s = jnp.concatenate([
        jnp.concatenate([_dot_nt(qb[h], kc[pl.ds(h, page, stride=heads), :].astype(BF16)) for kc in kc_refs],
                        axis=1) for h in range(heads)], axis=0) * scale
    bias = jnp.concatenate([
        cq[h] - jnp.concatenate([cp_ref[0, pl.ds(h * n_pages + p * n_group + g, 1), :] for g in range(n_group)],
                                axis=1) for h in range(heads)], axis=0)
    v_heads = [jnp.concatenate([vc[pl.ds(h, page, stride=heads), :] for vc in vc_refs], axis=0).astype(BF16)
               for h in range(heads)]
    _softmax_step_stacked(s + bias, v_heads, m_ref, l_ref, acc_ref, t_new)

    @pl.when(p == last)
    def _():
        kn = kn_ref[...]
        vn = vn_ref[...]
        s_new = jnp.concatenate([
            _dot_nt(qb[h], _pad_rows(kn[:, h * dh:(h + 1) * dh], page).astype(BF16)) for h in range(heads)],
            axis=0) * scale
        row = lax.broadcasted_iota(jnp.int32, (t_new, page), 0)
        col = lax.broadcasted_iota(jnp.int32, (t_new, page), 1)
        bias_new = jnp.concatenate([
            cq[h] - jnp.sum(jnp.where(row == col, cq[h], 0.0), axis=0, keepdims=True) for h in range(heads)],
            axis=0)
        v_new = [_pad_rows(vn[:, h * dh:(h + 1) * dh], page).astype(BF16) for h in range(heads)]
        _softmax_step_stacked(_causal_new(s_new + bias_new, t_new), v_new, m_ref, l_ref, acc_ref, t_new)
        o = acc_ref[...] / l_ref[...]
        for h in range(heads):
            o_ref[:, h * dh:(h + 1) * dh] = o[h * t_new:(h + 1) * t_new, :]


def _fox_sample_call(page_table, proj, logf, kc, vc, cpast, ctot, layer, *, n_prompt, t_new, heads, dh,
                     page, q_col, k_col, v_col, n_group):
    bs, n_pages = page_table.shape
    w = heads * dh
    row0 = n_prompt // t_new
    kern = functools.partial(_fox_sample_kernel, heads=heads, dh=dh, page=page, n_pages=n_pages,
                             n_group=n_group, scale=dh ** -0.5)

    def page_spec(g):
        return pl.BlockSpec((None, None, page * heads, dh),
                            lambda b, p, pt: (layer, pt[b, p * n_group + g], 0, 0))

    grid_spec = pltpu.PrefetchScalarGridSpec(
        num_scalar_prefetch=1,
        grid=(bs, n_pages // n_group),
        in_specs=[
            pl.BlockSpec((t_new, w), lambda b, p, pt: (row0 + b, q_col // w)),
            pl.BlockSpec((t_new, w), lambda b, p, pt: (row0 + b, k_col // w)),
            pl.BlockSpec((t_new, w), lambda b, p, pt: (row0 + b, v_col // w)),
            pl.BlockSpec((t_new, LANES), lambda b, p, pt: (row0 + b, 0)),
            *[page_spec(g) for g in range(n_group)],
            *[page_spec(g) for g in range(n_group)],
            pl.BlockSpec((1, heads * n_pages, page), lambda b, p, pt: (b, 0, 0)),
            pl.BlockSpec((1, 1, LANES), lambda b, p, pt: (b, 0, 0)),
        ],
        out_specs=pl.BlockSpec((t_new, w), lambda b, p, pt: (b, 0)),
        scratch_shapes=[
            pltpu.VMEM((heads * t_new, 1), F32), pltpu.VMEM((heads * t_new, 1), F32),
            pltpu.VMEM((heads * t_new, dh), F32), pltpu.VMEM((t_new, LANES), F32),
        ],
    )
    return pl.pallas_call(
        kern,
        grid_spec=grid_spec,
        out_shape=jax.ShapeDtypeStruct((bs * t_new, w), F32),
        compiler_params=_params(("parallel", "arbitrary")),
        name="fox_sample",
    )(page_table, proj, proj, proj, logf, *([kc] * n_group), *([vc] * n_group), cpast, ctot)


def _sample_attn_kernel(pt_ref, qd_ref, knd_ref, vnd_ref, qf_ref, knf_ref, vnf_ref, lfn_ref, *refs,
                        h_a, dk, dv, h_b, dh, page, n_pages, n_group, lam_init):
    del pt_ref
    g = n_group
    kcd, vcd, kcf, vcf = refs[:g], refs[g:2 * g], refs[2 * g:3 * g], refs[3 * g:4 * g]
    (lq1, lk1, lq2, lk2, sg_ref, cp_ref, ct_ref, od_ref, of_ref,
     md_ref, ld_ref, accd_ref, mf_ref, lf_ref, accf_ref, cn_ref) = refs[4 * g:]
    p = pl.program_id(1)
    last = pl.num_programs(1) - 1
    t_new = qd_ref.shape[0]
    n_maps = 2 * h_a
    nc = dv // LANES
    scale_d, scale_f = dk ** -0.5, dh ** -0.5

    @pl.when(p == 0)
    def _():
        for m_ref, l_ref, acc_ref in ((md_ref, ld_ref, accd_ref), (mf_ref, lf_ref, accf_ref)):
            m_ref[...] = jnp.full(m_ref.shape, NEG_INF, F32)
            l_ref[...] = jnp.zeros(l_ref.shape, F32)
            acc_ref[...] = jnp.zeros(acc_ref.shape, F32)
        x = lfn_ref[...]
        row = lax.broadcasted_iota(jnp.int32, x.shape, 0)
        k = 1
        while k < x.shape[0]:
            x = x + jnp.where(row >= k, pltpu.roll(x, k, 0), 0.0)
            k *= 2
        cn_ref[...] = ct_ref[0] + x

    qd = qd_ref[...]
    qdb = [qd[:, hm * dk:(hm + 1) * dk].astype(BF16) for hm in range(n_maps)]
    s_d = jnp.concatenate([
        jnp.concatenate([_dot_nt(qdb[hm], kc[pl.ds(hm, page, stride=n_maps), :].astype(BF16)) for kc in kcd],
                        axis=1) for hm in range(n_maps)], axis=0) * scale_d
    v_d = [jnp.concatenate([
        jnp.concatenate([vc[pl.ds(c * h_a + h, page, stride=h_a * nc), :] for c in range(nc)], axis=1)
        for vc in vcd], axis=0).astype(BF16) for h in range(h_a)]
    _softmax_step_stacked(s_d, v_d, md_ref, ld_ref, accd_ref, 2 * t_new)

    qf = qf_ref[...]
    cn = cn_ref[...]
    qfb = [qf[:, h * dh:(h + 1) * dh].astype(BF16) for h in range(h_b)]
    cq = [cn[:, h:h + 1] for h in range(h_b)]
    s_f = jnp.concatenate([
        jnp.concatenate([_dot_nt(qfb[h], kc[pl.ds(h, page, stride=h_b), :].astype(BF16)) for kc in kcf],
                        axis=1) for h in range(h_b)], axis=0) * scale_f
    bias = jnp.concatenate([
        cq[h] - jnp.concatenate([cp_ref[0, pl.ds(h * n_pages + p * g + i, 1), :] for i in range(g)], axis=1)
        for h in range(h_b)], axis=0)
    v_f = [jnp.concatenate([vc[pl.ds(h, page, stride=h_b), :] for vc in vcf], axis=0).astype(BF16)
           for h in range(h_b)]
    _softmax_step_stacked(s_f + bias, v_f, mf_ref, lf_ref, accf_ref, t_new)

    @pl.when(p == last)
    def _():
        knd, vnd = knd_ref[...], vnd_ref[...]
        sd_new = jnp.concatenate([
            _dot_nt(qdb[hm], _pad_rows(knd[:, hm * dk:(hm + 1) * dk], page).astype(BF16))
            for hm in range(n_maps)], axis=0) * scale_d
        vd_new = [_pad_rows(vnd[:, h * dv:(h + 1) * dv], page).astype(BF16) for h in range(h_a)]
        _softmax_step_stacked(_causal_new(sd_new, t_new), vd_new, md_ref, ld_ref, accd_ref, 2 * t_new)

        knf, vnf = knf_ref[...], vnf_ref[...]
        sf_new = jnp.concatenate([
            _dot_nt(qfb[h], _pad_rows(knf[:, h * dh:(h + 1) * dh], page).astype(BF16)) for h in range(h_b)],
            axis=0) * scale_f
        row = lax.broadcasted_iota(jnp.int32, (t_new, page), 0)
        col = lax.broadcasted_iota(jnp.int32, (t_new, page), 1)
        bias_new = jnp.concatenate([
            cq[h] - jnp.sum(jnp.where(row == col, cq[h], 0.0), axis=0, keepdims=True) for h in range(h_b)],
            axis=0)
        vf_new = [_pad_rows(vnf[:, h * dh:(h + 1) * dh], page).astype(BF16) for h in range(h_b)]
        _softmax_step_stacked(_causal_new(sf_new + bias_new, t_new), vf_new, mf_ref, lf_ref, accf_ref, t_new)

        lam = _diff_lambda(lq1, lk1, lq2, lk2, lam_init)
        o = accd_ref[...] / ld_ref[...]
        for h in range(h_a):
            r0 = 2 * h * t_new
            oh = o[r0:r0 + t_new, :] - lam * o[r0 + t_new:r0 + 2 * t_new, :]
            od_ref[:, h * dv:(h + 1) * dv] = _rms(oh, sg_ref[...], SUBLN_EPS) * (1.0 - lam_init)
        o = accf_ref[...] / lf_ref[...]
        for h in range(h_b):
            of_ref[:, h * dh:(h + 1) * dh] = o[h * t_new:(h + 1) * t_new, :]


def _sample_attn_call(page_table, proj, logf, caches, lams, sg, cpast, ctot, layer, *, n_prompt, t_new,
                      h_a, dk, dv, h_b, dh, page, cols, lam_init, n_group):
    bs, n_pages = page_table.shape
    wd, wf = h_a * dv, h_b * dh
    row0 = n_prompt // t_new
    kern = functools.partial(_sample_attn_kernel, h_a=h_a, dk=dk, dv=dv, h_b=h_b, dh=dh, page=page,
                             n_pages=n_pages, n_group=n_group, lam_init=lam_init)
    new_rows = lambda w, col: pl.BlockSpec((t_new, w), lambda b, p, pt: (row0 + b, col // w))
    const = lambda shape: pl.BlockSpec(shape, lambda b, p, pt: (0,) * len(shape))

    def pages(cache):
        rows = cache.shape[2]
        return [pl.BlockSpec((None, None, rows, LANES),
                             lambda b, p, pt, i=i: (layer, pt[b, p * n_group + i], 0, 0)) for i in range(n_group)]

    grid_spec = pltpu.PrefetchScalarGridSpec(
        num_scalar_prefetch=1,
        grid=(bs, n_pages // n_group),
        in_specs=[
            new_rows(wd, cols[0]), new_rows(wd, cols[1]), new_rows(wd, cols[2]),
            new_rows(wf, cols[3]), new_rows(wf, cols[4]), new_rows(wf, cols[5]),
            pl.BlockSpec((t_new, LANES), lambda b, p, pt: (row0 + b, 0)),
            *pages(caches[0]), *pages(caches[1]), *pages(caches[2]), *pages(caches[3]),
            const((1, dk)), const((1, dk)), const((1, dk)), const((1, dk)), const((1, dv)),
            pl.BlockSpec((1, h_b * n_pages, page), lambda b, p, pt: (b, 0, 0)),
            pl.BlockSpec((1, 1, LANES), lambda b, p, pt: (b, 0, 0)),
        ],
        out_specs=[pl.BlockSpec((t_new, wd), lambda b, p, pt: (b, 0)),
                   pl.BlockSpec((t_new, wf), lambda b, p, pt: (b, 0))],
        scratch_shapes=[
            pltpu.VMEM((2 * h_a * t_new, 1), F32), pltpu.VMEM((2 * h_a * t_new, 1), F32),
            pltpu.VMEM((2 * h_a * t_new, dv), F32),
            pltpu.VMEM((h_b * t_new, 1), F32), pltpu.VMEM((h_b * t_new, 1), F32),
            pltpu.VMEM((h_b * t_new, dh), F32), pltpu.VMEM((t_new, LANES), F32),
        ],
    )
    args = [proj] * 6 + [logf]
    for cache in caches:
        args += [cache] * n_group
    return pl.pallas_call(
        kern,
        grid_spec=grid_spec,
        out_shape=[jax.ShapeDtypeStruct((bs * t_new, wd), F32), jax.ShapeDtypeStruct((bs * t_new, wf), F32)],
        compiler_params=_params(("parallel", "arbitrary")),
        name="sample_attn",
    )(page_table, *args, *lams, sg, cpast, ctot)


def _lru_coeffs(x, xprev_ext, gate_w, cw_ref, cb_ref, wga_ref, bga_ref, wgx_ref, bgx_ref, lam_ref, conv_w):
    rows = x.shape[0]
    nb = wga_ref.shape[0]
    bw = wga_ref.shape[1]
    u = xprev_ext(conv_w - 1) * cw_ref[0:1, :]
    for k in range(1, conv_w - 1):
        u = u + xprev_ext(conv_w - 1 - k) * cw_ref[k:k + 1, :]
    u = u + x * cw_ref[conv_w - 1:conv_w, :]
    u = u + cb_ref[...]
    ub = u.astype(BF16)
    r_parts, i_parts = [], []
    for n in range(nb):
        un = ub[:, n * bw:(n + 1) * bw]
        r_parts.append(jnp.dot(un, wga_ref[n], preferred_element_type=F32))
        i_parts.append(jnp.dot(un, wgx_ref[n], preferred_element_type=F32))
    r = _sigmoid(jnp.concatenate(r_parts, axis=-1) + bga_ref[...])
    i = _sigmoid(jnp.concatenate(i_parts, axis=-1) + bgx_ref[...])
    log_a = -LRU_C * r * _softplus(-lam_ref[...])
    a = jnp.exp(log_a)
    b = jnp.sqrt(-jnp.tanh(log_a) * (a * a + 1.0)) * (i * u)
    del rows, gate_w
    return a, b


def _group_scan(a, b):
    row = lax.broadcasted_iota(jnp.int32, a.shape, 0) % SUBLANES
    k = 1
    while k < SUBLANES:
        keep = row >= k
        a_sh = jnp.where(keep, pltpu.roll(a, k, 0), 1.0)
        b_sh = jnp.where(keep, pltpu.roll(b, k, 0), 0.0)
        b = a * b_sh + b
        a = a * a_sh
        k *= 2
    return a, b


def _gelu_tanh(x):
    return 0.5 * x * (1.0 + jnp.tanh(math.sqrt(2.0 / math.pi) * (x + 0.044715 * (x * x * x))))


def _lru_prompt_kernel(x_ref, g_ref, cw_ref, cb_ref, wga_ref, bga_ref, wgx_ref, bgx_ref, lam_ref,
                       y_ref, ht_ref, tail_ref, hl_ref, a_ref, b_ref, hs_ref, *, conv_w):
    ti = pl.program_id(1)

    @pl.when(ti == 0)
    def _():
        tail_ref[...] = jnp.zeros(tail_ref.shape, F32)
        hl_ref[...] = jnp.zeros(hl_ref.shape, F32)

    x = x_ref[...]
    tt = x.shape[0]
    xe = jnp.concatenate([tail_ref[...], x], axis=0)
    delayed = lambda k: pltpu.roll(xe, k, 0)[SUBLANES:, :]
    a, b = _lru_coeffs(x, delayed, None, cw_ref, cb_ref, wga_ref, bga_ref, wgx_ref, bgx_ref, lam_ref, conv_w)
    a, b = _group_scan(a, b)
    a_ref[...] = a
    b_ref[...] = b
    tail_ref[...] = x[tt - SUBLANES:, :]

    def body(g, hlast):
        r0 = pl.multiple_of(g * SUBLANES, SUBLANES)
        hg = b_ref[pl.ds(r0, SUBLANES), :] + a_ref[pl.ds(r0, SUBLANES), :] * hlast
        hs_ref[pl.ds(r0, SUBLANES), :] = hg
        return hg[SUBLANES - 1:SUBLANES, :]

    hl = lax.fori_loop(0, tt // SUBLANES, body, hl_ref[...])
    hl_ref[...] = hl
    ht_ref[0] = hl
    y_ref[...] = (_gelu_tanh(g_ref[...]) * hs_ref[...]).astype(y_ref.dtype)


def _lru_prompt_call(proj, wts, *, batch, seq, wc, tt, g_col, x_col):
    cw, cb, wga, bga, wgx, bgx, lam = wts
    nt = seq // tt
    conv_w = cw.shape[0]
    full = lambda a: pl.BlockSpec(a.shape, lambda b, t: (0,) * a.ndim)
    kern = functools.partial(_lru_prompt_kernel, conv_w=conv_w)
    return pl.pallas_call(
        kern,
        grid=(batch, nt),
        in_specs=[
            pl.BlockSpec((tt, wc), lambda b, t: (b * nt + t, x_col // wc)),
            pl.BlockSpec((tt, wc), lambda b, t: (b * nt + t, g_col // wc)),
            full(cw), full(cb), full(wga), full(bga), full(wgx), full(bgx), full(lam),
        ],
        out_specs=[pl.BlockSpec((tt, wc), lambda b, t: (b * nt + t, 0)),
                   pl.BlockSpec((1, 1, wc), lambda b, t: (b, 0, 0))],
        out_shape=[jax.ShapeDtypeStruct((batch * seq, wc), BF16),
                   jax.ShapeDtypeStruct((batch, 1, wc), F32)],
        scratch_shapes=[pltpu.VMEM((SUBLANES, wc), F32), pltpu.VMEM((1, wc), F32),
                        pltpu.VMEM((tt, wc), F32), pltpu.VMEM((tt, wc), F32), pltpu.VMEM((tt, wc), F32)],
        compiler_params=_params(("parallel", "arbitrary")),
        name="lru_prompt",
    )(proj, proj, cw, cb, wga, bga, wgx, bgx, lam)


def _lru_sample_kernel(x_ref, g_ref, prev_ref, h0_ref, cw_ref, cb_ref, wga_ref, bga_ref, wgx_ref, bgx_ref,
                       lam_ref, y_ref, hs_ref, *, conv_w):
    x = x_ref[...]
    rows = x.shape[0]
    row = lax.broadcasted_iota(jnp.int32, x.shape, 0) % SUBLANES
    prev = prev_ref[...]

    def delayed(k):
        return jnp.where(row >= k, pltpu.roll(x, k, 0), pltpu.roll(prev, (k - SUBLANES) % rows, 0))

    a, b = _lru_coeffs(x, delayed, None, cw_ref, cb_ref, wga_ref, bga_ref, wgx_ref, bgx_ref, lam_ref, conv_w)
    a, b = _group_scan(a, b)
    hs = b + a * h0_ref[...]
    hs_ref[...] = hs
    y_ref[...] = _gelu_tanh(g_ref[...]) * hs


def _lru_sample_call(proj, prev8, h0rep, wts, *, n_prompt, n_sample, wc, g_col, x_col):
    cw, cb, wga, bga, wgx, bgx, lam = wts
    conv_w = cw.shape[0]
    rb = n_prompt // n_sample
    full = lambda a: pl.BlockSpec(a.shape, lambda i: (0,) * a.ndim)
    kern = functools.partial(_lru_sample_kernel, conv_w=conv_w)
    return pl.pallas_call(
        kern,
        grid=(1,),
        in_specs=[
            pl.BlockSpec((n_sample, wc), lambda i: (rb, x_col // wc)),
            pl.BlockSpec((n_sample, wc), lambda i: (rb, g_col // wc)),
            full(prev8), full(h0rep),
            full(cw), full(cb), full(wga), full(bga), full(wgx), full(bgx), full(lam),
        ],
        out_specs=[pl.BlockSpec((n_sample, wc), lambda i: (0, 0)),
                   pl.BlockSpec((n_sample, wc), lambda i: (0, 0))],
        out_shape=[jax.ShapeDtypeStruct((n_sample, wc), F32), jax.ShapeDtypeStruct((n_sample, wc), F32)],
        compiler_params=_params(("arbitrary",)),
        name="lru_sample",
    )(proj, proj, prev8, h0rep, cw, cb, wga, bga, wgx, bgx, lam)


def _mix_kernel(ap_ref, bp_ref, cp_ref, as_ref, bs_ref, cs_ref, w_ref, x_ref, g_ref, *rest,
                n_experts, n_prompt_tiles):
    if n_experts:
        wrh_ref, wrl_ref, x1_ref, rg_ref, re_ref = rest
    else:
        x1_ref, hn_ref = rest
    i = pl.program_id(0)
    wa, wb = ap_ref.shape[1], bp_ref.shape[1]

    def project(a_ref, b_ref, c_ref):
        d = lambda v, lo, hi: jnp.dot(v[...].astype(BF16), w_ref[lo:hi, :], preferred_element_type=F32)
        return d(a_ref, 0, wa) + d(b_ref, wa, wa + wb) + d(c_ref, wa + wb, w_ref.shape[0])

    @pl.when(i < n_prompt_tiles)
    def _():
        x1_ref[...] = x_ref[...] + project(ap_ref, bp_ref, cp_ref)

    @pl.when(i >= n_prompt_tiles)
    def _():
        x1_ref[...] = x_ref[...] + project(as_ref, bs_ref, cs_ref)

    hn = _rms(x1_ref[...], g_ref[...], RMS_EPS)
    if not n_experts:
        hn_ref[...] = hn.astype(BF16)
    else:
        h_hi = hn.astype(BF16)
        h_lo = (hn - h_hi.astype(F32)).astype(BF16)
        d = lambda a, b: jnp.dot(a, b[...], preferred_element_type=F32)
        logits = d(h_hi, wrh_ref) + (d(h_lo, wrh_ref) + d(h_hi, wrl_ref))
        lane = lax.broadcasted_iota(jnp.int32, logits.shape, 1)
        lane_f = lane.astype(F32)
        big = float(LANES)
        lg = jnp.where(lane < n_experts, logits, NEG_INF)
        v1 = jnp.max(lg, axis=-1, keepdims=True)
        i1 = jnp.min(jnp.where(lg == v1, lane_f, big), axis=-1, keepdims=True)
        lg2 = jnp.where(lane_f == i1, NEG_INF, lg)
        v2 = jnp.max(lg2, axis=-1, keepdims=True)
        i2 = jnp.min(jnp.where(lg2 == v2, lane_f, big), axis=-1, keepdims=True)
        e = jnp.exp(v2 - v1)
        g1 = 1.0 / (1.0 + e)
        g2 = e / (1.0 + e)
        rg_ref[...] = jnp.where(lane == 0, g1, jnp.where(lane == 1, g2, 0.0))
        re_ref[...] = jnp.where(lane == 0, i1, jnp.where(lane == 1, i2, 0.0)).astype(jnp.int32)


def _mix_call(branches_p, branches_s, w_out, x, g, w_router, *, tm):
    n, d = x.shape
    npt = branches_p[0].shape[0] // tm
    n_experts = 0 if w_router is None else w_router.shape[1]
    kern = functools.partial(_mix_kernel, n_experts=n_experts, n_prompt_tiles=npt)
    row = lambda c: pl.BlockSpec((tm, c), lambda i: (i, 0))
    row_p = lambda a: pl.BlockSpec((tm, a.shape[1]), lambda i: (jnp.minimum(i, npt - 1), 0))
    row_s = lambda a: pl.BlockSpec((tm, a.shape[1]), lambda i: (jnp.maximum(i - npt, 0), 0))
    in_specs = ([row_p(a) for a in branches_p] + [row_s(a) for a in branches_s]
                + [pl.BlockSpec((d, d), lambda i: (0, 0)), row(d), pl.BlockSpec((1, d), lambda i: (0, 0))])
    args = [*branches_p, *branches_s, w_out, x, g]
    if n_experts:
        wr = jnp.zeros((d, LANES), F32).at[:, :n_experts].set(w_router)
        wr_hi = wr.astype(BF16)
        wr_lo = (wr - wr_hi.astype(F32)).astype(BF16)
        in_specs += [pl.BlockSpec((d, LANES), lambda i: (0, 0))] * 2
        out_specs = [row(d), row(LANES), row(LANES)]
        out_shape = [jax.ShapeDtypeStruct((n, d), F32), jax.ShapeDtypeStruct((n, LANES), F32),
                     jax.ShapeDtypeStruct((n, LANES), jnp.int32)]
        args += [wr_hi, wr_lo]
    else:
        out_specs = [row(d), row(d)]
        out_shape = [jax.ShapeDtypeStruct((n, d), F32), jax.ShapeDtypeStruct((n, d), BF16)]
    return pl.pallas_call(
        kern,
        grid=(n // tm,),
        in_specs=in_specs,
        out_specs=out_specs,
        out_shape=out_shape,
        compiler_params=_params(("parallel",)),
        name="mix_router" if n_experts else "mix",
    )(*args)


def _swiglu_acc(h, wg, wu, wd):
    g = jnp.dot(h, wg, preferred_element_type=F32)
    u = jnp.dot(h, wu, preferred_element_type=F32)
    a = (g * _sigmoid(g) * u).astype(BF16)
    return jnp.dot(a, wd, preferred_element_type=F32)


def _ffn_kernel(h_ref, wg_ref, wu_ref, wd_ref, x_ref, o_ref):
    j = pl.program_id(1)

    @pl.when(j == 0)
    def _():
        o_ref[...] = x_ref[...]

    o_ref[...] += _swiglu_acc(h_ref[...], wg_ref[...], wu_ref[...], wd_ref[...])


def _ffn_call(hn, wg, wu, wd, x1, *, tm, tf):
    n, d = x1.shape
    f = wg.shape[1]
    return pl.pallas_call(
        _ffn_kernel,
        grid=(n // tm, f // tf),
        in_specs=[
            pl.BlockSpec((tm, d), lambda i, j: (i, 0)),
            pl.BlockSpec((d, tf), lambda i, j: (0, j)),
            pl.BlockSpec((d, tf), lambda i, j: (0, j)),
            pl.BlockSpec((tf, d), lambda i, j: (j, 0)),
            pl.BlockSpec((tm, d), lambda i, j: (i, 0), pipeline_mode=pl.Buffered(1)),
        ],
        out_specs=pl.BlockSpec((tm, d), lambda i, j: (i, 0)),
        out_shape=jax.ShapeDtypeStruct((n, d), F32),
        compiler_params=_params(("parallel", "arbitrary")),
        name="ffn_dense",
    )(hn, wg, wu, wd, x1)


def _gather_norm_kernel(rt_ref, nv_ref, x_hbm, g_ref, o_ref, buf_ref, sem, *, tg):
    t = pl.program_id(0)

    def row_copy(r, tok):
        return pltpu.make_async_copy(x_hbm.at[pl.ds(tok, 1), :], buf_ref.at[pl.ds(r, 1), :], sem)

    @pl.when(t < nv_ref[0])
    def _():
        def issue(r, carry):
            row_copy(r, rt_ref[t * tg + r]).start()
            return carry

        lax.fori_loop(0, tg, issue, 0, unroll=8)
        pltpu.make_async_copy(buf_ref, buf_ref, sem).wait()
        o_ref[...] = _rms(buf_ref[...], g_ref[...], RMS_EPS).astype(BF16)

    @pl.when(t >= nv_ref[0])
    def _():
        o_ref[...] = jnp.zeros(o_ref.shape, o_ref.dtype)


def _gather_norm_call(row_token, n_valid, x1, g, *, tg):
    r = row_token.shape[0]
    d = x1.shape[1]
    grid_spec = pltpu.PrefetchScalarGridSpec(
        num_scalar_prefetch=2,
        grid=(r // tg,),
        in_specs=[pl.BlockSpec(memory_space=pl.ANY), pl.BlockSpec((1, d), lambda t, rt, nv: (0, 0))],
        out_specs=pl.BlockSpec((tg, d), lambda t, rt, nv: (t, 0)),
        scratch_shapes=[pltpu.VMEM((tg, d), F32), pltpu.SemaphoreType.DMA(())],
    )
    return pl.pallas_call(
        functools.partial(_gather_norm_kernel, tg=tg),
        grid_spec=grid_spec,
        out_shape=jax.ShapeDtypeStruct((r, d), BF16),
        compiler_params=_params(("arbitrary",)),
        name="moe_gather",
    )(row_token, n_valid, x1, g)


def _moe_ffn_kernel(te_ref, nv_ref, h_ref, wg_ref, wu_ref, wd_ref, o_ref):
    del te_ref
    t = pl.program_id(0)
    j = pl.program_id(1)

    @pl.when(j == 0)
    def _():
        o_ref[...] = jnp.zeros(o_ref.shape, F32)

    @pl.when(t < nv_ref[0])
    def _():
        o_ref[...] += _swiglu_acc(h_ref[...], wg_ref[...], wu_ref[...], wd_ref[...])


def _moe_ffn_call(tile_expert, n_valid, xs, wg, wu, wd, *, tm, tf):
    r, d = xs.shape
    f = wg.shape[2]
    nj = f // tf

    def tile(t, nv):
        return jnp.minimum(t, nv[0] - 1)

    def fblk(t, j, nv):
        return jnp.where(t < nv[0], j, nj - 1)

    grid_spec = pltpu.PrefetchScalarGridSpec(
        num_scalar_prefetch=2,
        grid=(r // tm, nj),
        in_specs=[
            pl.BlockSpec((tm, d), lambda t, j, te, nv: (tile(t, nv), 0)),
            pl.BlockSpec((None, d, tf), lambda t, j, te, nv: (te[t], 0, fblk(t, j, nv))),
            pl.BlockSpec((None, d, tf), lambda t, j, te, nv: (te[t], 0, fblk(t, j, nv))),
            pl.BlockSpec((None, tf, d), lambda t, j, te, nv: (te[t], fblk(t, j, nv), 0)),
        ],
        out_specs=pl.BlockSpec((tm, d), lambda t, j, te, nv: (t, 0)),
    )
    return pl.pallas_call(
        _moe_ffn_kernel,
        grid_spec=grid_spec,
        out_shape=jax.ShapeDtypeStruct((r, d), F32),
        compiler_params=_params(("arbitrary", "arbitrary")),
        name="moe_ffn",
    )(tile_expert, n_valid, xs, wg, wu, wd)


def _combine_norm_kernel(pos_ref, x_ref, rg_ref, ys_hbm, g_ref, *rest, tc, final_norm, n_prompt_tiles):
    if final_norm:
        op_ref, os_ref, buf_ref, sem = rest
    else:
        o_ref, buf_ref, sem = rest
    i = pl.program_id(0)

    def row_copy(k, r, src):
        return pltpu.make_async_copy(ys_hbm.at[pl.ds(src, 1), :], buf_ref.at[k, pl.ds(r, 1), :], sem)

    def issue(r, carry):
        for k in range(TOP_K):
            row_copy(k, r, pos_ref[(i * tc + r) * TOP_K + k]).start()
        return carry

    lax.fori_loop(0, tc, issue, 0, unroll=4)
    pltpu.make_async_copy(buf_ref, buf_ref, sem).wait()
    rg = rg_ref[...]
    moe = rg[:, 0:1] * buf_ref[0]
    for k in range(1, TOP_K):
        moe = moe + rg[:, k:k + 1] * buf_ref[k]
    x2 = x_ref[...] + moe
    if final_norm:
        y = _rms(x2, g_ref[...], RMS_EPS)

        @pl.when(i < n_prompt_tiles)
        def _():
            op_ref[...] = y

        @pl.when(i >= n_prompt_tiles)
        def _():
            os_ref[...] = y
    else:
        o_ref[...] = x2


def _combine_norm_call(pos, x1, rg, ys, g, *, tc, final_norm, n_prompt):
    n, d = x1.shape
    npt = n_prompt // tc
    if final_norm:
        out_specs = [pl.BlockSpec((tc, d), lambda i, ps: (jnp.minimum(i, npt - 1), 0)),
                     pl.BlockSpec((tc, d), lambda i, ps: (jnp.maximum(i - npt, 0), 0))]
        out_shape = [jax.ShapeDtypeStruct((n_prompt, d), F32), jax.ShapeDtypeStruct((n - n_prompt, d), F32)]
    else:
        out_specs = pl.BlockSpec((tc, d), lambda i, ps: (i, 0))
        out_shape = jax.ShapeDtypeStruct((n, d), F32)
    grid_spec = pltpu.PrefetchScalarGridSpec(
        num_scalar_prefetch=1,
        grid=(n // tc,),
        in_specs=[
            pl.BlockSpec((tc, d), lambda i, ps: (i, 0)),
            pl.BlockSpec((tc, LANES), lambda i, ps: (i, 0)),
            pl.BlockSpec(memory_space=pl.ANY),
            pl.BlockSpec((1, d), lambda i, ps: (0, 0)),
        ],
        out_specs=out_specs,
        scratch_shapes=[pltpu.VMEM((TOP_K, tc, d), F32), pltpu.SemaphoreType.DMA(())],
    )
    return pl.pallas_call(
        functools.partial(_combine_norm_kernel, tc=tc, final_norm=final_norm, n_prompt_tiles=npt),
        grid_spec=grid_spec,
        out_shape=out_shape,
        compiler_params=_params(("arbitrary",)),
        name="moe_combine",
    )(pos, x1, rg, ys, g)


def _norm_kernel(x_ref, g_ref, o_ref):
    o_ref[...] = _rms(x_ref[...], g_ref[...], RMS_EPS)


def _norm_call(x, g, *, tm):
    n, d = x.shape
    return pl.pallas_call(
        _norm_kernel,
        grid=(n // tm,),
        in_specs=[pl.BlockSpec((tm, d), lambda i: (i, 0)), pl.BlockSpec((1, d), lambda i: (0, 0))],
        out_specs=pl.BlockSpec((tm, d), lambda i: (i, 0)),
        out_shape=jax.ShapeDtypeStruct((n, d), F32),
        compiler_params=_params(("parallel",)),
        name="final_norm",
    )(x, g)


def _route_plan(re, n_experts, tm):
    n = re.shape[0]
    e_flat = re.reshape(-1)
    oh = (e_flat[:, None] == jnp.arange(n_experts, dtype=jnp.int32)[None, :]).astype(jnp.int32)
    csum = jnp.cumsum(oh, axis=0)
    rank = jnp.sum((csum - oh) * oh, axis=1)
    counts = csum[-1]
    padded = ((counts + tm - 1) // tm) * tm
    ends = jnp.cumsum(padded)
    offs = ends - padded
    pos = jnp.sum(oh * offs[None, :], axis=1) + rank
    n_rows = ((n * TOP_K + tm - 1) // tm) * tm + n_experts * tm
    row_token = jnp.zeros((n_rows,), jnp.int32).at[pos].set(jnp.arange(n * TOP_K, dtype=jnp.int32) // TOP_K)
    n_tiles = n_rows // tm
    n_valid = (ends[-1] // tm).astype(jnp.int32)
    tile_start = jnp.arange(n_tiles, dtype=jnp.int32) * tm
    tile_e = jnp.sum((tile_start[:, None] >= ends[None, :]).astype(jnp.int32), axis=1)
    last_e = jnp.sum((tile_start[n_valid - 1] >= ends).astype(jnp.int32))
    tile_e = jnp.where(jnp.arange(n_tiles) < n_valid, tile_e, last_e).astype(jnp.int32)
    return pos.astype(jnp.int32), row_token, tile_e, n_valid.reshape(1)


def _tile(pref, *dims):
    t = pref
    for dim in dims:
        t = math.gcd(t, dim)
    return t


def _rope_tables(seq, n_past, t_new, bs, dk, rot_dim):
    half = rot_dim // 2
    inv = ROPE_THETA ** (-(jnp.arange(half, dtype=F32) * 2.0) / rot_dim)
    pos = jnp.concatenate([jnp.arange(seq), jnp.tile(n_past + jnp.arange(t_new), bs)])
    ang = pos.astype(F32)[:, None] * inv[None, :]
    cos, sin = jnp.cos(ang), jnp.sin(ang)
    n = pos.shape[0]
    ones = jnp.ones((n, dk - rot_dim), F32)
    zeros_h = jnp.zeros((n, half), F32)
    zeros_r = jnp.zeros((n, dk - rot_dim), F32)
    cos_t = jnp.concatenate([cos, cos, ones], axis=1)
    sa_t = jnp.concatenate([zeros_h, sin, zeros_r], axis=1)
    sb_t = jnp.concatenate([-sin, zeros_h, zeros_r], axis=1)
    return cos_t, sa_t, sb_t


def _page_mats(heads, page, n_pages):
    w = heads * page
    dst = jnp.arange(w)
    d_head, d_slot = dst // page, dst % page
    same = d_head[:, None] == d_head[None, :]
    mcs = (same & (d_slot[:, None] <= d_slot[None, :])).astype(BF16)
    mbc = ((d_head[:, None] == d_head[None, :]) & (d_slot[:, None] == page - 1)).astype(BF16)
    msel = ((d_slot[:, None] == page - 1) & (d_head[:, None] == jnp.arange(LANES)[None, :])).astype(BF16)
    pg = jnp.arange(n_pages)
    lst = (pg[None, :] < pg[:, None]).astype(BF16)
    return mcs, mbc, msel, lst


def kernel(x_prompt, x_sample, cache_k_diff, cache_v_diff, cache_k_fox, cache_v_fox, cache_logf_fox, state_rglru_h, state_rglru_conv, page_table, norm_mix_g, w_in, b_fgate, lambda_q1, lambda_k1, lambda_q2, lambda_k2, subln_g, conv_w, conv_b, w_gate_a, b_gate_a, w_gate_x, b_gate_x, lru_lambda, w_out, norm_ffn_g, w_dense_gate, w_dense_up, w_dense_down, w_router, w_moe_gate, w_moe_up, w_moe_down, norm_final_g):
    bp, seq, d = x_prompt.shape
    bs, t_new, _ = x_sample.shape
    depth = w_in.shape[0]
    n_pool, page, h_a = cache_k_diff.shape[1], cache_k_diff.shape[2], cache_k_diff.shape[3]
    dk_a = cache_k_diff.shape[5]
    dv_a = cache_v_diff.shape[4]
    h_b, dh_b = cache_k_fox.shape[3], cache_k_fox.shape[4]
    w_c = state_rglru_h.shape[2]
    cw_len = conv_w.shape[1]
    n_pages = page_table.shape[1]
    n_past = n_pages * page
    rot_dim = dk_a // 4
    n_experts = w_router.shape[2]
    n_p, n_s = bp * seq, bs * t_new
    n = n_p + n_s
    assert t_new == SUBLANES and page == LANES and dv_a == 2 * dk_a and dk_a == LANES and dh_b == LANES

    w_qa = h_a * 2 * dk_a
    w_va = h_a * dv_a
    w_b = h_b * dh_b
    col_qa, col_ka, col_va = 0, w_qa, 2 * w_qa
    col_qb = col_va + w_va
    col_kb, col_vb = col_qb + w_b, col_qb + 2 * w_b
    col_g = col_vb + w_b
    col_x = col_g + w_c
    n_main = col_x + w_c
    src_fl = 2 * w_qa + w_va + 3 * w_b

    tm = _tile(TM_TOKENS, seq, n_s)
    tn = _tile(TN_PROJ, w_qa, src_fl, n_main - src_fl)
    tq = _tile(TQ_ATTN, seq)
    tt = _tile(TT_LRU, seq)
    tm_mix = _tile(TM_MIX, tm)
    tf = _tile(TF_FFN, w_dense_gate.shape[2])
    tm_moe = _tile(TM_MOE, n)
    tc = _tile(TC_COMBINE, n)

    rope = _rope_tables(seq, n_past, t_new, bs, dk_a, rot_dim)
    tri = jnp.tril(jnp.ones((seq, seq), BF16))
    page_mats = _page_mats(h_b, page, n_pages)

    kc_d = cache_k_diff.reshape(depth, n_pool, page * h_a * 2, dk_a)
    nc_a = dv_a // LANES
    vc_d = jnp.swapaxes(cache_v_diff.reshape(depth, n_pool, page, h_a, nc_a, LANES), 3, 4).reshape(
        depth, n_pool, page * nc_a * h_a, LANES)
    kc_f = cache_k_fox.reshape(depth, n_pool, page * h_b, dh_b)
    vc_f = cache_v_fox.reshape(depth, n_pool, page * h_b, dh_b)
    lf_pages = jnp.swapaxes(cache_logf_fox, 2, 3).reshape(depth, n_pool, h_b * page)
    n_group = _tile(PAGES_PER_STEP, n_pages)

    x = jnp.concatenate([x_prompt.reshape(n_p, d), x_sample.reshape(n_s, d)], axis=0)
    row2 = lambda v: v.reshape(1, -1).astype(F32)
    st = {k: [] for k in ('kd', 'vd', 'kf', 'vf', 'lf', 'h_p', 'h_s', 'cv')}
    y_p = y_s = None
    cache_a = cache_b = None

    for l in range(depth):
        lam_init = 0.8 - 0.6 * math.exp(-0.3 * l)
        w_l = w_in[l]
        w_b2 = w_l[:, src_fl + h_b:].astype(BF16)
        b_fl = jnp.zeros((1, LANES), F32).at[0, :h_b].set(b_fgate[l])
        proj, logf = _proj_call(
            x, row2(norm_mix_g[l]), w_in, l, src_fl, w_b2, b_fl, rope, tm=tm, tn=tn, n_rope_cols=2 * w_qa,
            n_prompt_tiles=n_p // tm, rope_prompt_tiles=seq // tm, rot_half=rot_dim // 2)

        lams = (row2(lambda_q1[l]), row2(lambda_k1[l]), row2(lambda_q2[l]), row2(lambda_k2[l]))
        sg = row2(subln_g[l])

        oa_p, *cache_a = _diff_prompt_call(proj, lams, sg, cache_a, l, depth, batch=bp, seq=seq, heads=h_a,
                                           dk=dk_a, dv=dv_a, tq=tq, k_col=col_ka, v_col=col_va,
                                           lam_init=lam_init)
        c_col = _cumsum_call(logf, tri, bp, seq)
        c_row = jnp.swapaxes(c_col[:, :h_b].reshape(bp, seq, h_b), 1, 2).reshape(bp * h_b, seq // tq, tq)
        ob_p, *cache_b = _fox_prompt_call(proj, c_col, c_row, cache_b, l, depth, batch=bp, seq=seq, heads=h_b,
                                          dh=dh_b, tq=tq, q_col=col_qb, k_col=col_kb, v_col=col_vb)
        lru_w = (conv_w[l], row2(conv_b[l]), w_gate_a[l].astype(BF16), row2(b_gate_a[l]),
                 w_gate_x[l].astype(BF16), row2(b_gate_x[l]), row2(lru_lambda[l]))
        oc_p, ht_p = _lru_prompt_call(proj, lru_w, batch=bp, seq=seq, wc=w_c, tt=tt, g_col=col_g, x_col=col_x)

        cpast, ctot = _logf_pages_call(page_table, lf_pages[l], page_mats, heads=h_b, page=page)
        oa_s, ob_s = _sample_attn_call(
            page_table, proj, logf, (kc_d, vc_d, kc_f, vc_f), lams, sg, cpast, ctot, l, n_prompt=n_p,
            t_new=t_new, h_a=h_a, dk=dk_a, dv=dv_a, h_b=h_b, dh=dh_b, page=page,
            cols=(col_qa, col_ka, col_va, col_qb, col_kb, col_vb), lam_init=lam_init, n_group=n_group)
        prev8 = jnp.pad(state_rglru_conv[l], ((0, 0), (SUBLANES - (cw_len - 1), 0), (0, 0))).reshape(n_s, w_c)
        h0rep = jnp.repeat(state_rglru_h[l], t_new, axis=0)
        oc_s, hs_s = _lru_sample_call(proj, prev8, h0rep, lru_w, n_prompt=n_p, n_sample=n_s, wc=w_c,
                                      g_col=col_g, x_col=col_x)

        br_p, br_s = (oa_p, ob_p, oc_p), (oa_s, ob_s, oc_s)
        g_ffn = row2(norm_ffn_g[l])
        m = l // 2
        if l % 2 == 0:
            x1, hn = _mix_call(br_p, br_s, w_out[l].astype(BF16), x, g_ffn, None, tm=tm_mix)
            x = _ffn_call(hn, w_dense_gate[m].astype(BF16), w_dense_up[m].astype(BF16),
                          w_dense_down[m].astype(BF16), x1, tm=tm, tf=tf)
            if l == depth - 1:
                y_all = _norm_call(x, row2(norm_final_g), tm=tm_mix)
                y_p, y_s = y_all[:n_p], y_all[n_p:]
        else:
            x1, rg, re = _mix_call(br_p, br_s, w_out[l].astype(BF16), x, g_ffn, w_router[m], tm=tm_mix)
            pos, row_token, tile_e, n_valid = _route_plan(re[:, :TOP_K], n_experts, tm_moe)
            xs = _gather_norm_call(row_token, n_valid, x1, g_ffn, tg=tm_moe)
            ys = _moe_ffn_call(tile_e, n_valid, xs, w_moe_gate[m].astype(BF16), w_moe_up[m].astype(BF16),
                               w_moe_down[m].astype(BF16), tm=tm_moe,
                               tf=_tile(TF_MOE, w_moe_gate.shape[3]))
            final = l == depth - 1
            x = _combine_norm_call(pos, x1, rg, ys, row2(norm_final_g), tc=tc, final_norm=final, n_prompt=n_p)
            if final:
                y_p, y_s = x

        st['kd'].append(proj[n_p:, col_ka:col_ka + w_qa])
        st['vd'].append(proj[n_p:, col_va:col_va + w_va])
        st['kf'].append(proj[n_p:, col_kb:col_kb + w_b])
        st['vf'].append(proj[n_p:, col_vb:col_vb + w_b])
        st['lf'].append(logf[:, :h_b])
        st['h_p'].append(ht_p.reshape(bp, w_c))
        st['h_s'].append(hs_s.reshape(bs, t_new, w_c)[:, t_new - 1])
        st['cv'].append(proj[:, col_x:col_x + w_c])

    def both(name, shape_tail):
        a = jnp.stack(st[name])
        return (a[:, :n_p].reshape((depth, bp, seq) + shape_tail),
                a[:, n_p:].reshape((depth, bs, t_new) + shape_tail))

    sample = lambda name, tail: jnp.stack(st[name]).reshape((depth, bs, t_new) + tail)
    kd_s, vd_s = sample('kd', (h_a, 2, dk_a)), sample('vd', (h_a, dv_a))
    kf_s, vf_s = sample('kf', (h_b, dh_b)), sample('vf', (h_b, dh_b))
    kd_p = cache_a[0].reshape(depth, bp, seq, h_a, 2, dk_a)
    vd_p = jnp.swapaxes(cache_a[1].reshape(depth, bp, seq, nc_a, h_a, LANES), 3, 4).reshape(
        depth, bp, seq, h_a, dv_a)
    kf_p = cache_b[0].reshape(depth, bp, seq, h_b, dh_b)
    vf_p = cache_b[1].reshape(depth, bp, seq, h_b, dh_b)
    lf_p, lf_s = both('lf', (h_b,))
    cv_p, cv_s = both('cv', (w_c,))
    keep = cw_len - 1
    return (y_p.reshape(bp, seq, d), y_s.reshape(bs, t_new, d),
            kd_p, vd_p, kf_p, vf_p, lf_p, jnp.stack(st['h_p']), cv_p[:, :, seq - keep:],
            kd_s, vd_s, kf_s, vf_s, lf_s, jnp.stack(st['h_s']), cv_s[:, :, t_new - keep:])
```

```python
import functools
import math

import jax
import jax.numpy as jnp
from jax import lax
from jax.experimental import pallas as pl
from jax.experimental.pallas import tpu as pltpu

F32 = jnp.float32
BF16 = jnp.bfloat16

LANES = 128
SUBLANES = 8
VMEM_LIMIT = 56 * 1024 * 1024

TM_TOKENS = 1024
TN_PROJ = 512
TQ_ATTN = 512
TT_LRU = 256
TM_MIX = 512
TF_FFN = 256
TF_MOE = 256
TM_MOE = 1024
TC_COMBINE = 256
PAGES_PER_STEP = 8

RMS_EPS = 1e-6
SUBLN_EPS = 1e-5
LRU_C = 8.0
ROPE_THETA = 500000.0
TOP_K = 2
NEG_INF = float("-inf")
LOG2E = 1.4426950408889634


def _params(sem, vmem=VMEM_LIMIT):
    return pltpu.CompilerParams(dimension_semantics=sem, vmem_limit_bytes=vmem)


def _split3(x):
    hi = x.astype(BF16)
    r = x - hi.astype(F32)
    mid = r.astype(BF16)
    lo = (r - mid.astype(F32)).astype(BF16)
    return hi, mid, lo


def _dot_sel(x, m):
    hi, mid, lo = _split3(x)
    d = lambda a: jnp.dot(a, m, preferred_element_type=F32)
    return d(hi) + d(mid) + d(lo)


def _sel_dot(m, x):
    hi, mid, lo = _split3(x)
    d = lambda a: jnp.dot(m, a, preferred_element_type=F32)
    return d(hi) + d(mid) + d(lo)


def _dot_nt(a, b):
    return lax.dot_general(a, b, (((1,), (1,)), ((), ())), preferred_element_type=F32)


def _sigmoid(x):
    return 1.0 / (1.0 + jnp.exp(-x))


def _softplus(x):
    return jnp.maximum(x, 0.0) + jnp.log1p(jnp.exp(-jnp.abs(x)))


def _rms(x, g, eps):
    ms = jnp.mean(x * x, axis=-1, keepdims=True)
    return x * lax.rsqrt(ms + eps) * g


def _proj_kernel(x_ref, g_ref, wa_ref, wb_ref, wfl_ref, bf_ref, cos_ref, sa_ref, sb_ref,
                 proj_ref, logf_ref, hn_ref, *, n_rope_tiles, n_a_tiles, tn, rot_half):
    j = pl.program_id(1)

    @pl.when(j == 0)
    def _():
        hn = _rms(x_ref[...], g_ref[...], RMS_EPS).astype(BF16)
        hn_ref[...] = hn
        z = jnp.dot(hn, wfl_ref[...].astype(BF16), preferred_element_type=F32) + bf_ref[...]
        logf_ref[...] = jnp.minimum(z, 0.0) - jnp.log1p(jnp.exp(-jnp.abs(z)))

    @pl.when(j < n_rope_tiles)
    def _():
        y = jnp.dot(hn_ref[...], wa_ref[...].astype(BF16), preferred_element_type=F32)
        c, sa, sb = cos_ref[...], sa_ref[...], sb_ref[...]
        for k in range(tn // LANES):
            yc = y[:, k * LANES:(k + 1) * LANES]
            proj_ref[:, k * LANES:(k + 1) * LANES] = (
                yc * c + pltpu.roll(yc, rot_half, 1) * sa + pltpu.roll(yc, LANES - rot_half, 1) * sb)

    @pl.when((j >= n_rope_tiles) & (j < n_a_tiles))
    def _():
        proj_ref[...] = jnp.dot(hn_ref[...], wa_ref[...].astype(BF16), preferred_element_type=F32)

    @pl.when(j >= n_a_tiles)
    def _():
        proj_ref[...] = jnp.dot(hn_ref[...], wb_ref[...], preferred_element_type=F32)


def _proj_call(x, g, w_in, layer, fl_col, wb, bfl, rope, *, tm, tn, n_rope_cols, n_prompt_tiles,
               rope_prompt_tiles, rot_half):
    n, d = x.shape
    na, nb = fl_col // tn, wb.shape[1] // tn
    n_main = (na + nb) * tn
    cos_t, sa_t, sb_t = rope
    rope_map = lambda i, j: (jnp.where(i < n_prompt_tiles, i % rope_prompt_tiles,
                                       rope_prompt_tiles + i - n_prompt_tiles), 0)
    kern = functools.partial(_proj_kernel, n_rope_tiles=n_rope_cols // tn, n_a_tiles=na, tn=tn,
                             rot_half=rot_half)
    return pl.pallas_call(
        kern,
        grid=(n // tm, n_main // tn),
        in_specs=[
            pl.BlockSpec((tm, d), lambda i, j: (i, 0)),
            pl.BlockSpec((1, d), lambda i, j: (0, 0)),
            pl.BlockSpec((None, d, tn), lambda i, j: (layer, 0, jnp.minimum(j, na - 1))),
            pl.BlockSpec((d, tn), lambda i, j: (0, jnp.maximum(j - na, 0))),
            pl.BlockSpec((None, d, LANES), lambda i, j: (layer, 0, fl_col // LANES)),
            pl.BlockSpec((1, LANES), lambda i, j: (0, 0)),
            pl.BlockSpec((tm, LANES), rope_map),
            pl.BlockSpec((tm, LANES), rope_map),
            pl.BlockSpec((tm, LANES), rope_map),
        ],
        out_specs=[
            pl.BlockSpec((tm, tn), lambda i, j: (i, j)),
            pl.BlockSpec((tm, LANES), lambda i, j: (i, 0)),
        ],
        out_shape=[jax.ShapeDtypeStruct((n, n_main), F32), jax.ShapeDtypeStruct((n, LANES), F32)],
        scratch_shapes=[pltpu.VMEM((tm, d), BF16)],
        compiler_params=_params(("parallel", "arbitrary")),
        name="proj",
    )(x, g, w_in, wb, w_in, bfl, cos_t, sa_t, sb_t)


def _cumsum_kernel(x_ref, tri_ref, o_ref):
    o_ref[...] = _sel_dot(tri_ref[...], x_ref[...])


def _cumsum_call(logf_p, tri, batch, seq):
    return pl.pallas_call(
        _cumsum_kernel,
        grid=(batch,),
        in_specs=[pl.BlockSpec((seq, LANES), lambda b: (b, 0)),
                  pl.BlockSpec((seq, seq), lambda b: (0, 0))],
        out_specs=pl.BlockSpec((seq, LANES), lambda b: (b, 0)),
        out_shape=jax.ShapeDtypeStruct((batch * seq, LANES), F32),
        compiler_params=_params(("parallel",)),
        name="cumsum_prompt",
    )(logf_p, tri)


def _lane_chunks(s):
    return [s[:, c * LANES:(c + 1) * LANES] for c in range(s.shape[1] // LANES)]


def _rowmax(s):
    return jnp.max(functools.reduce(jnp.maximum, _lane_chunks(s)), axis=-1, keepdims=True)


def _rowsum(s):
    return jnp.sum(functools.reduce(jnp.add, _lane_chunks(s)), axis=-1, keepdims=True)


def _softmax_step(s, v, m_ref, l_ref, acc_ref, idx):
    m_old = m_ref[idx]
    m_new = jnp.maximum(m_old, _rowmax(s))
    alpha = jnp.exp2(m_old - m_new)
    p_chunks = [jnp.exp2(c - m_new) for c in _lane_chunks(s)]
    l_ref[idx] = alpha * l_ref[idx] + functools.reduce(jnp.add, p_chunks)
    p = jnp.concatenate(p_chunks, axis=1).astype(BF16)
    pv = jnp.dot(p, v, preferred_element_type=F32)
    acc_ref[idx] = jnp.concatenate([alpha] * (pv.shape[1] // LANES), axis=1) * acc_ref[idx] + pv
    m_ref[idx] = m_new


def _softmax_denominator(l_ref, idx):
    return jnp.sum(l_ref[idx], axis=-1, keepdims=True)


def _softmax_step_stacked(s, v_heads, m_ref, l_ref, acc_ref, rows_per_head):
    m_old = m_ref[...]
    m_new = jnp.maximum(m_old, _rowmax(s))
    p = jnp.exp(s - m_new)
    alpha = jnp.exp(m_old - m_new)
    l_ref[...] = alpha * l_ref[...] + _rowsum(p)
    pv = [jnp.dot(p[h * rows_per_head:(h + 1) * rows_per_head, :].astype(BF16), v,
                  preferred_element_type=F32) for h, v in enumerate(v_heads)]
    acc_ref[...] = alpha * acc_ref[...] + jnp.concatenate(pv, axis=0)
    m_ref[...] = m_new


def _diff_lambda(lq1, lk1, lq2, lk2, lam_init):
    return (jnp.exp(jnp.sum(lq1[...] * lk1[...], axis=-1, keepdims=True))
            - jnp.exp(jnp.sum(lq2[...] * lk2[...], axis=-1, keepdims=True)) + lam_init)


def _diff_finish(acc_ref, l_ref, base, lam, sg, lam_init):
    o = (acc_ref[base] / _softmax_denominator(l_ref, base)
         - lam * (acc_ref[base + 1] / _softmax_denominator(l_ref, base + 1)))
    return _rms(o, sg, SUBLN_EPS) * (1.0 - lam_init)


def _diff_prompt_kernel(q_ref, k_ref, v_ref, lq1, lk1, lq2, lk2, sg_ref, *rest, tq, dk, heads, n_alias,
                        lam_init, scale):
    o_ref, kc_ref, vc_ref, kb_ref, vb_ref, m_ref, l_ref, acc_ref = rest[n_alias:]
    h = pl.program_id(1)
    qi = pl.program_id(2)
    seq = k_ref.shape[0]

    @pl.when(qi == 0)
    def _():
        k = k_ref[...]
        v = v_ref[...]
        for m in range(2):
            kb_ref[m] = k[:, m * dk:(m + 1) * dk].T.astype(BF16)
            kc_ref[pl.ds(2 * h + m, seq, stride=2 * heads), :] = k[:, m * dk:(m + 1) * dk]
        vb_ref[...] = v.astype(BF16)
        nc = v.shape[1] // LANES
        for c in range(nc):
            vc_ref[pl.ds(c * heads + h, seq, stride=nc * heads), :] = v[:, c * LANES:(c + 1) * LANES]

    m_ref[...] = jnp.full(m_ref.shape, NEG_INF, F32)
    l_ref[...] = jnp.zeros(l_ref.shape, F32)
    acc_ref[...] = jnp.zeros(acc_ref.shape, F32)
    q = (q_ref[...] * (scale * LOG2E)).astype(BF16)

    def block(kk, causal):
        k0 = pl.multiple_of(kk * tq, tq)
        vblk = vb_ref[pl.ds(k0, tq), :]
        for m in range(2):
            s = jnp.dot(q[:, m * dk:(m + 1) * dk], kb_ref[m, :, pl.ds(k0, tq)],
                        preferred_element_type=F32)
            if causal:
                row = lax.broadcasted_iota(jnp.int32, s.shape, 0)
                col = lax.broadcasted_iota(jnp.int32, s.shape, 1)
                s = jnp.where(col <= row, s, NEG_INF)
            _softmax_step(s, vblk, m_ref, l_ref, acc_ref, m)

    def body(kk, carry):
        block(kk, False)
        return carry

    lax.fori_loop(0, qi, body, 0)
    block(qi, True)

    lam = _diff_lambda(lq1, lk1, lq2, lk2, lam_init)
    o_ref[...] = _diff_finish(acc_ref, l_ref, 0, lam, sg_ref[...], lam_init).astype(o_ref.dtype)


def _cache_out(prev, depth, rows_total, rows_block, layer):
    out_spec = pl.BlockSpec((None, rows_block, LANES), lambda *idx: (layer, idx[0], 0))
    out_shape = jax.ShapeDtypeStruct((depth, rows_total, LANES), F32)
    return out_spec, out_shape, ([] if prev is None else [prev])


def _diff_prompt_call(proj, lams, sg, prev, layer, depth, *, batch, seq, heads, dk, dv, tq, k_col, v_col,
                      lam_init):
    nq = seq // tq
    n_rows = batch * seq
    lam_spec = pl.BlockSpec((1, dk), lambda b, h, q: (0, 0))
    rows_k, rows_v = seq * 2 * heads, seq * heads * dv // LANES
    ks, kshape, kprev = _cache_out(None if prev is None else prev[0], depth, batch * rows_k, rows_k, layer)
    vs, vshape, vprev = _cache_out(None if prev is None else prev[1], depth, batch * rows_v, rows_v, layer)
    aliased = kprev + vprev
    n_in = 8
    kern = functools.partial(_diff_prompt_kernel, tq=tq, dk=dk, heads=heads, n_alias=len(aliased),
                             lam_init=lam_init, scale=dk ** -0.5)
    return pl.pallas_call(
        kern,
        grid=(batch, heads, nq),
        in_specs=[
            pl.BlockSpec((tq, 2 * dk), lambda b, h, q: (b * nq + q, h)),
            pl.BlockSpec((seq, 2 * dk), lambda b, h, q: (b, k_col // (2 * dk) + h)),
            pl.BlockSpec((seq, dv), lambda b, h, q: (b, v_col // dv + h)),
            lam_spec, lam_spec, lam_spec, lam_spec,
            pl.BlockSpec((1, dv), lambda b, h, q: (0, 0)),
        ] + [pl.BlockSpec(memory_space=pl.ANY)] * len(aliased),
        out_specs=[pl.BlockSpec((tq, dv), lambda b, h, q: (b * nq + q, h)), ks, vs],
        out_shape=[jax.ShapeDtypeStruct((n_rows, heads * dv), BF16), kshape, vshape],
        input_output_aliases={n_in + i: 1 + i for i in range(len(aliased))},
        scratch_shapes=[
            pltpu.VMEM((2, dk, seq), BF16), pltpu.VMEM((seq, dv), BF16),
            pltpu.VMEM((2, tq, LANES), F32), pltpu.VMEM((2, tq, LANES), F32), pltpu.VMEM((2, tq, dv), F32),
        ],
        compiler_params=_params(("parallel", "arbitrary", "arbitrary")),
        name="diff_prompt",
    )(proj, proj, proj, *lams, sg, *aliased)


def _fox_prompt_kernel(q_ref, k_ref, v_ref, cc_ref, cr_ref, *rest, tq, heads, n_alias, scale):
    o_ref, kc_ref, vc_ref, kb_ref, vb_ref, m_ref, l_ref, acc_ref = rest[n_alias:]
    h = pl.program_id(1)
    qi = pl.program_id(2)
    seq = k_ref.shape[0]

    @pl.when(qi == 0)
    def _():
        k = k_ref[...]
        v = v_ref[...]
        kb_ref[...] = k.T.astype(BF16)
        vb_ref[...] = v.astype(BF16)
        kc_ref[pl.ds(h, seq, stride=heads), :] = k
        vc_ref[pl.ds(h, seq, stride=heads), :] = v

    m_ref[...] = jnp.full(m_ref.shape, NEG_INF, F32)
    l_ref[...] = jnp.zeros(l_ref.shape, F32)
    acc_ref[...] = jnp.zeros(acc_ref.shape, F32)
    q = (q_ref[...] * (scale * LOG2E)).astype(BF16)
    cc = cc_ref[...]
    lane = lax.broadcasted_iota(jnp.int32, cc.shape, 1)
    cq = jnp.sum(jnp.where(lane == h, cc, 0.0), axis=-1, keepdims=True) * LOG2E

    def block(kk, causal):
        k0 = pl.multiple_of(kk * tq, tq)
        ck = cr_ref[0, pl.ds(kk, 1), :] * LOG2E
        s = jnp.dot(q, kb_ref[:, pl.ds(k0, tq)], preferred_element_type=F32) + (cq - ck)
        if causal:
            row = lax.broadcasted_iota(jnp.int32, s.shape, 0)
            col = lax.broadcasted_iota(jnp.int32, s.shape, 1)
            s = jnp.where(col <= row, s, NEG_INF)
        _softmax_step(s, vb_ref[pl.ds(k0, tq), :], m_ref, l_ref, acc_ref, 0)

    def body(kk, carry):
        block(kk, False)
        return carry

    lax.fori_loop(0, qi, body, 0)
    block(qi, True)
    o_ref[...] = (acc_ref[0] / _softmax_denominator(l_ref, 0)).astype(o_ref.dtype)


def _fox_prompt_call(proj, c_col, c_row, prev, layer, depth, *, batch, seq, heads, dh, tq, q_col, k_col, v_col):
    nq = seq // tq
    rows = seq * heads
    ks, kshape, kprev = _cache_out(None if prev is None else prev[0], depth, batch * rows, rows, layer)
    vs, vshape, vprev = _cache_out(None if prev is None else prev[1], depth, batch * rows, rows, layer)
    aliased = kprev + vprev
    n_in = 5
    kern = functools.partial(_fox_prompt_kernel, tq=tq, heads=heads, n_alias=len(aliased), scale=dh ** -0.5)
    return pl.pallas_call(
        kern,
        grid=(batch, heads, nq),
        in_specs=[
            pl.BlockSpec((tq, dh), lambda b, h, q: (b * nq + q, q_col // dh + h)),
            pl.BlockSpec((seq, dh), lambda b, h, q: (b, k_col // dh + h)),
            pl.BlockSpec((seq, dh), lambda b, h, q: (b, v_col // dh + h)),
            pl.BlockSpec((tq, LANES), lambda b, h, q: (b * nq + q, 0)),
            pl.BlockSpec((1, nq, tq), lambda b, h, q: (b * heads + h, 0, 0)),
        ] + [pl.BlockSpec(memory_space=pl.ANY)] * len(aliased),
        out_specs=[pl.BlockSpec((tq, dh), lambda b, h, q: (b * nq + q, h)), ks, vs],
        out_shape=[jax.ShapeDtypeStruct((batch * seq, heads * dh), BF16), kshape, vshape],
        input_output_aliases={n_in + i: 1 + i for i in range(len(aliased))},
        scratch_shapes=[
            pltpu.VMEM((dh, seq), BF16), pltpu.VMEM((seq, dh), BF16),
            pltpu.VMEM((1, tq, LANES), F32), pltpu.VMEM((1, tq, LANES), F32), pltpu.VMEM((1, tq, dh), F32),
        ],
        compiler_params=_params(("parallel", "arbitrary", "arbitrary")),
        name="fox_prompt",
    )(proj, proj, proj, c_col, c_row, *aliased)


def _pad_rows(x, rows):
    return jnp.concatenate([x, jnp.zeros((rows - x.shape[0], x.shape[1]), x.dtype)], axis=0)


def _causal_new(s, t_new):
    row = lax.broadcasted_iota(jnp.int32, s.shape, 0) % t_new
    col = lax.broadcasted_iota(jnp.int32, s.shape, 1)
    return jnp.where(col <= row, s, NEG_INF)


def _logf_pages_kernel(pt_ref, lf_ref, mcs_ref, mbc_ref, msel_ref, lst_ref, cp_ref, ct_ref, g_ref,
                       *, n_pages, heads, page):
    b = pl.program_id(0)
    for p in range(n_pages):
        g_ref[pl.ds(p, 1), :] = lf_ref[pl.ds(pt_ref[b, p], 1), :]
    x = g_ref[...]
    y = _dot_sel(x, mcs_ref[...])
    tot = _dot_sel(y, mbc_ref[...])
    c = y + _sel_dot(lst_ref[...], tot)
    for h in range(heads):
        cp_ref[0, h * n_pages:(h + 1) * n_pages, :] = c[:, h * page:(h + 1) * page]
    ct_ref[0] = _dot_sel(c, msel_ref[...])[n_pages - 1:n_pages, :]


def _logf_pages_call(page_table, lf2, mats, *, heads, page):
    bs, n_pages = page_table.shape
    n_pool, w = lf2.shape
    mcs, mbc, msel, lst = mats
    full = lambda a: pl.BlockSpec(a.shape, lambda b, pt: (0,) * a.ndim)
    kern = functools.partial(_logf_pages_kernel, n_pages=n_pages, heads=heads, page=page)
    grid_spec = pltpu.PrefetchScalarGridSpec(
        num_scalar_prefetch=1,
        grid=(bs,),
        in_specs=[full(lf2), full(mcs), full(mbc), full(msel), full(lst)],
        out_specs=[pl.BlockSpec((1, heads * n_pages, page), lambda b, pt: (b, 0, 0)),
                   pl.BlockSpec((1, 1, LANES), lambda b, pt: (b, 0, 0))],
        scratch_shapes=[pltpu.VMEM((n_pages, w), F32)],
    )
    return pl.pallas_call(
        kern,
        grid_spec=grid_spec,
        out_shape=[jax.ShapeDtypeStruct((bs, heads * n_pages, page), F32),
                   jax.ShapeDtypeStruct((bs, 1, LANES), F32)],
        compiler_params=_params(("arbitrary",)),
        name="logf_pages",
    )(page_table, lf2, mcs, mbc, msel, lst)


def _sample_attn_kernel(pt_ref, qd_ref, knd_ref, vnd_ref, qf_ref, knf_ref, vnf_ref, lfn_ref, *refs,
                        h_a, dk, dv, h_b, dh, page, n_pages, n_group, lam_init):
    del pt_ref
    g = n_group
    kcd, vcd, kcf, vcf = refs[:g], refs[g:2 * g], refs[2 * g:3 * g], refs[3 * g:4 * g]
    (lq1, lk1, lq2, lk2, sg_ref, cp_ref, ct_ref, od_ref, of_ref,
     md_ref, ld_ref, accd_ref, mf_ref, lf_ref, accf_ref, cn_ref) = refs[4 * g:]
    p = pl.program_id(1)
    last = pl.num_programs(1) - 1
    t_new = qd_ref.shape[0]
    n_maps = 2 * h_a
    nc = dv // LANES
    scale_d, scale_f = dk ** -0.5, dh ** -0.5

    @pl.when(p == 0)
    def _():
        for m_ref, l_ref, acc_ref in ((md_ref, ld_ref, accd_ref), (mf_ref, lf_ref, accf_ref)):
            m_ref[...] = jnp.full(m_ref.shape, NEG_INF, F32)
            l_ref[...] = jnp.zeros(l_ref.shape, F32)
            acc_ref[...] = jnp.zeros(acc_ref.shape, F32)
        x = lfn_ref[...]
        row = lax.broadcasted_iota(jnp.int32, x.shape, 0)
        k = 1
        while k < x.shape[0]:
            x = x + jnp.where(row >= k, pltpu.roll(x, k, 0), 0.0)
            k *= 2
        cn_ref[...] = ct_ref[0] + x

    qd = qd_ref[...]
    qdb = [qd[:, hm * dk:(hm + 1) * dk].astype(BF16) for hm in range(n_maps)]
    s_d = jnp.concatenate([
        jnp.concatenate([_dot_nt(qdb[hm], kc[pl.ds(hm, page, stride=n_maps), :].astype(BF16)) for kc in kcd],
                        axis=1) for hm in range(n_maps)], axis=0) * scale_d
    v_d = [jnp.concatenate([
        jnp.concatenate([vc[pl.ds(c * h_a + h, page, stride=h_a * nc), :] for c in range(nc)], axis=1)
        for vc in vcd], axis=0).astype(BF16) for h in range(h_a)]
    _softmax_step_stacked(s_d, v_d, md_ref, ld_ref, accd_ref, 2 * t_new)

    qf = qf_ref[...]
    cn = cn_ref[...]
    qfb = [qf[:, h * dh:(h + 1) * dh].astype(BF16) for h in range(h_b)]
    cq = [cn[:, h:h + 1] for h in range(h_b)]
    s_f = jnp.concatenate([
        jnp.concatenate([_dot_nt(qfb[h], kc[pl.ds(h, page, stride=h_b), :].astype(BF16)) for kc in kcf],
                        axis=1) for h in range(h_b)], axis=0) * scale_f
    bias = jnp.concatenate([
        cq[h] - jnp.concatenate([cp_ref[0, pl.ds(h * n_pages + p * g + i, 1), :] for i in range(g)], axis=1)
        for h in range(h_b)], axis=0)
    v_f = [jnp.concatenate([vc[pl.ds(h, page, stride=h_b), :] for vc in vcf], axis=0).astype(BF16)
           for h in range(h_b)]
    _softmax_step_stacked(s_f + bias, v_f, mf_ref, lf_ref, accf_ref, t_new)

    @pl.when(p == last)
    def _():
        knd, vnd = knd_ref[...], vnd_ref[...]
        sd_new = jnp.concatenate([
            _dot_nt(qdb[hm], _pad_rows(knd[:, hm * dk:(hm + 1) * dk], page).astype(BF16))
            for hm in range(n_maps)], axis=0) * scale_d
        vd_new = [_pad_rows(vnd[:, h * dv:(h + 1) * dv], page).astype(BF16) for h in range(h_a)]
        _softmax_step_stacked(_causal_new(sd_new, t_new), vd_new, md_ref, ld_ref, accd_ref, 2 * t_new)

        knf, vnf = knf_ref[...], vnf_ref[...]
        sf_new = jnp.concatenate([
            _dot_nt(qfb[h], _pad_rows(knf[:, h * dh:(h + 1) * dh], page).astype(BF16)) for h in range(h_b)],
            axis=0) * scale_f
        row = lax.broadcasted_iota(jnp.int32, (t_new, page), 0)
        col = lax.broadcasted_iota(jnp.int32, (t_new, page), 1)
        bias_new = jnp.concatenate([
            cq[h] - jnp.sum(jnp.where(row == col, cq[h], 0.0), axis=0, keepdims=True) for h in range(h_b)],
            axis=0)
        vf_new = [_pad_rows(vnf[:, h * dh:(h + 1) * dh], page).astype(BF16) for h in range(h_b)]
        _softmax_step_stacked(_causal_new(sf_new + bias_new, t_new), vf_new, mf_ref, lf_ref, accf_ref, t_new)

        lam = _diff_lambda(lq1, lk1, lq2, lk2, lam_init)
        o = accd_ref[...] / ld_ref[...]
        for h in range(h_a):
            r0 = 2 * h * t_new
            oh = o[r0:r0 + t_new, :] - lam * o[r0 + t_new:r0 + 2 * t_new, :]
            od_ref[:, h * dv:(h + 1) * dv] = _rms(oh, sg_ref[...], SUBLN_EPS) * (1.0 - lam_init)
        o = accf_ref[...] / lf_ref[...]
        for h in range(h_b):
            of_ref[:, h * dh:(h + 1) * dh] = o[h * t_new:(h + 1) * t_new, :]


def _sample_attn_call(page_table, proj, logf, caches, lams, sg, cpast, ctot, layer, *, n_prompt, t_new,
                      h_a, dk, dv, h_b, dh, page, cols, lam_init, n_group):
    bs, n_pages = page_table.shape
    wd, wf = h_a * dv, h_b * dh
    row0 = n_prompt // t_new
    kern = functools.partial(_sample_attn_kernel, h_a=h_a, dk=dk, dv=dv, h_b=h_b, dh=dh, page=page,
                             n_pages=n_pages, n_group=n_group, lam_init=lam_init)
    new_rows = lambda w, col: pl.BlockSpec((t_new, w), lambda b, p, pt: (row0 + b, col // w))
    const = lambda shape: pl.BlockSpec(shape, lambda b, p, pt: (0,) * len(shape))

    def pages(cache):
        rows = cache.shape[2]
        return [pl.BlockSpec((None, None, rows, LANES),
                             lambda b, p, pt, i=i: (layer, pt[b, p * n_group + i], 0, 0)) for i in range(n_group)]

    grid_spec = pltpu.PrefetchScalarGridSpec(
        num_scalar_prefetch=1,
        grid=(bs, n_pages // n_group),
        in_specs=[
            new_rows(wd, cols[0]), new_rows(wd, cols[1]), new_rows(wd, cols[2]),
            new_rows(wf, cols[3]), new_rows(wf, cols[4]), new_rows(wf, cols[5]),
            pl.BlockSpec((t_new, LANES), lambda b, p, pt: (row0 + b, 0)),
            *pages(caches[0]), *pages(caches[1]), *pages(caches[2]), *pages(caches[3]),
            const((1, dk)), const((1, dk)), const((1, dk)), const((1, dk)), const((1, dv)),
            pl.BlockSpec((1, h_b * n_pages, page), lambda b, p, pt: (b, 0, 0)),
            pl.BlockSpec((1, 1, LANES), lambda b, p, pt: (b, 0, 0)),
        ],
        out_specs=[pl.BlockSpec((t_new, wd), lambda b, p, pt: (b, 0)),
                   pl.BlockSpec((t_new, wf), lambda b, p, pt: (b, 0))],
        scratch_shapes=[
            pltpu.VMEM((2 * h_a * t_new, 1), F32), pltpu.VMEM((2 * h_a * t_new, 1), F32),
            pltpu.VMEM((2 * h_a * t_new, dv), F32),
            pltpu.VMEM((h_b * t_new, 1), F32), pltpu.VMEM((h_b * t_new, 1), F32),
            pltpu.VMEM((h_b * t_new, dh), F32), pltpu.VMEM((t_new, LANES), F32),
        ],
    )
    args = [proj] * 6 + [logf]
    for cache in caches:
        args += [cache] * n_group
    return pl.pallas_call(
        kern,
        grid_spec=grid_spec,
        out_shape=[jax.ShapeDtypeStruct((bs * t_new, wd), F32), jax.ShapeDtypeStruct((bs * t_new, wf), F32)],
        compiler_params=_params(("parallel", "arbitrary")),
        name="sample_attn",
    )(page_table, *args, *lams, sg, cpast, ctot)


def _lru_coeffs(x, xprev_ext, gate_w, cw_ref, cb_ref, wga_ref, bga_ref, wgx_ref, bgx_ref, lam_ref, conv_w):
    rows = x.shape[0]
    nb = wga_ref.shape[0]
    bw = wga_ref.shape[1]
    u = xprev_ext(conv_w - 1) * cw_ref[0:1, :]
    for k in range(1, conv_w - 1):
        u = u + xprev_ext(conv_w - 1 - k) * cw_ref[k:k + 1, :]
    u = u + x * cw_ref[conv_w - 1:conv_w, :]
    u = u + cb_ref[...]
    ub = u.astype(BF16)
    r_parts, i_parts = [], []
    for n in range(nb):
        un = ub[:, n * bw:(n + 1) * bw]
        r_parts.append(jnp.dot(un, wga_ref[n], preferred_element_type=F32))
        i_parts.append(jnp.dot(un, wgx_ref[n], preferred_element_type=F32))
    r = _sigmoid(jnp.concatenate(r_parts, axis=-1) + bga_ref[...])
    i = _sigmoid(jnp.concatenate(i_parts, axis=-1) + bgx_ref[...])
    log_a = -LRU_C * r * _softplus(-lam_ref[...])
    a = jnp.exp(log_a)
    b = jnp.sqrt(-jnp.tanh(log_a) * (a * a + 1.0)) * (i * u)
    del rows, gate_w
    return a, b


def _group_scan(a, b):
    row = lax.broadcasted_iota(jnp.int32, a.shape, 0) % SUBLANES
    k = 1
    while k < SUBLANES:
        keep = row >= k
        a_sh = jnp.where(keep, pltpu.roll(a, k, 0), 1.0)
        b_sh = jnp.where(keep, pltpu.roll(b, k, 0), 0.0)
        b = a * b_sh + b
        a = a * a_sh
        k *= 2
    return a, b


def _gelu_tanh(x):
    return 0.5 * x * (1.0 + jnp.tanh(math.sqrt(2.0 / math.pi) * (x + 0.044715 * (x * x * x))))


def _lru_prompt_kernel(x_ref, g_ref, cw_ref, cb_ref, wga_ref, bga_ref, wgx_ref, bgx_ref, lam_ref,
                       y_ref, ht_ref, tail_ref, hl_ref, a_ref, b_ref, hs_ref, *, conv_w):
    ti = pl.program_id(1)

    @pl.when(ti == 0)
    def _():
        tail_ref[...] = jnp.zeros(tail_ref.shape, F32)
        hl_ref[...] = jnp.zeros(hl_ref.shape, F32)

    x = x_ref[...]
    tt = x.shape[0]
    xe = jnp.concatenate([tail_ref[...], x], axis=0)
    delayed = lambda k: pltpu.roll(xe, k, 0)[SUBLANES:, :]
    a, b = _lru_coeffs(x, delayed, None, cw_ref, cb_ref, wga_ref, bga_ref, wgx_ref, bgx_ref, lam_ref, conv_w)
    a, b = _group_scan(a, b)
    a_ref[...] = a
    b_ref[...] = b
    tail_ref[...] = x[tt - SUBLANES:, :]

    def body(g, hlast):
        r0 = pl.multiple_of(g * SUBLANES, SUBLANES)
        hg = b_ref[pl.ds(r0, SUBLANES), :] + a_ref[pl.ds(r0, SUBLANES), :] * hlast
        hs_ref[pl.ds(r0, SUBLANES), :] = hg
        return hg[SUBLANES - 1:SUBLANES, :]

    hl = lax.fori_loop(0, tt // SUBLANES, body, hl_ref[...])
    hl_ref[...] = hl
    ht_ref[0] = hl
    y_ref[...] = (_gelu_tanh(g_ref[...]) * hs_ref[...]).astype(y_ref.dtype)


def _lru_prompt_call(proj, wts, *, batch, seq, wc, tt, g_col, x_col):
    cw, cb, wga, bga, wgx, bgx, lam = wts
    nt = seq // tt
    conv_w = cw.shape[0]
    full = lambda a: pl.BlockSpec(a.shape, lambda b, t: (0,) * a.ndim)
    kern = functools.partial(_lru_prompt_kernel, conv_w=conv_w)
    return pl.pallas_call(
        kern,
        grid=(batch, nt),
        in_specs=[
            pl.BlockSpec((tt, wc), lambda b, t: (b * nt + t, x_col // wc)),
            pl.BlockSpec((tt, wc), lambda b, t: (b * nt + t, g_col // wc)),
            full(cw), full(cb), full(wga), full(bga), full(wgx), full(bgx), full(lam),
        ],
        out_specs=[pl.BlockSpec((tt, wc), lambda b, t: (b * nt + t, 0)),
                   pl.BlockSpec((1, 1, wc), lambda b, t: (b, 0, 0))],
        out_shape=[jax.ShapeDtypeStruct((batch * seq, wc), BF16),
                   jax.ShapeDtypeStruct((batch, 1, wc), F32)],
        scratch_shapes=[pltpu.VMEM((SUBLANES, wc), F32), pltpu.VMEM((1, wc), F32),
                        pltpu.VMEM((tt, wc), F32), pltpu.VMEM((tt, wc), F32), pltpu.VMEM((tt, wc), F32)],
        compiler_params=_params(("parallel", "arbitrary")),
        name="lru_prompt",
    )(proj, proj, cw, cb, wga, bga, wgx, bgx, lam)


def _lru_sample_kernel(x_ref, g_ref, prev_ref, h0_ref, cw_ref, cb_ref, wga_ref, bga_ref, wgx_ref, bgx_ref,
                       lam_ref, y_ref, hs_ref, *, conv_w):
    x = x_ref[...]
    rows = x.shape[0]
    row = lax.broadcasted_iota(jnp.int32, x.shape, 0) % SUBLANES
    prev = prev_ref[...]

    def delayed(k):
        return jnp.where(row >= k, pltpu.roll(x, k, 0), pltpu.roll(prev, (k - SUBLANES) % rows, 0))

    a, b = _lru_coeffs(x, delayed, None, cw_ref, cb_ref, wga_ref, bga_ref, wgx_ref, bgx_ref, lam_ref, conv_w)
    a, b = _group_scan(a, b)
    hs = b + a * h0_ref[...]
    hs_ref[...] = hs
    y_ref[...] = _gelu_tanh(g_ref[...]) * hs


def _lru_sample_call(proj, prev8, h0rep, wts, *, n_prompt, n_sample, wc, g_col, x_col):
    cw, cb, wga, bga, wgx, bgx, lam = wts
    conv_w = cw.shape[0]
    rb = n_prompt // n_sample
    full = lambda a: pl.BlockSpec(a.shape, lambda i: (0,) * a.ndim)
    kern = functools.partial(_lru_sample_kernel, conv_w=conv_w)
    return pl.pallas_call(
        kern,
        grid=(1,),
        in_specs=[
            pl.BlockSpec((n_sample, wc), lambda i: (rb, x_col // wc)),
            pl.BlockSpec((n_sample, wc), lambda i: (rb, g_col // wc)),
            full(prev8), full(h0rep),
            full(cw), full(cb), full(wga), full(bga), full(wgx), full(bgx), full(lam),
        ],
        out_specs=[pl.BlockSpec((n_sample, wc), lambda i: (0, 0)),
                   pl.BlockSpec((n_sample, wc), lambda i: (0, 0))],
        out_shape=[jax.ShapeDtypeStruct((n_sample, wc), F32), jax.ShapeDtypeStruct((n_sample, wc), F32)],
        compiler_params=_params(("arbitrary",)),
        name="lru_sample",
    )(proj, proj, prev8, h0rep, cw, cb, wga, bga, wgx, bgx, lam)


def _mix_kernel(ap_ref, bp_ref, cp_ref, as_ref, bs_ref, cs_ref, w_ref, x_ref, g_ref, *rest,
                n_experts, n_prompt_tiles):
    if n_experts:
        wrh_ref, wrl_ref, x1_ref, rg_ref, re_ref = rest
    else:
        x1_ref, hn_ref = rest
    i = pl.program_id(0)
    wa, wb = ap_ref.shape[1], bp_ref.shape[1]

    def project(a_ref, b_ref, c_ref):
        d = lambda v, lo, hi: jnp.dot(v[...].astype(BF16), w_ref[lo:hi, :], preferred_element_type=F32)
        return d(a_ref, 0, wa) + d(b_ref, wa, wa + wb) + d(c_ref, wa + wb, w_ref.shape[0])

    @pl.when(i < n_prompt_tiles)
    def _():
        x1_ref[...] = x_ref[...] + project(ap_ref, bp_ref, cp_ref)

    @pl.when(i >= n_prompt_tiles)
    def _():
        x1_ref[...] = x_ref[...] + project(as_ref, bs_ref, cs_ref)

    hn = _rms(x1_ref[...], g_ref[...], RMS_EPS)
    if not n_experts:
        hn_ref[...] = hn.astype(BF16)
    else:
        h_hi = hn.astype(BF16)
        h_lo = (hn - h_hi.astype(F32)).astype(BF16)
        d = lambda a, b: jnp.dot(a, b[...], preferred_element_type=F32)
        logits = d(h_hi, wrh_ref) + (d(h_lo, wrh_ref) + d(h_hi, wrl_ref))
        lane = lax.broadcasted_iota(jnp.int32, logits.shape, 1)
        lane_f = lane.astype(F32)
        big = float(LANES)
        lg = jnp.where(lane < n_experts, logits, NEG_INF)
        v1 = jnp.max(lg, axis=-1, keepdims=True)
        i1 = jnp.min(jnp.where(lg == v1, lane_f, big), axis=-1, keepdims=True)
        lg2 = jnp.where(lane_f == i1, NEG_INF, lg)
        v2 = jnp.max(lg2, axis=-1, keepdims=True)
        i2 = jnp.min(jnp.where(lg2 == v2, lane_f, big), axis=-1, keepdims=True)
        e = jnp.exp(v2 - v1)
        g1 = 1.0 / (1.0 + e)
        g2 = e / (1.0 + e)
        rg_ref[...] = jnp.where(lane == 0, g1, jnp.where(lane == 1, g2, 0.0))
        re_ref[...] = jnp.where(lane == 0, i1, jnp.where(lane == 1, i2, 0.0)).astype(jnp.int32)


def _mix_call(branches_p, branches_s, w_out, x, g, w_router, *, tm):
    n, d = x.shape
    npt = branches_p[0].shape[0] // tm
    n_experts = 0 if w_router is None else w_router.shape[1]
    kern = functools.partial(_mix_kernel, n_experts=n_experts, n_prompt_tiles=npt)
    row = lambda c: pl.BlockSpec((tm, c), lambda i: (i, 0))
    row_p = lambda a: pl.BlockSpec((tm, a.shape[1]), lambda i: (jnp.minimum(i, npt - 1), 0))
    row_s = lambda a: pl.BlockSpec((tm, a.shape[1]), lambda i: (jnp.maximum(i - npt, 0), 0))
    in_specs = ([row_p(a) for a in branches_p] + [row_s(a) for a in branches_s]
                + [pl.BlockSpec((d, d), lambda i: (0, 0)), row(d), pl.BlockSpec((1, d), lambda i: (0, 0))])
    args = [*branches_p, *branches_s, w_out, x, g]
    if n_experts:
        wr = jnp.zeros((d, LANES), F32).at[:, :n_experts].set(w_router)
        wr_hi = wr.astype(BF16)
        wr_lo = (wr - wr_hi.astype(F32)).astype(BF16)
        in_specs += [pl.BlockSpec((d, LANES), lambda i: (0, 0))] * 2
        out_specs = [row(d), row(LANES), row(LANES)]
        out_shape = [jax.ShapeDtypeStruct((n, d), F32), jax.ShapeDtypeStruct((n, LANES), F32),
                     jax.ShapeDtypeStruct((n, LANES), jnp.int32)]
        args += [wr_hi, wr_lo]
    else:
        out_specs = [row(d), row(d)]
        out_shape = [jax.ShapeDtypeStruct((n, d), F32), jax.ShapeDtypeStruct((n, d), BF16)]
    return pl.pallas_call(
        kern,
        grid=(n // tm,),
        in_specs=in_specs,
        out_specs=out_specs,
        out_shape=out_shape,
        compiler_params=_params(("parallel",)),
        name="mix_router" if n_experts else "mix",
    )(*args)


def _swiglu_acc(h, wg, wu, wd):
    g = jnp.dot(h, wg.astype(BF16), preferred_element_type=F32)
    u = jnp.dot(h, wu.astype(BF16), preferred_element_type=F32)
    a = (g * _sigmoid(g) * u).astype(BF16)
    return jnp.dot(a, wd.astype(BF16), preferred_element_type=F32)


def _ffn_kernel(h_ref, wg_ref, wu_ref, wd_ref, x_ref, o_ref):
    j = pl.program_id(1)

    @pl.when(j == 0)
    def _():
        o_ref[...] = x_ref[...]

    o_ref[...] += _swiglu_acc(h_ref[...], wg_ref[...], wu_ref[...], wd_ref[...])


def _ffn_call(hn, wg, wu, wd, m, x1, *, tm, tf):
    n, d = x1.shape
    f = wg.shape[2]
    return pl.pallas_call(
        _ffn_kernel,
        grid=(n // tm, f // tf),
        in_specs=[
            pl.BlockSpec((tm, d), lambda i, j: (i, 0)),
            pl.BlockSpec((None, d, tf), lambda i, j: (m, 0, j)),
            pl.BlockSpec((None, d, tf), lambda i, j: (m, 0, j)),
            pl.BlockSpec((None, tf, d), lambda i, j: (m, j, 0)),
            pl.BlockSpec((tm, d), lambda i, j: (i, 0), pipeline_mode=pl.Buffered(1)),
        ],
        out_specs=pl.BlockSpec((tm, d), lambda i, j: (i, 0)),
        out_shape=jax.ShapeDtypeStruct((n, d), F32),
        compiler_params=_params(("parallel", "arbitrary")),
        name="ffn_dense",
    )(hn, wg, wu, wd, x1)


def _gather_norm_kernel(rt_ref, nv_ref, x_hbm, g_ref, o_ref, buf_ref, sem, *, tg):
    t = pl.program_id(0)

    def row_copy(r, tok):
        return pltpu.make_async_copy(x_hbm.at[pl.ds(tok, 1), :], buf_ref.at[pl.ds(r, 1), :], sem)

    @pl.when(t < nv_ref[0])
    def _():
        def issue(r, carry):
            row_copy(r, rt_ref[t * tg + r]).start()
            return carry

        lax.fori_loop(0, tg, issue, 0, unroll=8)
        pltpu.make_async_copy(buf_ref, buf_ref, sem).wait()
        o_ref[...] = _rms(buf_ref[...], g_ref[...], RMS_EPS).astype(BF16)

    @pl.when(t >= nv_ref[0])
    def _():
        o_ref[...] = jnp.zeros(o_ref.shape, o_ref.dtype)


def _gather_norm_call(row_token, n_valid, x1, g, *, tg):
    r = row_token.shape[0]
    d = x1.shape[1]
    grid_spec = pltpu.PrefetchScalarGridSpec(
        num_scalar_prefetch=2,
        grid=(r // tg,),
        in_specs=[pl.BlockSpec(memory_space=pl.ANY), pl.BlockSpec((1, d), lambda t, rt, nv: (0, 0))],
        out_specs=pl.BlockSpec((tg, d), lambda t, rt, nv: (t, 0)),
        scratch_shapes=[pltpu.VMEM((tg, d), F32), pltpu.SemaphoreType.DMA(())],
    )
    return pl.pallas_call(
        functools.partial(_gather_norm_kernel, tg=tg),
        grid_spec=grid_spec,
        out_shape=jax.ShapeDtypeStruct((r, d), BF16),
        compiler_params=_params(("arbitrary",)),
        name="moe_gather",
    )(row_token, n_valid, x1, g)


def _moe_ffn_kernel(te_ref, nv_ref, h_ref, wg_ref, wu_ref, wd_ref, o_ref):
    del te_ref
    t = pl.program_id(0)
    j = pl.program_id(1)

    @pl.when(j == 0)
    def _():
        o_ref[...] = jnp.zeros(o_ref.shape, F32)

    @pl.when(t < nv_ref[0])
    def _():
        o_ref[...] += _swiglu_acc(h_ref[...], wg_ref[...], wu_ref[...], wd_ref[...])


def _moe_ffn_call(tile_expert, n_valid, xs, wg, wu, wd, m, *, tm, tf):
    r, d = xs.shape
    f = wg.shape[3]
    nj = f // tf

    def tile(t, nv):
        return jnp.minimum(t, nv[0] - 1)

    def fblk(t, j, nv):
        return jnp.where(t < nv[0], j, nj - 1)

    grid_spec = pltpu.PrefetchScalarGridSpec(
        num_scalar_prefetch=2,
        grid=(r // tm, nj),
        in_specs=[
            pl.BlockSpec((tm, d), lambda t, j, te, nv: (tile(t, nv), 0)),
            pl.BlockSpec((None, None, d, tf), lambda t, j, te, nv: (m, te[t], 0, fblk(t, j, nv))),
            pl.BlockSpec((None, None, d, tf), lambda t, j, te, nv: (m, te[t], 0, fblk(t, j, nv))),
            pl.BlockSpec((None, None, tf, d), lambda t, j, te, nv: (m, te[t], fblk(t, j, nv), 0)),
        ],
        out_specs=pl.BlockSpec((tm, d), lambda t, j, te, nv: (t, 0)),
    )
    return pl.pallas_call(
        _moe_ffn_kernel,
        grid_spec=grid_spec,
        out_shape=jax.ShapeDtypeStruct((r, d), F32),
        compiler_params=_params(("arbitrary", "arbitrary")),
        name="moe_ffn",
    )(tile_expert, n_valid, xs, wg, wu, wd)


def _combine_norm_kernel(pos_ref, x_ref, rg_ref, ys_hbm, g_ref, *rest, tc, final_norm, n_prompt_tiles):
    if final_norm:
        op_ref, os_ref, buf_ref, sem = rest
    else:
        o_ref, buf_ref, sem = rest
    i = pl.program_id(0)

    def row_copy(k, r, src):
        return pltpu.make_async_copy(ys_hbm.at[pl.ds(src, 1), :], buf_ref.at[k, pl.ds(r, 1), :], sem)

    def issue(r, carry):
        for k in range(TOP_K):
            row_copy(k, r, pos_ref[(i * tc + r) * TOP_K + k]).start()
        return carry

    lax.fori_loop(0, tc, issue, 0, unroll=4)
    pltpu.make_async_copy(buf_ref, buf_ref, sem).wait()
    rg = rg_ref[...]
    moe = rg[:, 0:1] * buf_ref[0]
    for k in range(1, TOP_K):
        moe = moe + rg[:, k:k + 1] * buf_ref[k]
    x2 = x_ref[...] + moe
    if final_norm:
        y = _rms(x2, g_ref[...], RMS_EPS)

        @pl.when(i < n_prompt_tiles)
        def _():
            op_ref[...] = y

        @pl.when(i >= n_prompt_tiles)
        def _():
            os_ref[...] = y
    else:
        o_ref[...] = x2


def _combine_norm_call(pos, x1, rg, ys, g, *, tc, final_norm, n_prompt):
    n, d = x1.shape
    npt = n_prompt // tc
    if final_norm:
        out_specs = [pl.BlockSpec((tc, d), lambda i, ps: (jnp.minimum(i, npt - 1), 0)),
                     pl.BlockSpec((tc, d), lambda i, ps: (jnp.maximum(i - npt, 0), 0))]
        out_shape = [jax.ShapeDtypeStruct((n_prompt, d), F32), jax.ShapeDtypeStruct((n - n_prompt, d), F32)]
    else:
        out_specs = pl.BlockSpec((tc, d), lambda i, ps: (i, 0))
        out_shape = jax.ShapeDtypeStruct((n, d), F32)
    grid_spec = pltpu.PrefetchScalarGridSpec(
        num_scalar_prefetch=1,
        grid=(n // tc,),
        in_specs=[
            pl.BlockSpec((tc, d), lambda i, ps: (i, 0)),
            pl.BlockSpec((tc, LANES), lambda i, ps: (i, 0)),
            pl.BlockSpec(memory_space=pl.ANY),
            pl.BlockSpec((1, d), lambda i, ps: (0, 0)),
        ],
        out_specs=out_specs,
        scratch_shapes=[pltpu.VMEM((TOP_K, tc, d), F32), pltpu.SemaphoreType.DMA(())],
    )
    return pl.pallas_call(
        functools.partial(_combine_norm_kernel, tc=tc, final_norm=final_norm, n_prompt_tiles=npt),
        grid_spec=grid_spec,
        out_shape=out_shape,
        compiler_params=_params(("arbitrary",)),
        name="moe_combine",
    )(pos, x1, rg, ys, g)


def _norm_kernel(x_ref, g_ref, o_ref):
    o_ref[...] = _rms(x_ref[...], g_ref[...], RMS_EPS)


def _norm_call(x, g, *, tm):
    n, d = x.shape
    return pl.pallas_call(
        _norm_kernel,
        grid=(n // tm,),
        in_specs=[pl.BlockSpec((tm, d), lambda i: (i, 0)), pl.BlockSpec((1, d), lambda i: (0, 0))],
        out_specs=pl.BlockSpec((tm, d), lambda i: (i, 0)),
        out_shape=jax.ShapeDtypeStruct((n, d), F32),
        compiler_params=_params(("parallel",)),
        name="final_norm",
    )(x, g)


def _route_plan(re, n_experts, tm):
    n = re.shape[0]
    e_flat = re.reshape(-1)
    oh = (e_flat[:, None] == jnp.arange(n_experts, dtype=jnp.int32)[None, :]).astype(jnp.int32)
    csum = jnp.cumsum(oh, axis=0)
    rank = jnp.sum((csum - oh) * oh, axis=1)
    counts = csum[-1]
    padded = ((counts + tm - 1) // tm) * tm
    ends = jnp.cumsum(padded)
    offs = ends - padded
    pos = jnp.sum(oh * offs[None, :], axis=1) + rank
    n_rows = ((n * TOP_K + tm - 1) // tm) * tm + n_experts * tm
    row_token = jnp.zeros((n_rows,), jnp.int32).at[pos].set(jnp.arange(n * TOP_K, dtype=jnp.int32) // TOP_K)
    n_tiles = n_rows // tm
    n_valid = (ends[-1] // tm).astype(jnp.int32)
    tile_start = jnp.arange(n_tiles, dtype=jnp.int32) * tm
    tile_e = jnp.sum((tile_start[:, None] >= ends[None, :]).astype(jnp.int32), axis=1)
    last_e = jnp.sum((tile_start[n_valid - 1] >= ends).astype(jnp.int32))
    tile_e = jnp.where(jnp.arange(n_tiles) < n_valid, tile_e, last_e).astype(jnp.int32)
    return pos.astype(jnp.int32), row_token, tile_e, n_valid.reshape(1)


def _tile(pref, *dims):
    t = pref
    for dim in dims:
        t = math.gcd(t, dim)
    return t


def _rope_tables(seq, n_past, t_new, bs, dk, rot_dim):
    half = rot_dim // 2
    inv = ROPE_THETA ** (-(jnp.arange(half, dtype=F32) * 2.0) / rot_dim)
    pos = jnp.concatenate([jnp.arange(seq), jnp.tile(n_past + jnp.arange(t_new), bs)])
    ang = pos.astype(F32)[:, None] * inv[None, :]
    cos, sin = jnp.cos(ang), jnp.sin(ang)
    n = pos.shape[0]
    ones = jnp.ones((n, dk - rot_dim), F32)
    zeros_h = jnp.zeros((n, half), F32)
    zeros_r = jnp.zeros((n, dk - rot_dim), F32)
    cos_t = jnp.concatenate([cos, cos, ones], axis=1)
    sa_t = jnp.concatenate([zeros_h, sin, zeros_r], axis=1)
    sb_t = jnp.concatenate([-sin, zeros_h, zeros_r], axis=1)
    return cos_t, sa_t, sb_t


def _page_mats(heads, page, n_pages):
    w = heads * page
    dst = jnp.arange(w)
    d_head, d_slot = dst // page, dst % page
    same = d_head[:, None] == d_head[None, :]
    mcs = (same & (d_slot[:, None] <= d_slot[None, :])).astype(BF16)
    mbc = ((d_head[:, None] == d_head[None, :]) & (d_slot[:, None] == page - 1)).astype(BF16)
    msel = ((d_slot[:, None] == page - 1) & (d_head[:, None] == jnp.arange(LANES)[None, :])).astype(BF16)
    pg = jnp.arange(n_pages)
    lst = (pg[None, :] < pg[:, None]).astype(BF16)
    return mcs, mbc, msel, lst


def kernel(x_prompt, x_sample, cache_k_diff, cache_v_diff, cache_k_fox, cache_v_fox, cache_logf_fox, state_rglru_h, state_rglru_conv, page_table, norm_mix_g, w_in, b_fgate, lambda_q1, lambda_k1, lambda_q2, lambda_k2, subln_g, conv_w, conv_b, w_gate_a, b_gate_a, w_gate_x, b_gate_x, lru_lambda, w_out, norm_ffn_g, w_dense_gate, w_dense_up, w_dense_down, w_router, w_moe_gate, w_moe_up, w_moe_down, norm_final_g):
    bp, seq, d = x_prompt.shape
    bs, t_new, _ = x_sample.shape
    depth = w_in.shape[0]
    n_pool, page, h_a = cache_k_diff.shape[1], cache_k_diff.shape[2], cache_k_diff.shape[3]
    dk_a = cache_k_diff.shape[5]
    dv_a = cache_v_diff.shape[4]
    h_b, dh_b = cache_k_fox.shape[3], cache_k_fox.shape[4]
    w_c = state_rglru_h.shape[2]
    cw_len = conv_w.shape[1]
    n_pages = page_table.shape[1]
    n_past = n_pages * page
    rot_dim = dk_a // 4
    n_experts = w_router.shape[2]
    n_p, n_s = bp * seq, bs * t_new
    n = n_p + n_s
    assert t_new == SUBLANES and page == LANES and dv_a == 2 * dk_a and dk_a == LANES and dh_b == LANES

    w_qa = h_a * 2 * dk_a
    w_va = h_a * dv_a
    w_b = h_b * dh_b
    col_qa, col_ka, col_va = 0, w_qa, 2 * w_qa
    col_qb = col_va + w_va
    col_kb, col_vb = col_qb + w_b, col_qb + 2 * w_b
    col_g = col_vb + w_b
    col_x = col_g + w_c
    n_main = col_x + w_c
    src_fl = 2 * w_qa + w_va + 3 * w_b

    tm = _tile(TM_TOKENS, seq, n_s)
    tn = _tile(TN_PROJ, w_qa, src_fl, n_main - src_fl)
    tq = _tile(TQ_ATTN, seq)
    tt = _tile(TT_LRU, seq)
    tm_mix = _tile(TM_MIX, tm)
    tf = _tile(TF_FFN, w_dense_gate.shape[2])
    tm_moe = _tile(TM_MOE, n)
    tc = _tile(TC_COMBINE, n)

    rope = _rope_tables(seq, n_past, t_new, bs, dk_a, rot_dim)
    tri = jnp.tril(jnp.ones((seq, seq), BF16))
    page_mats = _page_mats(h_b, page, n_pages)

    kc_d = cache_k_diff.reshape(depth, n_pool, page * h_a * 2, dk_a)
    nc_a = dv_a // LANES
    vc_d = jnp.swapaxes(cache_v_diff.reshape(depth, n_pool, page, h_a, nc_a, LANES), 3, 4).reshape(
        depth, n_pool, page * nc_a * h_a, LANES)
    kc_f = cache_k_fox.reshape(depth, n_pool, page * h_b, dh_b)
    vc_f = cache_v_fox.reshape(depth, n_pool, page * h_b, dh_b)
    lf_pages = jnp.swapaxes(cache_logf_fox, 2, 3).reshape(depth, n_pool, h_b * page)
    n_group = _tile(PAGES_PER_STEP, n_pages)

    x = jnp.concatenate([x_prompt.reshape(n_p, d), x_sample.reshape(n_s, d)], axis=0)
    row2 = lambda v: v.reshape(1, -1).astype(F32)
    st = {k: [] for k in ('kd', 'vd', 'kf', 'vf', 'lf', 'h_p', 'h_s', 'cv')}
    y_p = y_s = None
    cache_a = cache_b = None

    for l in range(depth):
        lam_init = 0.8 - 0.6 * math.exp(-0.3 * l)
        w_l = w_in[l]
        w_b2 = w_l[:, src_fl + h_b:].astype(BF16)
        b_fl = jnp.zeros((1, LANES), F32).at[0, :h_b].set(b_fgate[l])
        proj, logf = _proj_call(
            x, row2(norm_mix_g[l]), w_in, l, src_fl, w_b2, b_fl, rope, tm=tm, tn=tn, n_rope_cols=2 * w_qa,
            n_prompt_tiles=n_p // tm, rope_prompt_tiles=seq // tm, rot_half=rot_dim // 2)

        lams = (row2(lambda_q1[l]), row2(lambda_k1[l]), row2(lambda_q2[l]), row2(lambda_k2[l]))
        sg = row2(subln_g[l])

        oa_p, *cache_a = _diff_prompt_call(proj, lams, sg, cache_a, l, depth, batch=bp, seq=seq, heads=h_a,
                                           dk=dk_a, dv=dv_a, tq=tq, k_col=col_ka, v_col=col_va,
                                           lam_init=lam_init)
        c_col = _cumsum_call(logf, tri, bp, seq)
        c_row = jnp.swapaxes(c_col[:, :h_b].reshape(bp, seq, h_b), 1, 2).reshape(bp * h_b, seq // tq, tq)
        ob_p, *cache_b = _fox_prompt_call(proj, c_col, c_row, cache_b, l, depth, batch=bp, seq=seq, heads=h_b,
                                          dh=dh_b, tq=tq, q_col=col_qb, k_col=col_kb, v_col=col_vb)
        lru_w = (conv_w[l], row2(conv_b[l]), w_gate_a[l].astype(BF16), row2(b_gate_a[l]),
                 w_gate_x[l].astype(BF16), row2(b_gate_x[l]), row2(lru_lambda[l]))
        oc_p, ht_p = _lru_prompt_call(proj, lru_w, batch=bp, seq=seq, wc=w_c, tt=tt, g_col=col_g, x_col=col_x)

        cpast, ctot = _logf_pages_call(page_table, lf_pages[l], page_mats, heads=h_b, page=page)
        oa_s, ob_s = _sample_attn_call(
            page_table, proj, logf, (kc_d, vc_d, kc_f, vc_f), lams, sg, cpast, ctot, l, n_prompt=n_p,
            t_new=t_new, h_a=h_a, dk=dk_a, dv=dv_a, h_b=h_b, dh=dh_b, page=page,
            cols=(col_qa, col_ka, col_va, col_qb, col_kb, col_vb), lam_init=lam_init, n_group=n_group)
        prev8 = jnp.pad(state_rglru_conv[l], ((0, 0), (SUBLANES - (cw_len - 1), 0), (0, 0))).reshape(n_s, w_c)
        h0rep = jnp.repeat(state_rglru_h[l], t_new, axis=0)
        oc_s, hs_s = _lru_sample_call(proj, prev8, h0rep, lru_w, n_prompt=n_p, n_sample=n_s, wc=w_c,
                                      g_col=col_g, x_col=col_x)

        br_p, br_s = (oa_p, ob_p, oc_p), (oa_s, ob_s, oc_s)
        g_ffn = row2(norm_ffn_g[l])
        m = l // 2
        if l % 2 == 0:
            x1, hn = _mix_call(br_p, br_s, w_out[l].astype(BF16), x, g_ffn, None, tm=tm_mix)
            x = _ffn_call(hn, w_dense_gate, w_dense_up, w_dense_down, m, x1, tm=tm, tf=tf)
            if l == depth - 1:
                y_all = _norm_call(x, row2(norm_final_g), tm=tm_mix)
                y_p, y_s = y_all[:n_p], y_all[n_p:]
        else:
            x1, rg, re = _mix_call(br_p, br_s, w_out[l].astype(BF16), x, g_ffn, w_router[m], tm=tm_mix)
            pos, row_token, tile_e, n_valid = _route_plan(re[:, :TOP_K], n_experts, tm_moe)
            xs = _gather_norm_call(row_token, n_valid, x1, g_ffn, tg=tm_moe)
            ys = _moe_ffn_call(tile_e, n_valid, xs, w_moe_gate, w_moe_up, w_moe_down, m, tm=tm_moe,
                               tf=_tile(TF_MOE, w_moe_gate.shape[3]))
            final = l == depth - 1
            x = _combine_norm_call(pos, x1, rg, ys, row2(norm_final_g), tc=tc, final_norm=final, n_prompt=n_p)
            if final:
                y_p, y_s = x

        st['kd'].append(proj[n_p:, col_ka:col_ka + w_qa])
        st['vd'].append(proj[n_p:, col_va:col_va + w_va])
        st['kf'].append(proj[n_p:, col_kb:col_kb + w_b])
        st['vf'].append(proj[n_p:, col_vb:col_vb + w_b])
        st['lf'].append(logf[:, :h_b])
        st['h_p'].append(ht_p.reshape(bp, w_c))
        st['h_s'].append(hs_s.reshape(bs, t_new, w_c)[:, t_new - 1])
        st['cv'].append(proj[:, col_x:col_x + w_c])

    def both(name, shape_tail):
        a = jnp.stack(st[name])
        return (a[:, :n_p].reshape((depth, bp, seq) + shape_tail),
                a[:, n_p:].reshape((depth, bs, t_new) + shape_tail))

    sample = lambda name, tail: jnp.stack(st[name]).reshape((depth, bs, t_new) + tail)
    kd_s, vd_s = sample('kd', (h_a, 2, dk_a)), sample('vd', (h_a, dv_a))
    kf_s, vf_s = sample('kf', (h_b, dh_b)), sample('vf', (h_b, dh_b))
    kd_p = cache_a[0].reshape(depth, bp, seq, h_a, 2, dk_a)
    vd_p = jnp.swapaxes(cache_a[1].reshape(depth, bp, seq, nc_a, h_a, LANES), 3, 4).reshape(
        depth, bp, seq, h_a, dv_a)
    kf_p = cache_b[0].reshape(depth, bp, seq, h_b, dh_b)
    vf_p = cache_b[1].reshape(depth, bp, seq, h_b, dh_b)
    lf_p, lf_s = both('lf', (h_b,))
    cv_p, cv_s = both('cv', (w_c,))
    keep = cw_len - 1
    return (y_p.reshape(bp, seq, d), y_s.reshape(bs, t_new, d),
            kd_p, vd_p, kf_p, vf_p, lf_p, jnp.stack(st['h_p']), cv_p[:, :, seq - keep:],
            kd_s, vd_s, kf_s, vf_s, lf_s, jnp.stack(st['h_s']), cv_s[:, :, t_new - keep:])
```

```python
import functools
import math

import jax
import jax.numpy as jnp
from jax import lax
from jax.experimental import pallas as pl
from jax.experimental.pallas import tpu as pltpu

F32 = jnp.float32
BF16 = jnp.bfloat16

LANES = 128
SUBLANES = 8
VMEM_LIMIT = 56 * 1024 * 1024

TM_TOKENS = 1024
TN_PROJ = 512
TQ_ATTN = 512
TT_LRU = 256
TM_MIX = 512
TF_FFN = 256
TF_MOE = 256
TM_MOE = 1024
TC_COMBINE = 256
PAGES_PER_STEP = 8
LOGF_BATCH = 8

RMS_EPS = 1e-6
SUBLN_EPS = 1e-5
LRU_C = 8.0
ROPE_THETA = 500000.0
TOP_K = 2
NEG_INF = float("-inf")
LOG2E = 1.4426950408889634


def _params(sem, vmem=VMEM_LIMIT):
    return pltpu.CompilerParams(dimension_semantics=sem, vmem_limit_bytes=vmem)


def _split3(x):
    hi = x.astype(BF16)
    r = x - hi.astype(F32)
    mid = r.astype(BF16)
    lo = (r - mid.astype(F32)).astype(BF16)
    return hi, mid, lo


def _dot_sel(x, m):
    hi, mid, lo = _split3(x)
    d = lambda a: jnp.dot(a, m, preferred_element_type=F32)
    return d(hi) + d(mid) + d(lo)


def _sel_dot(m, x):
    hi, mid, lo = _split3(x)
    d = lambda a: jnp.dot(m, a, preferred_element_type=F32)
    return d(hi) + d(mid) + d(lo)


def _dot_nt(a, b):
    return lax.dot_general(a, b, (((1,), (1,)), ((), ())), preferred_element_type=F32)


def _sigmoid(x):
    return 1.0 / (1.0 + jnp.exp(-x))


def _softplus(x):
    return jnp.maximum(x, 0.0) + jnp.log1p(jnp.exp(-jnp.abs(x)))


def _rms(x, g, eps):
    ms = jnp.mean(x * x, axis=-1, keepdims=True)
    return x * lax.rsqrt(ms + eps) * g


def _proj_kernel(x_ref, g_ref, wa_ref, wb_ref, wfl_ref, bf_ref, cos_ref, sa_ref, sb_ref,
                 proj_ref, logf_ref, hn_ref, *, n_rope_tiles, n_a_tiles, tn, rot_half):
    j = pl.program_id(1)

    @pl.when(j == 0)
    def _():
        hn = _rms(x_ref[...], g_ref[...], RMS_EPS).astype(BF16)
        hn_ref[...] = hn
        z = jnp.dot(hn, wfl_ref[...].astype(BF16), preferred_element_type=F32) + bf_ref[...]
        logf_ref[...] = jnp.minimum(z, 0.0) - jnp.log1p(jnp.exp(-jnp.abs(z)))

    @pl.when(j < n_rope_tiles)
    def _():
        y = jnp.dot(hn_ref[...], wa_ref[...].astype(BF16), preferred_element_type=F32)
        c, sa, sb = cos_ref[...], sa_ref[...], sb_ref[...]
        for k in range(tn // LANES):
            yc = y[:, k * LANES:(k + 1) * LANES]
            proj_ref[:, k * LANES:(k + 1) * LANES] = (
                yc * c + pltpu.roll(yc, rot_half, 1) * sa + pltpu.roll(yc, LANES - rot_half, 1) * sb)

    @pl.when((j >= n_rope_tiles) & (j < n_a_tiles))
    def _():
        proj_ref[...] = jnp.dot(hn_ref[...], wa_ref[...].astype(BF16), preferred_element_type=F32)

    @pl.when(j >= n_a_tiles)
    def _():
        proj_ref[...] = jnp.dot(hn_ref[...], wb_ref[...], preferred_element_type=F32)


def _proj_call(x, g, w_in, layer, fl_col, wb, bfl, rope, *, tm, tn, n_rope_cols, n_prompt_tiles,
               rope_prompt_tiles, rot_half):
    n, d = x.shape
    na, nb = fl_col // tn, wb.shape[1] // tn
    n_main = (na + nb) * tn
    cos_t, sa_t, sb_t = rope
    rope_map = lambda i, j: (jnp.where(i < n_prompt_tiles, i % rope_prompt_tiles,
                                       rope_prompt_tiles + i - n_prompt_tiles), 0)
    kern = functools.partial(_proj_kernel, n_rope_tiles=n_rope_cols // tn, n_a_tiles=na, tn=tn,
                             rot_half=rot_half)
    return pl.pallas_call(
        kern,
        grid=(n // tm, n_main // tn),
        in_specs=[
            pl.BlockSpec((tm, d), lambda i, j: (i, 0)),
            pl.BlockSpec((1, d), lambda i, j: (0, 0)),
            pl.BlockSpec((None, d, tn), lambda i, j: (layer, 0, jnp.minimum(j, na - 1))),
            pl.BlockSpec((d, tn), lambda i, j: (0, jnp.maximum(j - na, 0))),
            pl.BlockSpec((None, d, LANES), lambda i, j: (layer, 0, fl_col // LANES)),
            pl.BlockSpec((1, LANES), lambda i, j: (0, 0)),
            pl.BlockSpec((tm, LANES), rope_map),
            pl.BlockSpec((tm, LANES), rope_map),
            pl.BlockSpec((tm, LANES), rope_map),
        ],
        out_specs=[
            pl.BlockSpec((tm, tn), lambda i, j: (i, j)),
            pl.BlockSpec((tm, LANES), lambda i, j: (i, 0)),
        ],
        out_shape=[jax.ShapeDtypeStruct((n, n_main), F32), jax.ShapeDtypeStruct((n, LANES), F32)],
        scratch_shapes=[pltpu.VMEM((tm, d), BF16)],
        compiler_params=_params(("parallel", "arbitrary")),
        name="proj",
    )(x, g, w_in, wb, w_in, bfl, cos_t, sa_t, sb_t)


def _cumsum_kernel(x_ref, tri_ref, o_ref):
    o_ref[...] = _sel_dot(tri_ref[...], x_ref[...])


def _cumsum_call(logf_p, tri, batch, seq):
    return pl.pallas_call(
        _cumsum_kernel,
        grid=(batch,),
        in_specs=[pl.BlockSpec((seq, LANES), lambda b: (b, 0)),
                  pl.BlockSpec((seq, seq), lambda b: (0, 0))],
        out_specs=pl.BlockSpec((seq, LANES), lambda b: (b, 0)),
        out_shape=jax.ShapeDtypeStruct((batch * seq, LANES), F32),
        compiler_params=_params(("parallel",)),
        name="cumsum_prompt",
    )(logf_p, tri)


def _lane_chunks(s):
    return [s[:, c * LANES:(c + 1) * LANES] for c in range(s.shape[1] // LANES)]


def _rowmax(s):
    return jnp.max(functools.reduce(jnp.maximum, _lane_chunks(s)), axis=-1, keepdims=True)


def _rowsum(s):
    return jnp.sum(functools.reduce(jnp.add, _lane_chunks(s)), axis=-1, keepdims=True)


def _softmax_step(s, v, m_ref, l_ref, acc_ref, idx):
    m_old = m_ref[idx]
    m_new = jnp.maximum(m_old, _rowmax(s))
    alpha = jnp.exp2(m_old - m_new)
    p_chunks = [jnp.exp2(c - m_new) for c in _lane_chunks(s)]
    l_ref[idx] = alpha * l_ref[idx] + functools.reduce(jnp.add, p_chunks)
    p = jnp.concatenate(p_chunks, axis=1).astype(BF16)
    pv = jnp.dot(p, v, preferred_element_type=F32)
    acc_ref[idx] = jnp.concatenate([alpha] * (pv.shape[1] // LANES), axis=1) * acc_ref[idx] + pv
    m_ref[idx] = m_new


def _softmax_denominator(l_ref, idx):
    return jnp.sum(l_ref[idx], axis=-1, keepdims=True)


def _softmax_step_stacked(s, v_heads, m_ref, l_ref, acc_ref, rows_per_head):
    m_old = m_ref[...]
    m_new = jnp.maximum(m_old, _rowmax(s))
    p = jnp.exp(s - m_new)
    alpha = jnp.exp(m_old - m_new)
    l_ref[...] = alpha * l_ref[...] + _rowsum(p)
    pv = [jnp.dot(p[h * rows_per_head:(h + 1) * rows_per_head, :].astype(BF16), v,
                  preferred_element_type=F32) for h, v in enumerate(v_heads)]
    acc_ref[...] = alpha * acc_ref[...] + jnp.concatenate(pv, axis=0)
    m_ref[...] = m_new


def _diff_lambda(lq1, lk1, lq2, lk2, lam_init):
    return (jnp.exp(jnp.sum(lq1[...] * lk1[...], axis=-1, keepdims=True))
            - jnp.exp(jnp.sum(lq2[...] * lk2[...], axis=-1, keepdims=True)) + lam_init)


def _diff_finish(acc_ref, l_ref, base, lam, sg, lam_init):
    o = (acc_ref[base] / _softmax_denominator(l_ref, base)
         - lam * (acc_ref[base + 1] / _softmax_denominator(l_ref, base + 1)))
    return _rms(o, sg, SUBLN_EPS) * (1.0 - lam_init)


def _diff_prompt_kernel(q_ref, k_ref, v_ref, lq1, lk1, lq2, lk2, sg_ref, *rest, tq, dk, heads, n_alias,
                        lam_init, scale):
    o_ref, kc_ref, vc_ref, kb_ref, vb_ref, m_ref, l_ref, acc_ref = rest[n_alias:]
    h = pl.program_id(1)
    qi = pl.program_id(2)
    seq = k_ref.shape[0]

    @pl.when(qi == 0)
    def _():
        k = k_ref[...]
        v = v_ref[...]
        for m in range(2):
            kb_ref[m] = k[:, m * dk:(m + 1) * dk].T.astype(BF16)
            kc_ref[pl.ds(2 * h + m, seq, stride=2 * heads), :] = k[:, m * dk:(m + 1) * dk]
        vb_ref[...] = v.astype(BF16)
        nc = v.shape[1] // LANES
        for c in range(nc):
            vc_ref[pl.ds(c * heads + h, seq, stride=nc * heads), :] = v[:, c * LANES:(c + 1) * LANES]

    m_ref[...] = jnp.full(m_ref.shape, NEG_INF, F32)
    l_ref[...] = jnp.zeros(l_ref.shape, F32)
    acc_ref[...] = jnp.zeros(acc_ref.shape, F32)
    q = (q_ref[...] * (scale * LOG2E)).astype(BF16)

    def block(kk, causal):
        k0 = pl.multiple_of(kk * tq, tq)
        vblk = vb_ref[pl.ds(k0, tq), :]
        for m in range(2):
            s = jnp.dot(q[:, m * dk:(m + 1) * dk], kb_ref[m, :, pl.ds(k0, tq)],
                        preferred_element_type=F32)
            if causal:
                row = lax.broadcasted_iota(jnp.int32, s.shape, 0)
                col = lax.broadcasted_iota(jnp.int32, s.shape, 1)
                s = jnp.where(col <= row, s, NEG_INF)
            _softmax_step(s, vblk, m_ref, l_ref, acc_ref, m)

    def body(kk, carry):
        block(kk, False)
        return carry

    lax.fori_loop(0, qi, body, 0)
    block(qi, True)

    lam = _diff_lambda(lq1, lk1, lq2, lk2, lam_init)
    o_ref[...] = _diff_finish(acc_ref, l_ref, 0, lam, sg_ref[...], lam_init).astype(o_ref.dtype)


def _cache_out(prev, depth, rows_total, rows_block, layer):
    out_spec = pl.BlockSpec((None, rows_block, LANES), lambda *idx: (layer, idx[0], 0))
    out_shape = jax.ShapeDtypeStruct((depth, rows_total, LANES), F32)
    return out_spec, out_shape, ([] if prev is None else [prev])


def _diff_prompt_call(proj, lams, sg, prev, layer, depth, *, batch, seq, heads, dk, dv, tq, k_col, v_col,
                      lam_init):
    nq = seq // tq
    n_rows = batch * seq
    lam_spec = pl.BlockSpec((1, dk), lambda b, h, q: (0, 0))
    rows_k, rows_v = seq * 2 * heads, seq * heads * dv // LANES
    ks, kshape, kprev = _cache_out(None if prev is None else prev[0], depth, batch * rows_k, rows_k, layer)
    vs, vshape, vprev = _cache_out(None if prev is None else prev[1], depth, batch * rows_v, rows_v, layer)
    aliased = kprev + vprev
    n_in = 8
    kern = functools.partial(_diff_prompt_kernel, tq=tq, dk=dk, heads=heads, n_alias=len(aliased),
                             lam_init=lam_init, scale=dk ** -0.5)
    return pl.pallas_call(
        kern,
        grid=(batch, heads, nq),
        in_specs=[
            pl.BlockSpec((tq, 2 * dk), lambda b, h, q: (b * nq + q, h)),
            pl.BlockSpec((seq, 2 * dk), lambda b, h, q: (b, k_col // (2 * dk) + h)),
            pl.BlockSpec((seq, dv), lambda b, h, q: (b, v_col // dv + h)),
            lam_spec, lam_spec, lam_spec, lam_spec,
            pl.BlockSpec((1, dv), lambda b, h, q: (0, 0)),
        ] + [pl.BlockSpec(memory_space=pl.ANY)] * len(aliased),
        out_specs=[pl.BlockSpec((tq, dv), lambda b, h, q: (b * nq + q, h)), ks, vs],
        out_shape=[jax.ShapeDtypeStruct((n_rows, heads * dv), BF16), kshape, vshape],
        input_output_aliases={n_in + i: 1 + i for i in range(len(aliased))},
        scratch_shapes=[
            pltpu.VMEM((2, dk, seq), BF16), pltpu.VMEM((seq, dv), BF16),
            pltpu.VMEM((2, tq, LANES), F32), pltpu.VMEM((2, tq, LANES), F32), pltpu.VMEM((2, tq, dv), F32),
        ],
        compiler_params=_params(("parallel", "arbitrary", "arbitrary")),
        name="diff_prompt",
    )(proj, proj, proj, *lams, sg, *aliased)


def _fox_prompt_kernel(q_ref, k_ref, v_ref, cc_ref, cr_ref, *rest, tq, heads, n_alias, scale):
    o_ref, kc_ref, vc_ref, kb_ref, vb_ref, m_ref, l_ref, acc_ref = rest[n_alias:]
    h = pl.program_id(1)
    qi = pl.program_id(2)
    seq = k_ref.shape[0]

    @pl.when(qi == 0)
    def _():
        k = k_ref[...]
        v = v_ref[...]
        kb_ref[...] = k.T.astype(BF16)
        vb_ref[...] = v.astype(BF16)
        kc_ref[pl.ds(h, seq, stride=heads), :] = k
        vc_ref[pl.ds(h, seq, stride=heads), :] = v

    m_ref[...] = jnp.full(m_ref.shape, NEG_INF, F32)
    l_ref[...] = jnp.zeros(l_ref.shape, F32)
    acc_ref[...] = jnp.zeros(acc_ref.shape, F32)
    q = (q_ref[...] * (scale * LOG2E)).astype(BF16)
    cc = cc_ref[...]
    lane = lax.broadcasted_iota(jnp.int32, cc.shape, 1)
    cq = jnp.sum(jnp.where(lane == h, cc, 0.0), axis=-1, keepdims=True) * LOG2E

    def block(kk, causal):
        k0 = pl.multiple_of(kk * tq, tq)
        ck = cr_ref[0, pl.ds(kk, 1), :] * LOG2E
        s = jnp.dot(q, kb_ref[:, pl.ds(k0, tq)], preferred_element_type=F32) + (cq - ck)
        if causal:
            row = lax.broadcasted_iota(jnp.int32, s.shape, 0)
            col = lax.broadcasted_iota(jnp.int32, s.shape, 1)
            s = jnp.where(col <= row, s, NEG_INF)
        _softmax_step(s, vb_ref[pl.ds(k0, tq), :], m_ref, l_ref, acc_ref, 0)

    def body(kk, carry):
        block(kk, False)
        return carry

    lax.fori_loop(0, qi, body, 0)
    block(qi, True)
    o_ref[...] = (acc_ref[0] / _softmax_denominator(l_ref, 0)).astype(o_ref.dtype)


def _fox_prompt_call(proj, c_col, c_row, prev, layer, depth, *, batch, seq, heads, dh, tq, q_col, k_col, v_col):
    nq = seq // tq
    rows = seq * heads
    ks, kshape, kprev = _cache_out(None if prev is None else prev[0], depth, batch * rows, rows, layer)
    vs, vshape, vprev = _cache_out(None if prev is None else prev[1], depth, batch * rows, rows, layer)
    aliased = kprev + vprev
    n_in = 5
    kern = functools.partial(_fox_prompt_kernel, tq=tq, heads=heads, n_alias=len(aliased), scale=dh ** -0.5)
    return pl.pallas_call(
        kern,
        grid=(batch, heads, nq),
        in_specs=[
            pl.BlockSpec((tq, dh), lambda b, h, q: (b * nq + q, q_col // dh + h)),
            pl.BlockSpec((seq, dh), lambda b, h, q: (b, k_col // dh + h)),
            pl.BlockSpec((seq, dh), lambda b, h, q: (b, v_col // dh + h)),
            pl.BlockSpec((tq, LANES), lambda b, h, q: (b * nq + q, 0)),
            pl.BlockSpec((1, nq, tq), lambda b, h, q: (b * heads + h, 0, 0)),
        ] + [pl.BlockSpec(memory_space=pl.ANY)] * len(aliased),
        out_specs=[pl.BlockSpec((tq, dh), lambda b, h, q: (b * nq + q, h)), ks, vs],
        out_shape=[jax.ShapeDtypeStruct((batch * seq, heads * dh), BF16), kshape, vshape],
        input_output_aliases={n_in + i: 1 + i for i in range(len(aliased))},
        scratch_shapes=[
            pltpu.VMEM((dh, seq), BF16), pltpu.VMEM((seq, dh), BF16),
            pltpu.VMEM((1, tq, LANES), F32), pltpu.VMEM((1, tq, LANES), F32), pltpu.VMEM((1, tq, dh), F32),
        ],
        compiler_params=_params(("parallel", "arbitrary", "arbitrary")),
        name="fox_prompt",
    )(proj, proj, proj, c_col, c_row, *aliased)


def _pad_rows(x, rows):
    return jnp.concatenate([x, jnp.zeros((rows - x.shape[0], x.shape[1]), x.dtype)], axis=0)


def _causal_new(s, t_new):
    row = lax.broadcasted_iota(jnp.int32, s.shape, 0) % t_new
    col = lax.broadcasted_iota(jnp.int32, s.shape, 1)
    return jnp.where(col <= row, s, NEG_INF)


def _logf_pages_kernel(pt_ref, lf_ref, mcs_ref, mbc_ref, msel_ref, lst_ref, cp_ref, ct_ref, g_ref,
                       *, n_pages, heads, page):
    nb = lst_ref.shape[0] // n_pages
    b0 = pl.program_id(0) * nb
    for i in range(nb):
        for p in range(n_pages):
            g_ref[pl.ds(i * n_pages + p, 1), :] = lf_ref[pl.ds(pt_ref[b0 + i, p], 1), :]
    x = g_ref[...]
    y = _dot_sel(x, mcs_ref[...])
    tot = _dot_sel(y, mbc_ref[...])
    c = y + _sel_dot(lst_ref[...], tot)
    last = _dot_sel(c, msel_ref[...])
    for i in range(nb):
        r0 = i * n_pages
        for h in range(heads):
            cp_ref[i, h * n_pages:(h + 1) * n_pages, :] = c[r0:r0 + n_pages, h * page:(h + 1) * page]
        ct_ref[i] = last[r0 + n_pages - 1:r0 + n_pages, :]


def _logf_pages_call(page_table, lf2, mats, *, heads, page):
    bs, n_pages = page_table.shape
    n_pool, w = lf2.shape
    mcs, mbc, msel, lst = mats
    nb = lst.shape[0] // n_pages
    full = lambda a: pl.BlockSpec(a.shape, lambda b, pt: (0,) * a.ndim)
    kern = functools.partial(_logf_pages_kernel, n_pages=n_pages, heads=heads, page=page)
    grid_spec = pltpu.PrefetchScalarGridSpec(
        num_scalar_prefetch=1,
        grid=(bs // nb,),
        in_specs=[full(lf2), full(mcs), full(mbc), full(msel), full(lst)],
        out_specs=[pl.BlockSpec((nb, heads * n_pages, page), lambda b, pt: (b, 0, 0)),
                   pl.BlockSpec((nb, 1, LANES), lambda b, pt: (b, 0, 0))],
        scratch_shapes=[pltpu.VMEM((nb * n_pages, w), F32)],
    )
    return pl.pallas_call(
        kern,
        grid_spec=grid_spec,
        out_shape=[jax.ShapeDtypeStruct((bs, heads * n_pages, page), F32),
                   jax.ShapeDtypeStruct((bs, 1, LANES), F32)],
        compiler_params=_params(("arbitrary",)),
        name="logf_pages",
    )(page_table, lf2, mcs, mbc, msel, lst)


def _sample_attn_kernel(pt_ref, qd_ref, knd_ref, vnd_ref, qf_ref, knf_ref, vnf_ref, lfn_ref, *refs,
                        h_a, dk, dv, h_b, dh, page, n_pages, n_group, lam_init):
    del pt_ref
    g = n_group
    kcd, vcd, kcf, vcf = refs[:g], refs[g:2 * g], refs[2 * g:3 * g], refs[3 * g:4 * g]
    (lq1, lk1, lq2, lk2, sg_ref, cp_ref, ct_ref, od_ref, of_ref,
     md_ref, ld_ref, accd_ref, mf_ref, lf_ref, accf_ref, cn_ref) = refs[4 * g:]
    p = pl.program_id(1)
    last = pl.num_programs(1) - 1
    t_new = qd_ref.shape[0]
    n_maps = 2 * h_a
    nc = dv // LANES
    scale_d, scale_f = dk ** -0.5, dh ** -0.5

    @pl.when(p == 0)
    def _():
        for m_ref, l_ref, acc_ref in ((md_ref, ld_ref, accd_ref), (mf_ref, lf_ref, accf_ref)):
            m_ref[...] = jnp.full(m_ref.shape, NEG_INF, F32)
            l_ref[...] = jnp.zeros(l_ref.shape, F32)
            acc_ref[...] = jnp.zeros(acc_ref.shape, F32)
        x = lfn_ref[...]
        row = lax.broadcasted_iota(jnp.int32, x.shape, 0)
        k = 1
        while k < x.shape[0]:
            x = x + jnp.where(row >= k, pltpu.roll(x, k, 0), 0.0)
            k *= 2
        cn_ref[...] = ct_ref[0] + x

    qd = qd_ref[...]
    qdb = [qd[:, hm * dk:(hm + 1) * dk].astype(BF16) for hm in range(n_maps)]
    s_d = jnp.concatenate([
        jnp.concatenate([_dot_nt(qdb[hm], kc[pl.ds(hm, page, stride=n_maps), :].astype(BF16)) for kc in kcd],
                        axis=1) for hm in range(n_maps)], axis=0) * scale_d
    v_d = [jnp.concatenate([
        jnp.concatenate([vc[pl.ds(c * h_a + h, page, stride=h_a * nc), :] for c in range(nc)], axis=1)
        for vc in vcd], axis=0).astype(BF16) for h in range(h_a)]
    _softmax_step_stacked(s_d, v_d, md_ref, ld_ref, accd_ref, 2 * t_new)

    qf = qf_ref[...]
    cn = cn_ref[...]
    qfb = [qf[:, h * dh:(h + 1) * dh].astype(BF16) for h in range(h_b)]
    cq = [cn[:, h:h + 1] for h in range(h_b)]
    s_f = jnp.concatenate([
        jnp.concatenate([_dot_nt(qfb[h], kc[pl.ds(h, page, stride=h_b), :].astype(BF16)) for kc in kcf],
                        axis=1) for h in range(h_b)], axis=0) * scale_f
    bias = jnp.concatenate([
        cq[h] - jnp.concatenate([cp_ref[0, pl.ds(h * n_pages + p * g + i, 1), :] for i in range(g)], axis=1)
        for h in range(h_b)], axis=0)
    v_f = [jnp.concatenate([vc[pl.ds(h, page, stride=h_b), :] for vc in vcf], axis=0).astype(BF16)
           for h in range(h_b)]
    _softmax_step_stacked(s_f + bias, v_f, mf_ref, lf_ref, accf_ref, t_new)

    @pl.when(p == last)
    def _():
        knd, vnd = knd_ref[...], vnd_ref[...]
        sd_new = jnp.concatenate([
            _dot_nt(qdb[hm], _pad_rows(knd[:, hm * dk:(hm + 1) * dk], page).astype(BF16))
            for hm in range(n_maps)], axis=0) * scale_d
        vd_new = [_pad_rows(vnd[:, h * dv:(h + 1) * dv], page).astype(BF16) for h in range(h_a)]
        _softmax_step_stacked(_causal_new(sd_new, t_new), vd_new, md_ref, ld_ref, accd_ref, 2 * t_new)

        knf, vnf = knf_ref[...], vnf_ref[...]
        sf_new = jnp.concatenate([
            _dot_nt(qfb[h], _pad_rows(knf[:, h * dh:(h + 1) * dh], page).astype(BF16)) for h in range(h_b)],
            axis=0) * scale_f
        row = lax.broadcasted_iota(jnp.int32, (t_new, page), 0)
        col = lax.broadcasted_iota(jnp.int32, (t_new, page), 1)
        bias_new = jnp.concatenate([
            cq[h] - jnp.sum(jnp.where(row == col, cq[h], 0.0), axis=0, keepdims=True) for h in range(h_b)],
            axis=0)
        vf_new = [_pad_rows(vnf[:, h * dh:(h + 1) * dh], page).astype(BF16) for h in range(h_b)]
        _softmax_step_stacked(_causal_new(sf_new + bias_new, t_new), vf_new, mf_ref, lf_ref, accf_ref, t_new)

        lam = _diff_lambda(lq1, lk1, lq2, lk2, lam_init)
        o = accd_ref[...] / ld_ref[...]
        for h in range(h_a):
            r0 = 2 * h * t_new
            oh = o[r0:r0 + t_new, :] - lam * o[r0 + t_new:r0 + 2 * t_new, :]
            od_ref[:, h * dv:(h + 1) * dv] = _rms(oh, sg_ref[...], SUBLN_EPS) * (1.0 - lam_init)
        o = accf_ref[...] / lf_ref[...]
        for h in range(h_b):
            of_ref[:, h * dh:(h + 1) * dh] = o[h * t_new:(h + 1) * t_new, :]


def _sample_attn_call(page_table, proj, logf, caches, lams, sg, cpast, ctot, layer, *, n_prompt, t_new,
                      h_a, dk, dv, h_b, dh, page, cols, lam_init, n_group):
    bs, n_pages = page_table.shape
    wd, wf = h_a * dv, h_b * dh
    row0 = n_prompt // t_new
    kern = functools.partial(_sample_attn_kernel, h_a=h_a, dk=dk, dv=dv, h_b=h_b, dh=dh, page=page,
                             n_pages=n_pages, n_group=n_group, lam_init=lam_init)
    new_rows = lambda w, col: pl.BlockSpec((t_new, w), lambda b, p, pt: (row0 + b, col // w))
    const = lambda shape: pl.BlockSpec(shape, lambda b, p, pt: (0,) * len(shape))

    def pages(cache):
        rows = cache.shape[2]
        return [pl.BlockSpec((None, None, rows, LANES),
                             lambda b, p, pt, i=i: (layer, pt[b, p * n_group + i], 0, 0)) for i in range(n_group)]

    grid_spec = pltpu.PrefetchScalarGridSpec(
        num_scalar_prefetch=1,
        grid=(bs, n_pages // n_group),
        in_specs=[
            new_rows(wd, cols[0]), new_rows(wd, cols[1]), new_rows(wd, cols[2]),
            new_rows(wf, cols[3]), new_rows(wf, cols[4]), new_rows(wf, cols[5]),
            pl.BlockSpec((t_new, LANES), lambda b, p, pt: (row0 + b, 0)),
            *pages(caches[0]), *pages(caches[1]), *pages(caches[2]), *pages(caches[3]),
            const((1, dk)), const((1, dk)), const((1, dk)), const((1, dk)), const((1, dv)),
            pl.BlockSpec((1, h_b * n_pages, page), lambda b, p, pt: (b, 0, 0)),
            pl.BlockSpec((1, 1, LANES), lambda b, p, pt: (b, 0, 0)),
        ],
        out_specs=[pl.BlockSpec((t_new, wd), lambda b, p, pt: (b, 0)),
                   pl.BlockSpec((t_new, wf), lambda b, p, pt: (b, 0))],
        scratch_shapes=[
            pltpu.VMEM((2 * h_a * t_new, 1), F32), pltpu.VMEM((2 * h_a * t_new, 1), F32),
            pltpu.VMEM((2 * h_a * t_new, dv), F32),
            pltpu.VMEM((h_b * t_new, 1), F32), pltpu.VMEM((h_b * t_new, 1), F32),
            pltpu.VMEM((h_b * t_new, dh), F32), pltpu.VMEM((t_new, LANES), F32),
        ],
    )
    args = [proj] * 6 + [logf]
    for cache in caches:
        args += [cache] * n_group
    return pl.pallas_call(
        kern,
        grid_spec=grid_spec,
        out_shape=[jax.ShapeDtypeStruct((bs * t_new, wd), F32), jax.ShapeDtypeStruct((bs * t_new, wf), F32)],
        compiler_params=_params(("parallel", "arbitrary")),
        name="sample_attn",
    )(page_table, *args, *lams, sg, cpast, ctot)


def _lru_coeffs(x, xprev_ext, gate_w, cw_ref, cb_ref, wga_ref, bga_ref, wgx_ref, bgx_ref, lam_ref, conv_w):
    rows = x.shape[0]
    nb = wga_ref.shape[0]
    bw = wga_ref.shape[1]
    u = xprev_ext(conv_w - 1) * cw_ref[0:1, :]
    for k in range(1, conv_w - 1):
        u = u + xprev_ext(conv_w - 1 - k) * cw_ref[k:k + 1, :]
    u = u + x * cw_ref[conv_w - 1:conv_w, :]
    u = u + cb_ref[...]
    ub = u.astype(BF16)
    r_parts, i_parts = [], []
    for n in range(nb):
        un = ub[:, n * bw:(n + 1) * bw]
        r_parts.append(jnp.dot(un, wga_ref[n], preferred_element_type=F32))
        i_parts.append(jnp.dot(un, wgx_ref[n], preferred_element_type=F32))
    r = _sigmoid(jnp.concatenate(r_parts, axis=-1) + bga_ref[...])
    i = _sigmoid(jnp.concatenate(i_parts, axis=-1) + bgx_ref[...])
    log_a = -LRU_C * r * _softplus(-lam_ref[...])
    a = jnp.exp(log_a)
    b = jnp.sqrt(-jnp.tanh(log_a) * (a * a + 1.0)) * (i * u)
    del rows, gate_w
    return a, b


def _group_scan(a, b):
    row = lax.broadcasted_iota(jnp.int32, a.shape, 0) % SUBLANES
    k = 1
    while k < SUBLANES:
        keep = row >= k
        a_sh = jnp.where(keep, pltpu.roll(a, k, 0), 1.0)
        b_sh = jnp.where(keep, pltpu.roll(b, k, 0), 0.0)
        b = a * b_sh + b
        a = a * a_sh
        k *= 2
    return a, b


def _gelu_tanh(x):
    return 0.5 * x * (1.0 + jnp.tanh(math.sqrt(2.0 / math.pi) * (x + 0.044715 * (x * x * x))))


def _lru_prompt_kernel(x_ref, g_ref, cw_ref, cb_ref, wga_ref, bga_ref, wgx_ref, bgx_ref, lam_ref,
                       y_ref, ht_ref, tail_ref, hl_ref, a_ref, b_ref, hs_ref, *, conv_w):
    ti = pl.program_id(1)

    @pl.when(ti == 0)
    def _():
        tail_ref[...] = jnp.zeros(tail_ref.shape, F32)
        hl_ref[...] = jnp.zeros(hl_ref.shape, F32)

    x = x_ref[...]
    tt = x.shape[0]
    xe = jnp.concatenate([tail_ref[...], x], axis=0)
    delayed = lambda k: pltpu.roll(xe, k, 0)[SUBLANES:, :]
    a, b = _lru_coeffs(x, delayed, None, cw_ref, cb_ref, wga_ref, bga_ref, wgx_ref, bgx_ref, lam_ref, conv_w)
    a, b = _group_scan(a, b)
    a_ref[...] = a
    b_ref[...] = b
    tail_ref[...] = x[tt - SUBLANES:, :]

    def body(g, hlast):
        r0 = pl.multiple_of(g * SUBLANES, SUBLANES)
        hg = b_ref[pl.ds(r0, SUBLANES), :] + a_ref[pl.ds(r0, SUBLANES), :] * hlast
        hs_ref[pl.ds(r0, SUBLANES), :] = hg
        return hg[SUBLANES - 1:SUBLANES, :]

    hl = lax.fori_loop(0, tt // SUBLANES, body, hl_ref[...])
    hl_ref[...] = hl
    ht_ref[0] = hl
    y_ref[...] = (_gelu_tanh(g_ref[...]) * hs_ref[...]).astype(y_ref.dtype)


def _lru_prompt_call(proj, wts, *, batch, seq, wc, tt, g_col, x_col):
    cw, cb, wga, bga, wgx, bgx, lam = wts
    nt = seq // tt
    conv_w = cw.shape[0]
    full = lambda a: pl.BlockSpec(a.shape, lambda b, t: (0,) * a.ndim)
    kern = functools.partial(_lru_prompt_kernel, conv_w=conv_w)
    return pl.pallas_call(
        kern,
        grid=(batch, nt),
        in_specs=[
            pl.BlockSpec((tt, wc), lambda b, t: (b * nt + t, x_col // wc)),
            pl.BlockSpec((tt, wc), lambda b, t: (b * nt + t, g_col // wc)),
            full(cw), full(cb), full(wga), full(bga), full(wgx), full(bgx), full(lam),
        ],
        out_specs=[pl.BlockSpec((tt, wc), lambda b, t: (b * nt + t, 0)),
                   pl.BlockSpec((1, 1, wc), lambda b, t: (b, 0, 0))],
        out_shape=[jax.ShapeDtypeStruct((batch * seq, wc), BF16),
                   jax.ShapeDtypeStruct((batch, 1, wc), F32)],
        scratch_shapes=[pltpu.VMEM((SUBLANES, wc), F32), pltpu.VMEM((1, wc), F32),
                        pltpu.VMEM((tt, wc), F32), pltpu.VMEM((tt, wc), F32), pltpu.VMEM((tt, wc), F32)],
        compiler_params=_params(("parallel", "arbitrary")),
        name="lru_prompt",
    )(proj, proj, cw, cb, wga, bga, wgx, bgx, lam)


def _lru_sample_kernel(x_ref, g_ref, prev_ref, h0_ref, cw_ref, cb_ref, wga_ref, bga_ref, wgx_ref, bgx_ref,
                       lam_ref, y_ref, hs_ref, *, conv_w):
    x = x_ref[...]
    rows = x.shape[0]
    row = lax.broadcasted_iota(jnp.int32, x.shape, 0) % SUBLANES
    prev = prev_ref[...]

    def delayed(k):
        return jnp.where(row >= k, pltpu.roll(x, k, 0), pltpu.roll(prev, (k - SUBLANES) % rows, 0))

    a, b = _lru_coeffs(x, delayed, None, cw_ref, cb_ref, wga_ref, bga_ref, wgx_ref, bgx_ref, lam_ref, conv_w)
    a, b = _group_scan(a, b)
    hs = b + a * h0_ref[...]
    hs_ref[...] = hs
    y_ref[...] = _gelu_tanh(g_ref[...]) * hs


def _lru_sample_call(proj, prev8, h0rep, wts, *, n_prompt, n_sample, wc, g_col, x_col):
    cw, cb, wga, bga, wgx, bgx, lam = wts
    conv_w = cw.shape[0]
    rb = n_prompt // n_sample
    full = lambda a: pl.BlockSpec(a.shape, lambda i: (0,) * a.ndim)
    kern = functools.partial(_lru_sample_kernel, conv_w=conv_w)
    return pl.pallas_call(
        kern,
        grid=(1,),
        in_specs=[
            pl.BlockSpec((n_sample, wc), lambda i: (rb, x_col // wc)),
            pl.BlockSpec((n_sample, wc), lambda i: (rb, g_col // wc)),
            full(prev8), full(h0rep),
            full(cw), full(cb), full(wga), full(bga), full(wgx), full(bgx), full(lam),
        ],
        out_specs=[pl.BlockSpec((n_sample, wc), lambda i: (0, 0)),
                   pl.BlockSpec((n_sample, wc), lambda i: (0, 0))],
        out_shape=[jax.ShapeDtypeStruct((n_sample, wc), F32), jax.ShapeDtypeStruct((n_sample, wc), F32)],
        compiler_params=_params(("arbitrary",)),
        name="lru_sample",
    )(proj, proj, prev8, h0rep, cw, cb, wga, bga, wgx, bgx, lam)


def _mix_kernel(ap_ref, bp_ref, cp_ref, as_ref, bs_ref, cs_ref, w_ref, x_ref, g_ref, *rest,
                n_experts, n_prompt_tiles):
    if n_experts:
        wrh_ref, wrl_ref, x1_ref, rg_ref, re_ref = rest
    else:
        x1_ref, hn_ref = rest
    i = pl.program_id(0)
    wa, wb = ap_ref.shape[1], bp_ref.shape[1]

    def project(a_ref, b_ref, c_ref):
        d = lambda v, lo, hi: jnp.dot(v[...].astype(BF16), w_ref[lo:hi, :], preferred_element_type=F32)
        return d(a_ref, 0, wa) + d(b_ref, wa, wa + wb) + d(c_ref, wa + wb, w_ref.shape[0])

    @pl.when(i < n_prompt_tiles)
    def _():
        x1_ref[...] = x_ref[...] + project(ap_ref, bp_ref, cp_ref)

    @pl.when(i >= n_prompt_tiles)
    def _():
        x1_ref[...] = x_ref[...] + project(as_ref, bs_ref, cs_ref)

    hn = _rms(x1_ref[...], g_ref[...], RMS_EPS)
    if not n_experts:
        hn_ref[...] = hn.astype(BF16)
    else:
        h_hi = hn.astype(BF16)
        h_lo = (hn - h_hi.astype(F32)).astype(BF16)
        d = lambda a, b: jnp.dot(a, b[...], preferred_element_type=F32)
        logits = d(h_hi, wrh_ref) + (d(h_lo, wrh_ref) + d(h_hi, wrl_ref))
        lane = lax.broadcasted_iota(jnp.int32, logits.shape, 1)
        lane_f = lane.astype(F32)
        big = float(LANES)
        lg = jnp.where(lane < n_experts, logits, NEG_INF)
        v1 = jnp.max(lg, axis=-1, keepdims=True)
        i1 = jnp.min(jnp.where(lg == v1, lane_f, big), axis=-1, keepdims=True)
        lg2 = jnp.where(lane_f == i1, NEG_INF, lg)
        v2 = jnp.max(lg2, axis=-1, keepdims=True)
        i2 = jnp.min(jnp.where(lg2 == v2, lane_f, big), axis=-1, keepdims=True)
        e = jnp.exp(v2 - v1)
        g1 = 1.0 / (1.0 + e)
        g2 = e / (1.0 + e)
        rg_ref[...] = jnp.where(lane == 0, g1, jnp.where(lane == 1, g2, 0.0))
        re_ref[...] = jnp.where(lane == 0, i1, jnp.where(lane == 1, i2, 0.0)).astype(jnp.int32)


def _mix_call(branches_p, branches_s, w_out, x, g, w_router, *, tm):
    n, d = x.shape
    npt = branches_p[0].shape[0] // tm
    n_experts = 0 if w_router is None else w_router.shape[1]
    kern = functools.partial(_mix_kernel, n_experts=n_experts, n_prompt_tiles=npt)
    row = lambda c: pl.BlockSpec((tm, c), lambda i: (i, 0))
    row_p = lambda a: pl.BlockSpec((tm, a.shape[1]), lambda i: (jnp.minimum(i, npt - 1), 0))
    row_s = lambda a: pl.BlockSpec((tm, a.shape[1]), lambda i: (jnp.maximum(i - npt, 0), 0))
    in_specs = ([row_p(a) for a in branches_p] + [row_s(a) for a in branches_s]
                + [pl.BlockSpec((d, d), lambda i: (0, 0)), row(d), pl.BlockSpec((1, d), lambda i: (0, 0))])
    args = [*branches_p, *branches_s, w_out, x, g]
    if n_experts:
        wr = jnp.zeros((d, LANES), F32).at[:, :n_experts].set(w_router)
        wr_hi = wr.astype(BF16)
        wr_lo = (wr - wr_hi.astype(F32)).astype(BF16)
        in_specs += [pl.BlockSpec((d, LANES), lambda i: (0, 0))] * 2
        out_specs = [row(d), row(LANES), row(LANES)]
        out_shape = [jax.ShapeDtypeStruct((n, d), F32), jax.ShapeDtypeStruct((n, LANES), F32),
                     jax.ShapeDtypeStruct((n, LANES), jnp.int32)]
        args += [wr_hi, wr_lo]
    else:
        out_specs = [row(d), row(d)]
        out_shape = [jax.ShapeDtypeStruct((n, d), F32), jax.ShapeDtypeStruct((n, d), BF16)]
    return pl.pallas_call(
        kern,
        grid=(n // tm,),
        in_specs=in_specs,
        out_specs=out_specs,
        out_shape=out_shape,
        compiler_params=_params(("parallel",)),
        name="mix_router" if n_experts else "mix",
    )(*args)


def _swiglu_acc(h, wg, wu, wd):
    g = jnp.dot(h, wg.astype(BF16), preferred_element_type=F32)
    u = jnp.dot(h, wu.astype(BF16), preferred_element_type=F32)
    a = (g * _sigmoid(g) * u).astype(BF16)
    return jnp.dot(a, wd.astype(BF16), preferred_element_type=F32)


def _ffn_kernel(h_ref, wg_ref, wu_ref, wd_ref, x_ref, o_ref):
    j = pl.program_id(1)

    @pl.when(j == 0)
    def _():
        o_ref[...] = x_ref[...]

    o_ref[...] += _swiglu_acc(h_ref[...], wg_ref[...], wu_ref[...], wd_ref[...])


def _ffn_call(hn, wg, wu, wd, m, x1, *, tm, tf):
    n, d = x1.shape
    f = wg.shape[2]
    return pl.pallas_call(
        _ffn_kernel,
        grid=(n // tm, f // tf),
        in_specs=[
            pl.BlockSpec((tm, d), lambda i, j: (i, 0)),
            pl.BlockSpec((None, d, tf), lambda i, j: (m, 0, j)),
            pl.BlockSpec((None, d, tf), lambda i, j: (m, 0, j)),
            pl.BlockSpec((None, tf, d), lambda i, j: (m, j, 0)),
            pl.BlockSpec((tm, d), lambda i, j: (i, 0), pipeline_mode=pl.Buffered(1)),
        ],
        out_specs=pl.BlockSpec((tm, d), lambda i, j: (i, 0)),
        out_shape=jax.ShapeDtypeStruct((n, d), F32),
        compiler_params=_params(("parallel", "arbitrary")),
        name="ffn_dense",
    )(hn, wg, wu, wd, x1)


def _gather_norm_kernel(rt_ref, nv_ref, x_hbm, g_ref, o_ref, buf_ref, sem, *, tg):
    t = pl.program_id(0)

    def row_copy(r, tok):
        return pltpu.make_async_copy(x_hbm.at[pl.ds(tok, 1), :], buf_ref.at[pl.ds(r, 1), :], sem)

    @pl.when(t < nv_ref[0])
    def _():
        def issue(r, carry):
            row_copy(r, rt_ref[t * tg + r]).start()
            return carry

        lax.fori_loop(0, tg, issue, 0, unroll=8)
        pltpu.make_async_copy(buf_ref, buf_ref, sem).wait()
        o_ref[...] = _rms(buf_ref[...], g_ref[...], RMS_EPS).astype(BF16)

    @pl.when(t >= nv_ref[0])
    def _():
        o_ref[...] = jnp.zeros(o_ref.shape, o_ref.dtype)


def _gather_norm_call(row_token, n_valid, x1, g, *, tg):
    r = row_token.shape[0]
    d = x1.shape[1]
    grid_spec = pltpu.PrefetchScalarGridSpec(
        num_scalar_prefetch=2,
        grid=(r // tg,),
        in_specs=[pl.BlockSpec(memory_space=pl.ANY), pl.BlockSpec((1, d), lambda t, rt, nv: (0, 0))],
        out_specs=pl.BlockSpec((tg, d), lambda t, rt, nv: (t, 0)),
        scratch_shapes=[pltpu.VMEM((tg, d), F32), pltpu.SemaphoreType.DMA(())],
    )
    return pl.pallas_call(
        functools.partial(_gather_norm_kernel, tg=tg),
        grid_spec=grid_spec,
        out_shape=jax.ShapeDtypeStruct((r, d), BF16),
        compiler_params=_params(("arbitrary",)),
        name="moe_gather",
    )(row_token, n_valid, x1, g)


def _moe_ffn_kernel(te_ref, nv_ref, h_ref, wg_ref, wu_ref, wd_ref, o_ref):
    del te_ref
    t = pl.program_id(0)
    j = pl.program_id(1)

    @pl.when(j == 0)
    def _():
        o_ref[...] = jnp.zeros(o_ref.shape, F32)

    @pl.when(t < nv_ref[0])
    def _():
        o_ref[...] += _swiglu_acc(h_ref[...], wg_ref[...], wu_ref[...], wd_ref[...])


def _moe_ffn_call(tile_expert, n_valid, xs, wg, wu, wd, m, *, tm, tf):
    r, d = xs.shape
    f = wg.shape[3]
    nj = f // tf

    def tile(t, nv):
        return jnp.minimum(t, nv[0] - 1)

    def fblk(t, j, nv):
        return jnp.where(t < nv[0], j, nj - 1)

    grid_spec = pltpu.PrefetchScalarGridSpec(
        num_scalar_prefetch=2,
        grid=(r // tm, nj),
        in_specs=[
            pl.BlockSpec((tm, d), lambda t, j, te, nv: (tile(t, nv), 0)),
            pl.BlockSpec((None, None, d, tf), lambda t, j, te, nv: (m, te[t], 0, fblk(t, j, nv))),
            pl.BlockSpec((None, None, d, tf), lambda t, j, te, nv: (m, te[t], 0, fblk(t, j, nv))),
            pl.BlockSpec((None, None, tf, d), lambda t, j, te, nv: (m, te[t], fblk(t, j, nv), 0)),
        ],
        out_specs=pl.BlockSpec((tm, d), lambda t, j, te, nv: (t, 0)),
    )
    return pl.pallas_call(
        _moe_ffn_kernel,
        grid_spec=grid_spec,
        out_shape=jax.ShapeDtypeStruct((r, d), F32),
        compiler_params=_params(("arbitrary", "arbitrary")),
        name="moe_ffn",
    )(tile_expert, n_valid, xs, wg, wu, wd)


def _combine_norm_kernel(pos_ref, x_ref, rg_ref, ys_hbm, g_ref, *rest, tc, final_norm, n_prompt_tiles):
    if final_norm:
        op_ref, os_ref, buf_ref, sem = rest
    else:
        o_ref, buf_ref, sem = rest
    i = pl.program_id(0)

    def row_copy(k, r, src):
        return pltpu.make_async_copy(ys_hbm.at[pl.ds(src, 1), :], buf_ref.at[k, pl.ds(r, 1), :], sem)

    def issue(r, carry):
        for k in range(TOP_K):
            row_copy(k, r, pos_ref[(i * tc + r) * TOP_K + k]).start()
        return carry

    lax.fori_loop(0, tc, issue, 0, unroll=4)
    pltpu.make_async_copy(buf_ref, buf_ref, sem).wait()
    rg = rg_ref[...]
    moe = rg[:, 0:1] * buf_ref[0]
    for k in range(1, TOP_K):
        moe = moe + rg[:, k:k + 1] * buf_ref[k]
    x2 = x_ref[...] + moe
    if final_norm:
        y = _rms(x2, g_ref[...], RMS_EPS)

        @pl.when(i < n_prompt_tiles)
        def _():
            op_ref[...] = y

        @pl.when(i >= n_prompt_tiles)
        def _():
            os_ref[...] = y
    else:
        o_ref[...] = x2


def _combine_norm_call(pos, x1, rg, ys, g, *, tc, final_norm, n_prompt):
    n, d = x1.shape
    npt = n_prompt // tc
    if final_norm:
        out_specs = [pl.BlockSpec((tc, d), lambda i, ps: (jnp.minimum(i, npt - 1), 0)),
                     pl.BlockSpec((tc, d), lambda i, ps: (jnp.maximum(i - npt, 0), 0))]
        out_shape = [jax.ShapeDtypeStruct((n_prompt, d), F32), jax.ShapeDtypeStruct((n - n_prompt, d), F32)]
    else:
        out_specs = pl.BlockSpec((tc, d), lambda i, ps: (i, 0))
        out_shape = jax.ShapeDtypeStruct((n, d), F32)
    grid_spec = pltpu.PrefetchScalarGridSpec(
        num_scalar_prefetch=1,
        grid=(n // tc,),
        in_specs=[
            pl.BlockSpec((tc, d), lambda i, ps: (i, 0)),
            pl.BlockSpec((tc, LANES), lambda i, ps: (i, 0)),
            pl.BlockSpec(memory_space=pl.ANY),
            pl.BlockSpec((1, d), lambda i, ps: (0, 0)),
        ],
        out_specs=out_specs,
        scratch_shapes=[pltpu.VMEM((TOP_K, tc, d), F32), pltpu.SemaphoreType.DMA(())],
    )
    return pl.pallas_call(
        functools.partial(_combine_norm_kernel, tc=tc, final_norm=final_norm, n_prompt_tiles=npt),
        grid_spec=grid_spec,
        out_shape=out_shape,
        compiler_params=_params(("arbitrary",)),
        name="moe_combine",
    )(pos, x1, rg, ys, g)


def _norm_kernel(x_ref, g_ref, o_ref):
    o_ref[...] = _rms(x_ref[...], g_ref[...], RMS_EPS)


def _norm_call(x, g, *, tm):
    n, d = x.shape
    return pl.pallas_call(
        _norm_kernel,
        grid=(n // tm,),
        in_specs=[pl.BlockSpec((tm, d), lambda i: (i, 0)), pl.BlockSpec((1, d), lambda i: (0, 0))],
        out_specs=pl.BlockSpec((tm, d), lambda i: (i, 0)),
        out_shape=jax.ShapeDtypeStruct((n, d), F32),
        compiler_params=_params(("parallel",)),
        name="final_norm",
    )(x, g)


def _route_plan(re, n_experts, tm):
    n = re.shape[0]
    e_flat = re.reshape(-1)
    oh = (e_flat[:, None] == jnp.arange(n_experts, dtype=jnp.int32)[None, :]).astype(jnp.int32)
    csum = jnp.cumsum(oh, axis=0)
    rank = jnp.sum((csum - oh) * oh, axis=1)
    counts = csum[-1]
    padded = ((counts + tm - 1) // tm) * tm
    ends = jnp.cumsum(padded)
    offs = ends - padded
    pos = jnp.sum(oh * offs[None, :], axis=1) + rank
    n_rows = ((n * TOP_K + tm - 1) // tm) * tm + n_experts * tm
    row_token = jnp.zeros((n_rows,), jnp.int32).at[pos].set(jnp.arange(n * TOP_K, dtype=jnp.int32) // TOP_K)
    n_tiles = n_rows // tm
    n_valid = (ends[-1] // tm).astype(jnp.int32)
    tile_start = jnp.arange(n_tiles, dtype=jnp.int32) * tm
    tile_e = jnp.sum((tile_start[:, None] >= ends[None, :]).astype(jnp.int32), axis=1)
    last_e = jnp.sum((tile_start[n_valid - 1] >= ends).astype(jnp.int32))
    tile_e = jnp.where(jnp.arange(n_tiles) < n_valid, tile_e, last_e).astype(jnp.int32)
    return pos.astype(jnp.int32), row_token, tile_e, n_valid.reshape(1)


def _tile(pref, *dims):
    t = pref
    for dim in dims:
        t = math.gcd(t, dim)
    return t


def _rope_tables(seq, n_past, t_new, bs, dk, rot_dim):
    half = rot_dim // 2
    inv = ROPE_THETA ** (-(jnp.arange(half, dtype=F32) * 2.0) / rot_dim)
    pos = jnp.concatenate([jnp.arange(seq), jnp.tile(n_past + jnp.arange(t_new), bs)])
    ang = pos.astype(F32)[:, None] * inv[None, :]
    cos, sin = jnp.cos(ang), jnp.sin(ang)
    n = pos.shape[0]
    ones = jnp.ones((n, dk - rot_dim), F32)
    zeros_h = jnp.zeros((n, half), F32)
    zeros_r = jnp.zeros((n, dk - rot_dim), F32)
    cos_t = jnp.concatenate([cos, cos, ones], axis=1)
    sa_t = jnp.concatenate([zeros_h, sin, zeros_r], axis=1)
    sb_t = jnp.concatenate([-sin, zeros_h, zeros_r], axis=1)
    return cos_t, sa_t, sb_t


def _page_mats(heads, page, n_pages, nb):
    w = heads * page
    dst = jnp.arange(w)
    d_head, d_slot = dst // page, dst % page
    same = d_head[:, None] == d_head[None, :]
    mcs = (same & (d_slot[:, None] <= d_slot[None, :])).astype(BF16)
    mbc = ((d_head[:, None] == d_head[None, :]) & (d_slot[:, None] == page - 1)).astype(BF16)
    msel = ((d_slot[:, None] == page - 1) & (d_head[:, None] == jnp.arange(LANES)[None, :])).astype(BF16)
    r = jnp.arange(nb * n_pages)
    lst = ((r[None, :] // n_pages == r[:, None] // n_pages) & (r[None, :] < r[:, None])).astype(BF16)
    return mcs, mbc, msel, lst


def kernel(x_prompt, x_sample, cache_k_diff, cache_v_diff, cache_k_fox, cache_v_fox, cache_logf_fox, state_rglru_h, state_rglru_conv, page_table, norm_mix_g, w_in, b_fgate, lambda_q1, lambda_k1, lambda_q2, lambda_k2, subln_g, conv_w, conv_b, w_gate_a, b_gate_a, w_gate_x, b_gate_x, lru_lambda, w_out, norm_ffn_g, w_dense_gate, w_dense_up, w_dense_down, w_router, w_moe_gate, w_moe_up, w_moe_down, norm_final_g):
    bp, seq, d = x_prompt.shape
    bs, t_new, _ = x_sample.shape
    depth = w_in.shape[0]
    n_pool, page, h_a = cache_k_diff.shape[1], cache_k_diff.shape[2], cache_k_diff.shape[3]
    dk_a = cache_k_diff.shape[5]
    dv_a = cache_v_diff.shape[4]
    h_b, dh_b = cache_k_fox.shape[3], cache_k_fox.shape[4]
    w_c = state_rglru_h.shape[2]
    cw_len = conv_w.shape[1]
    n_pages = page_table.shape[1]
    n_past = n_pages * page
    rot_dim = dk_a // 4
    n_experts = w_router.shape[2]
    n_p, n_s = bp * seq, bs * t_new
    n = n_p + n_s
    assert t_new == SUBLANES and page == LANES and dv_a == 2 * dk_a and dk_a == LANES and dh_b == LANES

    w_qa = h_a * 2 * dk_a
    w_va = h_a * dv_a
    w_b = h_b * dh_b
    col_qa, col_ka, col_va = 0, w_qa, 2 * w_qa
    col_qb = col_va + w_va
    col_kb, col_vb = col_qb + w_b, col_qb + 2 * w_b
    col_g = col_vb + w_b
    col_x = col_g + w_c
    n_main = col_x + w_c
    src_fl = 2 * w_qa + w_va + 3 * w_b

    tm = _tile(TM_TOKENS, seq, n_s)
    tn = _tile(TN_PROJ, w_qa, src_fl, n_main - src_fl)
    tq = _tile(TQ_ATTN, seq)
    tt = _tile(TT_LRU, seq)
    tm_mix = _tile(TM_MIX, tm)
    tf = _tile(TF_FFN, w_dense_gate.shape[2])
    tm_moe = _tile(TM_MOE, n)
    tc = _tile(TC_COMBINE, n)

    rope = _rope_tables(seq, n_past, t_new, bs, dk_a, rot_dim)
    tri = jnp.tril(jnp.ones((seq, seq), BF16))
    page_mats = _page_mats(h_b, page, n_pages, _tile(LOGF_BATCH, bs))

    kc_d = cache_k_diff.reshape(depth, n_pool, page * h_a * 2, dk_a)
    nc_a = dv_a // LANES
    vc_d = jnp.swapaxes(cache_v_diff.reshape(depth, n_pool, page, h_a, nc_a, LANES), 3, 4).reshape(
        depth, n_pool, page * nc_a * h_a, LANES)
    kc_f = cache_k_fox.reshape(depth, n_pool, page * h_b, dh_b)
    vc_f = cache_v_fox.reshape(depth, n_pool, page * h_b, dh_b)
    lf_pages = jnp.swapaxes(cache_logf_fox, 2, 3).reshape(depth, n_pool, h_b * page)
    n_group = _tile(PAGES_PER_STEP, n_pages)

    x = jnp.concatenate([x_prompt.reshape(n_p, d), x_sample.reshape(n_s, d)], axis=0)
    row2 = lambda v: v.reshape(1, -1).astype(F32)
    st = {k: [] for k in ('kd', 'vd', 'kf', 'vf', 'lf', 'h_p', 'h_s', 'cv')}
    y_p = y_s = None
    cache_a = cache_b = None

    for l in range(depth):
        lam_init = 0.8 - 0.6 * math.exp(-0.3 * l)
        w_l = w_in[l]
        w_b2 = w_l[:, src_fl + h_b:].astype(BF16)
        b_fl = jnp.zeros((1, LANES), F32).at[0, :h_b].set(b_fgate[l])
        proj, logf = _proj_call(
            x, row2(norm_mix_g[l]), w_in, l, src_fl, w_b2, b_fl, rope, tm=tm, tn=tn, n_rope_cols=2 * w_qa,
            n_prompt_tiles=n_p // tm, rope_prompt_tiles=seq // tm, rot_half=rot_dim // 2)

        lams = (row2(lambda_q1[l]), row2(lambda_k1[l]), row2(lambda_q2[l]), row2(lambda_k2[l]))
        sg = row2(subln_g[l])

        oa_p, *cache_a = _diff_prompt_call(proj, lams, sg, cache_a, l, depth, batch=bp, seq=seq, heads=h_a,
                                           dk=dk_a, dv=dv_a, tq=tq, k_col=col_ka, v_col=col_va,
                                           lam_init=lam_init)
        c_col = _cumsum_call(logf, tri, bp, seq)
        c_row = jnp.swapaxes(c_col[:, :h_b].reshape(bp, seq, h_b), 1, 2).reshape(bp * h_b, seq // tq, tq)
        ob_p, *cache_b = _fox_prompt_call(proj, c_col, c_row, cache_b, l, depth, batch=bp, seq=seq, heads=h_b,
                                          dh=dh_b, tq=tq, q_col=col_qb, k_col=col_kb, v_col=col_vb)
        lru_w = (conv_w[l], row2(conv_b[l]), w_gate_a[l].astype(BF16), row2(b_gate_a[l]),
                 w_gate_x[l].astype(BF16), row2(b_gate_x[l]), row2(lru_lambda[l]))
        oc_p, ht_p = _lru_prompt_call(proj, lru_w, batch=bp, seq=seq, wc=w_c, tt=tt, g_col=col_g, x_col=col_x)

        cpast, ctot = _logf_pages_call(page_table, lf_pages[l], page_mats, heads=h_b, page=page)
        oa_s, ob_s = _sample_attn_call(
            page_table, proj, logf, (kc_d, vc_d, kc_f, vc_f), lams, sg, cpast, ctot, l, n_prompt=n_p,
            t_new=t_new, h_a=h_a, dk=dk_a, dv=dv_a, h_b=h_b, dh=dh_b, page=page,
            cols=(col_qa, col_ka, col_va, col_qb, col_kb, col_vb), lam_init=lam_init, n_group=n_group)
        prev8 = jnp.pad(state_rglru_conv[l], ((0, 0), (SUBLANES - (cw_len - 1), 0), (0, 0))).reshape(n_s, w_c)
        h0rep = jnp.repeat(state_rglru_h[l], t_new, axis=0)
        oc_s, hs_s = _lru_sample_call(proj, prev8, h0rep, lru_w, n_prompt=n_p, n_sample=n_s, wc=w_c,
                                      g_col=col_g, x_col=col_x)

        br_p, br_s = (oa_p, ob_p, oc_p), (oa_s, ob_s, oc_s)
        g_ffn = row2(norm_ffn_g[l])
        m = l // 2
        if l % 2 == 0:
            x1, hn = _mix_call(br_p, br_s, w_out[l].astype(BF16), x, g_ffn, None, tm=tm_mix)
            x = _ffn_call(hn, w_dense_gate, w_dense_up, w_dense_down, m, x1, tm=tm, tf=tf)
            if l == depth - 1:
                y_all = _norm_call(x, row2(norm_final_g), tm=tm_mix)
                y_p, y_s = y_all[:n_p], y_all[n_p:]
        else:
            x1, rg, re = _mix_call(br_p, br_s, w_out[l].astype(BF16), x, g_ffn, w_router[m], tm=tm_mix)
            pos, row_token, tile_e, n_valid = _route_plan(re[:, :TOP_K], n_experts, tm_moe)
            xs = _gather_norm_call(row_token, n_valid, x1, g_ffn, tg=tm_moe)
            ys = _moe_ffn_call(tile_e, n_valid, xs, w_moe_gate, w_moe_up, w_moe_down, m, tm=tm_moe,
                               tf=_tile(TF_MOE, w_moe_gate.shape[3]))
            final = l == depth - 1
            x = _combine_norm_call(pos, x1, rg, ys, row2(norm_final_g), tc=tc, final_norm=final, n_prompt=n_p)
            if final:
                y_p, y_s = x

        st['kd'].append(proj[n_p:, col_ka:col_ka + w_qa])
        st['vd'].append(proj[n_p:, col_va:col_va + w_va])
        st['kf'].append(proj[n_p:, col_kb:col_kb + w_b])
        st['vf'].append(proj[n_p:, col_vb:col_vb + w_b])
        st['lf'].append(logf[:, :h_b])
        st['h_p'].append(ht_p.reshape(bp, w_c))
        st['h_s'].append(hs_s.reshape(bs, t_new, w_c)[:, t_new - 1])
        st['cv'].append(proj[:, col_x:col_x + w_c])

    def both(name, shape_tail):
        a = jnp.stack(st[name])
        return (a[:, :n_p].reshape((depth, bp, seq) + shape_tail),
                a[:, n_p:].reshape((depth, bs, t_new) + shape_tail))

    sample = lambda name, tail: jnp.stack(st[name]).reshape((depth, bs, t_new) + tail)
    kd_s, vd_s = sample('kd', (h_a, 2, dk_a)), sample('vd', (h_a, dv_a))
    kf_s, vf_s = sample('kf', (h_b, dh_b)), sample('vf', (h_b, dh_b))
    kd_p = cache_a[0].reshape(depth, bp, seq, h_a, 2, dk_a)
    vd_p = jnp.swapaxes(cache_a[1].reshape(depth, bp, seq, nc_a, h_a, LANES), 3, 4).reshape(
        depth, bp, seq, h_a, dv_a)
    kf_p = cache_b[0].reshape(depth, bp, seq, h_b, dh_b)
    vf_p = cache_b[1].reshape(depth, bp, seq, h_b, dh_b)
    lf_p, lf_s = both('lf', (h_b,))
    cv_p, cv_s = both('cv', (w_c,))
    keep = cw_len - 1
    return (y_p.reshape(bp, seq, d), y_s.reshape(bs, t_new, d),
            kd_p, vd_p, kf_p, vf_p, lf_p, jnp.stack(st['h_p']), cv_p[:, :, seq - keep:],
            kd_s, vd_s, kf_s, vf_s, lf_s, jnp.stack(st['h_s']), cv_s[:, :, t_new - keep:])
```

```python
import functools
import math

import jax
import jax.numpy as jnp
from jax import lax
from jax.experimental import pallas as pl
from jax.experimental.pallas import tpu as pltpu

F32 = jnp.float32
BF16 = jnp.bfloat16

LANES = 128
SUBLANES = 8
VMEM_LIMIT = 56 * 1024 * 1024

TM_TOKENS = 1024
TN_PROJ = 512
TQ_ATTN = 512
TT_LRU = 256
TM_MIX = 512
TF_FFN = 256
TF_MOE = 256
TM_MOE = 1024
TC_COMBINE = 256
PAGES_PER_STEP = 8
LOGF_BATCH = 8

RMS_EPS = 1e-6
SUBLN_EPS = 1e-5
LRU_C = 8.0
ROPE_THETA = 500000.0
TOP_K = 2
NEG_INF = float("-inf")
LOG2E = 1.4426950408889634
DMA_PRIORITIES = 2


def _params(sem, vmem=VMEM_LIMIT):
    return pltpu.CompilerParams(dimension_semantics=sem, vmem_limit_bytes=vmem)


def _split3(x):
    hi = x.astype(BF16)
    r = x - hi.astype(F32)
    mid = r.astype(BF16)
    lo = (r - mid.astype(F32)).astype(BF16)
    return hi, mid, lo


def _dot_sel(x, m):
    hi, mid, lo = _split3(x)
    d = lambda a: jnp.dot(a, m, preferred_element_type=F32)
    return d(hi) + d(mid) + d(lo)


def _sel_dot(m, x):
    hi, mid, lo = _split3(x)
    d = lambda a: jnp.dot(m, a, preferred_element_type=F32)
    return d(hi) + d(mid) + d(lo)


def _dot_nt(a, b):
    return lax.dot_general(a, b, (((1,), (1,)), ((), ())), preferred_element_type=F32)


def _sigmoid(x):
    return 1.0 / (1.0 + jnp.exp(-x))


def _softplus(x):
    return jnp.maximum(x, 0.0) + jnp.log1p(jnp.exp(-jnp.abs(x)))


def _rms(x, g, eps):
    ms = jnp.mean(x * x, axis=-1, keepdims=True)
    return x * lax.rsqrt(ms + eps) * g


def _proj_kernel(x_ref, g_ref, wa_ref, wb_ref, wfl_ref, bf_ref, cos_ref, sa_ref, sb_ref,
                 proj_ref, logf_ref, hn_ref, *, n_rope_tiles, n_a_tiles, tn, rot_half):
    j = pl.program_id(1)

    @pl.when(j == 0)
    def _():
        hn = _rms(x_ref[...], g_ref[...], RMS_EPS).astype(BF16)
        hn_ref[...] = hn
        z = jnp.dot(hn, wfl_ref[...].astype(BF16), preferred_element_type=F32) + bf_ref[...]
        logf_ref[...] = jnp.minimum(z, 0.0) - jnp.log1p(jnp.exp(-jnp.abs(z)))

    @pl.when(j < n_rope_tiles)
    def _():
        y = jnp.dot(hn_ref[...], wa_ref[...].astype(BF16), preferred_element_type=F32)
        c, sa, sb = cos_ref[...], sa_ref[...], sb_ref[...]
        for k in range(tn // LANES):
            yc = y[:, k * LANES:(k + 1) * LANES]
            proj_ref[:, k * LANES:(k + 1) * LANES] = (
                yc * c + pltpu.roll(yc, rot_half, 1) * sa + pltpu.roll(yc, LANES - rot_half, 1) * sb)

    @pl.when((j >= n_rope_tiles) & (j < n_a_tiles))
    def _():
        proj_ref[...] = jnp.dot(hn_ref[...], wa_ref[...].astype(BF16), preferred_element_type=F32)

    @pl.when(j >= n_a_tiles)
    def _():
        proj_ref[...] = jnp.dot(hn_ref[...], wb_ref[...], preferred_element_type=F32)


def _proj_call(x, g, w_in, layer, fl_col, wb, bfl, rope, *, tm, tn, n_rope_cols, n_prompt_tiles,
               rope_prompt_tiles, rot_half):
    n, d = x.shape
    na, nb = fl_col // tn, wb.shape[1] // tn
    n_main = (na + nb) * tn
    cos_t, sa_t, sb_t = rope
    rope_map = lambda i, j: (jnp.where(i < n_prompt_tiles, i % rope_prompt_tiles,
                                       rope_prompt_tiles + i - n_prompt_tiles), 0)
    kern = functools.partial(_proj_kernel, n_rope_tiles=n_rope_cols // tn, n_a_tiles=na, tn=tn,
                             rot_half=rot_half)
    return pl.pallas_call(
        kern,
        grid=(n // tm, n_main // tn),
        in_specs=[
            pl.BlockSpec((tm, d), lambda i, j: (i, 0)),
            pl.BlockSpec((1, d), lambda i, j: (0, 0)),
            pl.BlockSpec((None, d, tn), lambda i, j: (layer, 0, jnp.minimum(j, na - 1))),
            pl.BlockSpec((d, tn), lambda i, j: (0, jnp.maximum(j - na, 0))),
            pl.BlockSpec((None, d, LANES), lambda i, j: (layer, 0, fl_col // LANES)),
            pl.BlockSpec((1, LANES), lambda i, j: (0, 0)),
            pl.BlockSpec((tm, LANES), rope_map),
            pl.BlockSpec((tm, LANES), rope_map),
            pl.BlockSpec((tm, LANES), rope_map),
        ],
        out_specs=[
            pl.BlockSpec((tm, tn), lambda i, j: (i, j)),
            pl.BlockSpec((tm, LANES), lambda i, j: (i, 0)),
        ],
        out_shape=[jax.ShapeDtypeStruct((n, n_main), F32), jax.ShapeDtypeStruct((n, LANES), F32)],
        scratch_shapes=[pltpu.VMEM((tm, d), BF16)],
        compiler_params=_params(("parallel", "arbitrary")),
        name="proj",
    )(x, g, w_in, wb, w_in, bfl, cos_t, sa_t, sb_t)


def _cumsum_kernel(x_ref, tri_ref, o_ref):
    o_ref[...] = _sel_dot(tri_ref[...], x_ref[...])


def _cumsum_call(logf_p, tri, batch, seq):
    return pl.pallas_call(
        _cumsum_kernel,
        grid=(batch,),
        in_specs=[pl.BlockSpec((seq, LANES), lambda b: (b, 0)),
                  pl.BlockSpec((seq, seq), lambda b: (0, 0))],
        out_specs=pl.BlockSpec((seq, LANES), lambda b: (b, 0)),
        out_shape=jax.ShapeDtypeStruct((batch * seq, LANES), F32),
        compiler_params=_params(("parallel",)),
        name="cumsum_prompt",
    )(logf_p, tri)


def _lane_chunks(s):
    return [s[:, c * LANES:(c + 1) * LANES] for c in range(s.shape[1] // LANES)]


def _rowmax(s):
    return jnp.max(functools.reduce(jnp.maximum, _lane_chunks(s)), axis=-1, keepdims=True)


def _rowsum(s):
    return jnp.sum(functools.reduce(jnp.add, _lane_chunks(s)), axis=-1, keepdims=True)


def _softmax_step(s, v, m_ref, l_ref, acc_ref, idx):
    m_old = m_ref[idx]
    m_new = jnp.maximum(m_old, _rowmax(s))
    alpha = jnp.exp2(m_old - m_new)
    p_chunks = [jnp.exp2(c - m_new) for c in _lane_chunks(s)]
    l_ref[idx] = alpha * l_ref[idx] + functools.reduce(jnp.add, p_chunks)
    p = jnp.concatenate(p_chunks, axis=1).astype(BF16)
    pv = jnp.dot(p, v, preferred_element_type=F32)
    acc_ref[idx] = jnp.concatenate([alpha] * (pv.shape[1] // LANES), axis=1) * acc_ref[idx] + pv
    m_ref[idx] = m_new


def _softmax_denominator(l_ref, idx):
    return jnp.sum(l_ref[idx], axis=-1, keepdims=True)


def _softmax_step_stacked(s, v_heads, m_ref, l_ref, acc_ref, rows_per_head):
    m_old = m_ref[...]
    m_new = jnp.maximum(m_old, _rowmax(s))
    p = jnp.exp(s - m_new)
    alpha = jnp.exp(m_old - m_new)
    l_ref[...] = alpha * l_ref[...] + _rowsum(p)
    pv = [jnp.dot(p[h * rows_per_head:(h + 1) * rows_per_head, :].astype(BF16), v,
                  preferred_element_type=F32) for h, v in enumerate(v_heads)]
    acc_ref[...] = alpha * acc_ref[...] + jnp.concatenate(pv, axis=0)
    m_ref[...] = m_new


def _diff_lambda(lq1, lk1, lq2, lk2, lam_init):
    return (jnp.exp(jnp.sum(lq1[...] * lk1[...], axis=-1, keepdims=True))
            - jnp.exp(jnp.sum(lq2[...] * lk2[...], axis=-1, keepdims=True)) + lam_init)


def _diff_finish(acc_ref, l_ref, base, lam, sg, lam_init):
    o = (acc_ref[base] / _softmax_denominator(l_ref, base)
         - lam * (acc_ref[base + 1] / _softmax_denominator(l_ref, base + 1)))
    return _rms(o, sg, SUBLN_EPS) * (1.0 - lam_init)


def _diff_prompt_kernel(q_ref, k_ref, v_ref, lq1, lk1, lq2, lk2, sg_ref, *rest, tq, dk, heads, n_alias,
                        lam_init, scale):
    o_ref, kc_ref, vc_ref, kb_ref, vb_ref, m_ref, l_ref, acc_ref = rest[n_alias:]
    h = pl.program_id(1)
    qi = pl.program_id(2)
    seq = k_ref.shape[0]

    @pl.when(qi == 0)
    def _():
        k = k_ref[...]
        v = v_ref[...]
        for m in range(2):
            kb_ref[m] = k[:, m * dk:(m + 1) * dk].T.astype(BF16)
            kc_ref[pl.ds(2 * h + m, seq, stride=2 * heads), :] = k[:, m * dk:(m + 1) * dk]
        vb_ref[...] = v.astype(BF16)
        nc = v.shape[1] // LANES
        for c in range(nc):
            vc_ref[pl.ds(c * heads + h, seq, stride=nc * heads), :] = v[:, c * LANES:(c + 1) * LANES]

    m_ref[...] = jnp.full(m_ref.shape, NEG_INF, F32)
    l_ref[...] = jnp.zeros(l_ref.shape, F32)
    acc_ref[...] = jnp.zeros(acc_ref.shape, F32)
    q = (q_ref[...] * (scale * LOG2E)).astype(BF16)

    def block(kk, causal):
        k0 = pl.multiple_of(kk * tq, tq)
        vblk = vb_ref[pl.ds(k0, tq), :]
        for m in range(2):
            s = jnp.dot(q[:, m * dk:(m + 1) * dk], kb_ref[m, :, pl.ds(k0, tq)],
                        preferred_element_type=F32)
            if causal:
                row = lax.broadcasted_iota(jnp.int32, s.shape, 0)
                col = lax.broadcasted_iota(jnp.int32, s.shape, 1)
                s = jnp.where(col <= row, s, NEG_INF)
            _softmax_step(s, vblk, m_ref, l_ref, acc_ref, m)

    def body(kk, carry):
        block(kk, False)
        return carry

    lax.fori_loop(0, qi, body, 0)
    block(qi, True)

    lam = _diff_lambda(lq1, lk1, lq2, lk2, lam_init)
    o_ref[...] = _diff_finish(acc_ref, l_ref, 0, lam, sg_ref[...], lam_init).astype(o_ref.dtype)


def _cache_out(prev, depth, rows_total, rows_block, layer):
    out_spec = pl.BlockSpec((None, rows_block, LANES), lambda *idx: (layer, idx[0], 0))
    out_shape = jax.ShapeDtypeStruct((depth, rows_total, LANES), F32)
    return out_spec, out_shape, ([] if prev is None else [prev])


def _diff_prompt_call(proj, lams, sg, prev, layer, depth, *, batch, seq, heads, dk, dv, tq, k_col, v_col,
                      lam_init):
    nq = seq // tq
    n_rows = batch * seq
    lam_spec = pl.BlockSpec((1, dk), lambda b, h, q: (0, 0))
    rows_k, rows_v = seq * 2 * heads, seq * heads * dv // LANES
    ks, kshape, kprev = _cache_out(None if prev is None else prev[0], depth, batch * rows_k, rows_k, layer)
    vs, vshape, vprev = _cache_out(None if prev is None else prev[1], depth, batch * rows_v, rows_v, layer)
    aliased = kprev + vprev
    n_in = 8
    kern = functools.partial(_diff_prompt_kernel, tq=tq, dk=dk, heads=heads, n_alias=len(aliased),
                             lam_init=lam_init, scale=dk ** -0.5)
    return pl.pallas_call(
        kern,
        grid=(batch, heads, nq),
        in_specs=[
            pl.BlockSpec((tq, 2 * dk), lambda b, h, q: (b * nq + q, h)),
            pl.BlockSpec((seq, 2 * dk), lambda b, h, q: (b, k_col // (2 * dk) + h)),
            pl.BlockSpec((seq, dv), lambda b, h, q: (b, v_col // dv + h)),
            lam_spec, lam_spec, lam_spec, lam_spec,
            pl.BlockSpec((1, dv), lambda b, h, q: (0, 0)),
        ] + [pl.BlockSpec(memory_space=pl.ANY)] * len(aliased),
        out_specs=[pl.BlockSpec((tq, dv), lambda b, h, q: (b * nq + q, h)), ks, vs],
        out_shape=[jax.ShapeDtypeStruct((n_rows, heads * dv), BF16), kshape, vshape],
        input_output_aliases={n_in + i: 1 + i for i in range(len(aliased))},
        scratch_shapes=[
            pltpu.VMEM((2, dk, seq), BF16), pltpu.VMEM((seq, dv), BF16),
            pltpu.VMEM((2, tq, LANES), F32), pltpu.VMEM((2, tq, LANES), F32), pltpu.VMEM((2, tq, dv), F32),
        ],
        compiler_params=_params(("parallel", "arbitrary", "arbitrary")),
        name="diff_prompt",
    )(proj, proj, proj, *lams, sg, *aliased)


def _fox_prompt_kernel(q_ref, k_ref, v_ref, cc_ref, cr_ref, *rest, tq, heads, n_alias, scale):
    o_ref, kc_ref, vc_ref, kb_ref, vb_ref, m_ref, l_ref, acc_ref = rest[n_alias:]
    h = pl.program_id(1)
    qi = pl.program_id(2)
    seq = k_ref.shape[0]

    @pl.when(qi == 0)
    def _():
        k = k_ref[...]
        v = v_ref[...]
        kb_ref[...] = k.T.astype(BF16)
        vb_ref[...] = v.astype(BF16)
        kc_ref[pl.ds(h, seq, stride=heads), :] = k
        vc_ref[pl.ds(h, seq, stride=heads), :] = v

    m_ref[...] = jnp.full(m_ref.shape, NEG_INF, F32)
    l_ref[...] = jnp.zeros(l_ref.shape, F32)
    acc_ref[...] = jnp.zeros(acc_ref.shape, F32)
    q = (q_ref[...] * (scale * LOG2E)).astype(BF16)
    cc = cc_ref[...]
    lane = lax.broadcasted_iota(jnp.int32, cc.shape, 1)
    cq = jnp.sum(jnp.where(lane == h, cc, 0.0), axis=-1, keepdims=True) * LOG2E

    def block(kk, causal):
        k0 = pl.multiple_of(kk * tq, tq)
        ck = cr_ref[0, pl.ds(kk, 1), :] * LOG2E
        s = jnp.dot(q, kb_ref[:, pl.ds(k0, tq)], preferred_element_type=F32) + (cq - ck)
        if causal:
            row = lax.broadcasted_iota(jnp.int32, s.shape, 0)
            col = lax.broadcasted_iota(jnp.int32, s.shape, 1)
            s = jnp.where(col <= row, s, NEG_INF)
        _softmax_step(s, vb_ref[pl.ds(k0, tq), :], m_ref, l_ref, acc_ref, 0)

    def body(kk, carry):
        block(kk, False)
        return carry

    lax.fori_loop(0, qi, body, 0)
    block(qi, True)
    o_ref[...] = (acc_ref[0] / _softmax_denominator(l_ref, 0)).astype(o_ref.dtype)


def _fox_prompt_call(proj, c_col, c_row, prev, layer, depth, *, batch, seq, heads, dh, tq, q_col, k_col, v_col):
    nq = seq // tq
    rows = seq * heads
    ks, kshape, kprev = _cache_out(None if prev is None else prev[0], depth, batch * rows, rows, layer)
    vs, vshape, vprev = _cache_out(None if prev is None else prev[1], depth, batch * rows, rows, layer)
    aliased = kprev + vprev
    n_in = 5
    kern = functools.partial(_fox_prompt_kernel, tq=tq, heads=heads, n_alias=len(aliased), scale=dh ** -0.5)
    return pl.pallas_call(
        kern,
        grid=(batch, heads, nq),
        in_specs=[
            pl.BlockSpec((tq, dh), lambda b, h, q: (b * nq + q, q_col // dh + h)),
            pl.BlockSpec((seq, dh), lambda b, h, q: (b, k_col // dh + h)),
            pl.BlockSpec((seq, dh), lambda b, h, q: (b, v_col // dh + h)),
            pl.BlockSpec((tq, LANES), lambda b, h, q: (b * nq + q, 0)),
            pl.BlockSpec((1, nq, tq), lambda b, h, q: (b * heads + h, 0, 0)),
        ] + [pl.BlockSpec(memory_space=pl.ANY)] * len(aliased),
        out_specs=[pl.BlockSpec((tq, dh), lambda b, h, q: (b * nq + q, h)), ks, vs],
        out_shape=[jax.ShapeDtypeStruct((batch * seq, heads * dh), BF16), kshape, vshape],
        input_output_aliases={n_in + i: 1 + i for i in range(len(aliased))},
        scratch_shapes=[
            pltpu.VMEM((dh, seq), BF16), pltpu.VMEM((seq, dh), BF16),
            pltpu.VMEM((1, tq, LANES), F32), pltpu.VMEM((1, tq, LANES), F32), pltpu.VMEM((1, tq, dh), F32),
        ],
        compiler_params=_params(("parallel", "arbitrary", "arbitrary")),
        name="fox_prompt",
    )(proj, proj, proj, c_col, c_row, *aliased)


def _pad_rows(x, rows):
    return jnp.concatenate([x, jnp.zeros((rows - x.shape[0], x.shape[1]), x.dtype)], axis=0)


def _causal_new(s, t_new):
    row = lax.broadcasted_iota(jnp.int32, s.shape, 0) % t_new
    col = lax.broadcasted_iota(jnp.int32, s.shape, 1)
    return jnp.where(col <= row, s, NEG_INF)


def _logf_pages_kernel(pt_ref, lf_ref, mcs_ref, mbc_ref, msel_ref, lst_ref, cp_ref, ct_ref, g_ref,
                       *, n_pages, heads, page):
    nb = lst_ref.shape[0] // n_pages
    b0 = pl.program_id(0) * nb
    for i in range(nb):
        for p in range(n_pages):
            g_ref[pl.ds(i * n_pages + p, 1), :] = lf_ref[pl.ds(pt_ref[b0 + i, p], 1), :]
    x = g_ref[...]
    y = _dot_sel(x, mcs_ref[...])
    tot = _dot_sel(y, mbc_ref[...])
    c = y + _sel_dot(lst_ref[...], tot)
    last = _dot_sel(c, msel_ref[...])
    for i in range(nb):
        r0 = i * n_pages
        for h in range(heads):
            cp_ref[i, h * n_pages:(h + 1) * n_pages, :] = c[r0:r0 + n_pages, h * page:(h + 1) * page]
        ct_ref[i] = last[r0 + n_pages - 1:r0 + n_pages, :]


def _logf_pages_call(page_table, lf2, mats, *, heads, page):
    bs, n_pages = page_table.shape
    n_pool, w = lf2.shape
    mcs, mbc, msel, lst = mats
    nb = lst.shape[0] // n_pages
    full = lambda a: pl.BlockSpec(a.shape, lambda b, pt: (0,) * a.ndim)
    kern = functools.partial(_logf_pages_kernel, n_pages=n_pages, heads=heads, page=page)
    grid_spec = pltpu.PrefetchScalarGridSpec(
        num_scalar_prefetch=1,
        grid=(bs // nb,),
        in_specs=[full(lf2), full(mcs), full(mbc), full(msel), full(lst)],
        out_specs=[pl.BlockSpec((nb, heads * n_pages, page), lambda b, pt: (b, 0, 0)),
                   pl.BlockSpec((nb, 1, LANES), lambda b, pt: (b, 0, 0))],
        scratch_shapes=[pltpu.VMEM((nb * n_pages, w), F32)],
    )
    return pl.pallas_call(
        kern,
        grid_spec=grid_spec,
        out_shape=[jax.ShapeDtypeStruct((bs, heads * n_pages, page), F32),
                   jax.ShapeDtypeStruct((bs, 1, LANES), F32)],
        compiler_params=_params(("arbitrary",)),
        name="logf_pages",
    )(page_table, lf2, mcs, mbc, msel, lst)


def _sample_attn_kernel(pt_ref, qd_ref, knd_ref, vnd_ref, qf_ref, knf_ref, vnf_ref, lfn_ref, *refs,
                        h_a, dk, dv, h_b, dh, page, n_pages, n_group, lam_init):
    del pt_ref
    g = n_group
    kcd, vcd, kcf, vcf = refs[:g], refs[g:2 * g], refs[2 * g:3 * g], refs[3 * g:4 * g]
    (lq1, lk1, lq2, lk2, sg_ref, cp_ref, ct_ref, od_ref, of_ref,
     md_ref, ld_ref, accd_ref, mf_ref, lf_ref, accf_ref, cn_ref) = refs[4 * g:]
    p = pl.program_id(1)
    last = pl.num_programs(1) - 1
    t_new = qd_ref.shape[0]
    n_maps = 2 * h_a
    nc = dv // LANES
    scale_d, scale_f = dk ** -0.5, dh ** -0.5

    @pl.when(p == 0)
    def _():
        for m_ref, l_ref, acc_ref in ((md_ref, ld_ref, accd_ref), (mf_ref, lf_ref, accf_ref)):
            m_ref[...] = jnp.full(m_ref.shape, NEG_INF, F32)
            l_ref[...] = jnp.zeros(l_ref.shape, F32)
            acc_ref[...] = jnp.zeros(acc_ref.shape, F32)
        x = lfn_ref[...]
        row = lax.broadcasted_iota(jnp.int32, x.shape, 0)
        k = 1
        while k < x.shape[0]:
            x = x + jnp.where(row >= k, pltpu.roll(x, k, 0), 0.0)
            k *= 2
        cn_ref[...] = ct_ref[0] + x

    qd = qd_ref[...]
    qdb = [qd[:, hm * dk:(hm + 1) * dk].astype(BF16) for hm in range(n_maps)]
    s_d = jnp.concatenate([
        jnp.concatenate([_dot_nt(qdb[hm], kc[pl.ds(hm, page, stride=n_maps), :].astype(BF16)) for kc in kcd],
                        axis=1) for hm in range(n_maps)], axis=0) * scale_d
    v_d = [jnp.concatenate([
        jnp.concatenate([vc[pl.ds(c * h_a + h, page, stride=h_a * nc), :] for c in range(nc)], axis=1)
        for vc in vcd], axis=0).astype(BF16) for h in range(h_a)]
    _softmax_step_stacked(s_d, v_d, md_ref, ld_ref, accd_ref, 2 * t_new)

    qf = qf_ref[...]
    cn = cn_ref[...]
    qfb = [qf[:, h * dh:(h + 1) * dh].astype(BF16) for h in range(h_b)]
    cq = [cn[:, h:h + 1] for h in range(h_b)]
    s_f = jnp.concatenate([
        jnp.concatenate([_dot_nt(qfb[h], kc[pl.ds(h, page, stride=h_b), :].astype(BF16)) for kc in kcf],
                        axis=1) for h in range(h_b)], axis=0) * scale_f
    bias = jnp.concatenate([
        cq[h] - jnp.concatenate([cp_ref[0, pl.ds(h * n_pages + p * g + i, 1), :] for i in range(g)], axis=1)
        for h in range(h_b)], axis=0)
    v_f = [jnp.concatenate([vc[pl.ds(h, page, stride=h_b), :] for vc in vcf], axis=0).astype(BF16)
           for h in range(h_b)]
    _softmax_step_stacked(s_f + bias, v_f, mf_ref, lf_ref, accf_ref, t_new)

    @pl.when(p == last)
    def _():
        knd, vnd = knd_ref[...], vnd_ref[...]
        sd_new = jnp.concatenate([
            _dot_nt(qdb[hm], _pad_rows(knd[:, hm * dk:(hm + 1) * dk], page).astype(BF16))
            for hm in range(n_maps)], axis=0) * scale_d
        vd_new = [_pad_rows(vnd[:, h * dv:(h + 1) * dv], page).astype(BF16) for h in range(h_a)]
        _softmax_step_stacked(_causal_new(sd_new, t_new), vd_new, md_ref, ld_ref, accd_ref, 2 * t_new)

        knf, vnf = knf_ref[...], vnf_ref[...]
        sf_new = jnp.concatenate([
            _dot_nt(qfb[h], _pad_rows(knf[:, h * dh:(h + 1) * dh], page).astype(BF16)) for h in range(h_b)],
            axis=0) * scale_f
        row = lax.broadcasted_iota(jnp.int32, (t_new, page), 0)
        col = lax.broadcasted_iota(jnp.int32, (t_new, page), 1)
        bias_new = jnp.concatenate([
            cq[h] - jnp.sum(jnp.where(row == col, cq[h], 0.0), axis=0, keepdims=True) for h in range(h_b)],
            axis=0)
        vf_new = [_pad_rows(vnf[:, h * dh:(h + 1) * dh], page).astype(BF16) for h in range(h_b)]
        _softmax_step_stacked(_causal_new(sf_new + bias_new, t_new), vf_new, mf_ref, lf_ref, accf_ref, t_new)

        lam = _diff_lambda(lq1, lk1, lq2, lk2, lam_init)
        o = accd_ref[...] / ld_ref[...]
        for h in range(h_a):
            r0 = 2 * h * t_new
            oh = o[r0:r0 + t_new, :] - lam * o[r0 + t_new:r0 + 2 * t_new, :]
            od_ref[:, h * dv:(h + 1) * dv] = _rms(oh, sg_ref[...], SUBLN_EPS) * (1.0 - lam_init)
        o = accf_ref[...] / lf_ref[...]
        for h in range(h_b):
            of_ref[:, h * dh:(h + 1) * dh] = o[h * t_new:(h + 1) * t_new, :]


def _sample_attn_call(page_table, proj, logf, caches, lams, sg, cpast, ctot, layer, *, n_prompt, t_new,
                      h_a, dk, dv, h_b, dh, page, cols, lam_init, n_group):
    bs, n_pages = page_table.shape
    wd, wf = h_a * dv, h_b * dh
    row0 = n_prompt // t_new
    kern = functools.partial(_sample_attn_kernel, h_a=h_a, dk=dk, dv=dv, h_b=h_b, dh=dh, page=page,
                             n_pages=n_pages, n_group=n_group, lam_init=lam_init)
    new_rows = lambda w, col: pl.BlockSpec((t_new, w), lambda b, p, pt: (row0 + b, col // w))
    const = lambda shape: pl.BlockSpec(shape, lambda b, p, pt: (0,) * len(shape))

    def pages(cache):
        rows = cache.shape[2]
        return [pl.BlockSpec((None, None, rows, LANES),
                             lambda b, p, pt, i=i: (layer, pt[b, p * n_group + i], 0, 0)) for i in range(n_group)]

    grid_spec = pltpu.PrefetchScalarGridSpec(
        num_scalar_prefetch=1,
        grid=(bs, n_pages // n_group),
        in_specs=[
            new_rows(wd, cols[0]), new_rows(wd, cols[1]), new_rows(wd, cols[2]),
            new_rows(wf, cols[3]), new_rows(wf, cols[4]), new_rows(wf, cols[5]),
            pl.BlockSpec((t_new, LANES), lambda b, p, pt: (row0 + b, 0)),
            *pages(caches[0]), *pages(caches[1]), *pages(caches[2]), *pages(caches[3]),
            const((1, dk)), const((1, dk)), const((1, dk)), const((1, dk)), const((1, dv)),
            pl.BlockSpec((1, h_b * n_pages, page), lambda b, p, pt: (b, 0, 0)),
            pl.BlockSpec((1, 1, LANES), lambda b, p, pt: (b, 0, 0)),
        ],
        out_specs=[pl.BlockSpec((t_new, wd), lambda b, p, pt: (b, 0)),
                   pl.BlockSpec((t_new, wf), lambda b, p, pt: (b, 0))],
        scratch_shapes=[
            pltpu.VMEM((2 * h_a * t_new, 1), F32), pltpu.VMEM((2 * h_a * t_new, 1), F32),
            pltpu.VMEM((2 * h_a * t_new, dv), F32),
            pltpu.VMEM((h_b * t_new, 1), F32), pltpu.VMEM((h_b * t_new, 1), F32),
            pltpu.VMEM((h_b * t_new, dh), F32), pltpu.VMEM((t_new, LANES), F32),
        ],
    )
    args = [proj] * 6 + [logf]
    for cache in caches:
        args += [cache] * n_group
    return pl.pallas_call(
        kern,
        grid_spec=grid_spec,
        out_shape=[jax.ShapeDtypeStruct((bs * t_new, wd), F32), jax.ShapeDtypeStruct((bs * t_new, wf), F32)],
        compiler_params=_params(("parallel", "arbitrary")),
        name="sample_attn",
    )(page_table, *args, *lams, sg, cpast, ctot)


def _lru_coeffs(x, xprev_ext, gate_w, cw_ref, cb_ref, wga_ref, bga_ref, wgx_ref, bgx_ref, lam_ref, conv_w):
    rows = x.shape[0]
    nb = wga_ref.shape[0]
    bw = wga_ref.shape[1]
    u = xprev_ext(conv_w - 1) * cw_ref[0:1, :]
    for k in range(1, conv_w - 1):
        u = u + xprev_ext(conv_w - 1 - k) * cw_ref[k:k + 1, :]
    u = u + x * cw_ref[conv_w - 1:conv_w, :]
    u = u + cb_ref[...]
    ub = u.astype(BF16)
    r_parts, i_parts = [], []
    for n in range(nb):
        un = ub[:, n * bw:(n + 1) * bw]
        r_parts.append(jnp.dot(un, wga_ref[n], preferred_element_type=F32))
        i_parts.append(jnp.dot(un, wgx_ref[n], preferred_element_type=F32))
    r = _sigmoid(jnp.concatenate(r_parts, axis=-1) + bga_ref[...])
    i = _sigmoid(jnp.concatenate(i_parts, axis=-1) + bgx_ref[...])
    log_a = -LRU_C * r * _softplus(-lam_ref[...])
    a = jnp.exp(log_a)
    b = jnp.sqrt(-jnp.tanh(log_a) * (a * a + 1.0)) * (i * u)
    del rows, gate_w
    return a, b


def _group_scan(a, b):
    row = lax.broadcasted_iota(jnp.int32, a.shape, 0) % SUBLANES
    k = 1
    while k < SUBLANES:
        keep = row >= k
        a_sh = jnp.where(keep, pltpu.roll(a, k, 0), 1.0)
        b_sh = jnp.where(keep, pltpu.roll(b, k, 0), 0.0)
        b = a * b_sh + b
        a = a * a_sh
        k *= 2
    return a, b


def _gelu_tanh(x):
    return 0.5 * x * (1.0 + jnp.tanh(math.sqrt(2.0 / math.pi) * (x + 0.044715 * (x * x * x))))


def _lru_prompt_kernel(x_ref, g_ref, cw_ref, cb_ref, wga_ref, bga_ref, wgx_ref, bgx_ref, lam_ref,
                       y_ref, ht_ref, tail_ref, hl_ref, a_ref, b_ref, hs_ref, *, conv_w):
    ti = pl.program_id(1)

    @pl.when(ti == 0)
    def _():
        tail_ref[...] = jnp.zeros(tail_ref.shape, F32)
        hl_ref[...] = jnp.zeros(hl_ref.shape, F32)

    x = x_ref[...]
    tt = x.shape[0]
    xe = jnp.concatenate([tail_ref[...], x], axis=0)
    delayed = lambda k: pltpu.roll(xe, k, 0)[SUBLANES:, :]
    a, b = _lru_coeffs(x, delayed, None, cw_ref, cb_ref, wga_ref, bga_ref, wgx_ref, bgx_ref, lam_ref, conv_w)
    a, b = _group_scan(a, b)
    a_ref[...] = a
    b_ref[...] = b
    tail_ref[...] = x[tt - SUBLANES:, :]

    def body(g, hlast):
        r0 = pl.multiple_of(g * SUBLANES, SUBLANES)
        hg = b_ref[pl.ds(r0, SUBLANES), :] + a_ref[pl.ds(r0, SUBLANES), :] * hlast
        hs_ref[pl.ds(r0, SUBLANES), :] = hg
        return hg[SUBLANES - 1:SUBLANES, :]

    hl = lax.fori_loop(0, tt // SUBLANES, body, hl_ref[...])
    hl_ref[...] = hl
    ht_ref[0] = hl
    y_ref[...] = (_gelu_tanh(g_ref[...]) * hs_ref[...]).astype(y_ref.dtype)


def _lru_prompt_call(proj, wts, *, batch, seq, wc, tt, g_col, x_col):
    cw, cb, wga, bga, wgx, bgx, lam = wts
    nt = seq // tt
    conv_w = cw.shape[0]
    full = lambda a: pl.BlockSpec(a.shape, lambda b, t: (0,) * a.ndim)
    kern = functools.partial(_lru_prompt_kernel, conv_w=conv_w)
    return pl.pallas_call(
        kern,
        grid=(batch, nt),
        in_specs=[
            pl.BlockSpec((tt, wc), lambda b, t: (b * nt + t, x_col // wc)),
            pl.BlockSpec((tt, wc), lambda b, t: (b * nt + t, g_col // wc)),
            full(cw), full(cb), full(wga), full(bga), full(wgx), full(bgx), full(lam),
        ],
        out_specs=[pl.BlockSpec((tt, wc), lambda b, t: (b * nt + t, 0)),
                   pl.BlockSpec((1, 1, wc), lambda b, t: (b, 0, 0))],
        out_shape=[jax.ShapeDtypeStruct((batch * seq, wc), BF16),
                   jax.ShapeDtypeStruct((batch, 1, wc), F32)],
        scratch_shapes=[pltpu.VMEM((SUBLANES, wc), F32), pltpu.VMEM((1, wc), F32),
                        pltpu.VMEM((tt, wc), F32), pltpu.VMEM((tt, wc), F32), pltpu.VMEM((tt, wc), F32)],
        compiler_params=_params(("parallel", "arbitrary")),
        name="lru_prompt",
    )(proj, proj, cw, cb, wga, bga, wgx, bgx, lam)


def _lru_sample_kernel(x_ref, g_ref, prev_ref, h0_ref, cw_ref, cb_ref, wga_ref, bga_ref, wgx_ref, bgx_ref,
                       lam_ref, y_ref, hs_ref, *, conv_w):
    x = x_ref[...]
    rows = x.shape[0]
    row = lax.broadcasted_iota(jnp.int32, x.shape, 0) % SUBLANES
    prev = prev_ref[...]

    def delayed(k):
        return jnp.where(row >= k, pltpu.roll(x, k, 0), pltpu.roll(prev, (k - SUBLANES) % rows, 0))

    a, b = _lru_coeffs(x, delayed, None, cw_ref, cb_ref, wga_ref, bga_ref, wgx_ref, bgx_ref, lam_ref, conv_w)
    a, b = _group_scan(a, b)
    hs = b + a * h0_ref[...]
    hs_ref[...] = hs
    y_ref[...] = _gelu_tanh(g_ref[...]) * hs


def _lru_sample_call(proj, prev8, h0rep, wts, *, n_prompt, n_sample, wc, g_col, x_col):
    cw, cb, wga, bga, wgx, bgx, lam = wts
    conv_w = cw.shape[0]
    rb = n_prompt // n_sample
    full = lambda a: pl.BlockSpec(a.shape, lambda i: (0,) * a.ndim)
    kern = functools.partial(_lru_sample_kernel, conv_w=conv_w)
    return pl.pallas_call(
        kern,
        grid=(1,),
        in_specs=[
            pl.BlockSpec((n_sample, wc), lambda i: (rb, x_col // wc)),
            pl.BlockSpec((n_sample, wc), lambda i: (rb, g_col // wc)),
            full(prev8), full(h0rep),
            full(cw), full(cb), full(wga), full(bga), full(wgx), full(bgx), full(lam),
        ],
        out_specs=[pl.BlockSpec((n_sample, wc), lambda i: (0, 0)),
                   pl.BlockSpec((n_sample, wc), lambda i: (0, 0))],
        out_shape=[jax.ShapeDtypeStruct((n_sample, wc), F32), jax.ShapeDtypeStruct((n_sample, wc), F32)],
        compiler_params=_params(("arbitrary",)),
        name="lru_sample",
    )(proj, proj, prev8, h0rep, cw, cb, wga, bga, wgx, bgx, lam)


def _mix_kernel(ap_ref, bp_ref, cp_ref, as_ref, bs_ref, cs_ref, w_ref, x_ref, g_ref, *rest,
                n_experts, n_prompt_tiles):
    if n_experts:
        wrh_ref, wrl_ref, x1_ref, rg_ref, re_ref = rest
    else:
        x1_ref, hn_ref = rest
    i = pl.program_id(0)
    wa, wb = ap_ref.shape[1], bp_ref.shape[1]

    def project(a_ref, b_ref, c_ref):
        d = lambda v, lo, hi: jnp.dot(v[...].astype(BF16), w_ref[lo:hi, :], preferred_element_type=F32)
        return d(a_ref, 0, wa) + d(b_ref, wa, wa + wb) + d(c_ref, wa + wb, w_ref.shape[0])

    @pl.when(i < n_prompt_tiles)
    def _():
        x1_ref[...] = x_ref[...] + project(ap_ref, bp_ref, cp_ref)

    @pl.when(i >= n_prompt_tiles)
    def _():
        x1_ref[...] = x_ref[...] + project(as_ref, bs_ref, cs_ref)

    hn = _rms(x1_ref[...], g_ref[...], RMS_EPS)
    if not n_experts:
        hn_ref[...] = hn.astype(BF16)
    else:
        h_hi = hn.astype(BF16)
        h_lo = (hn - h_hi.astype(F32)).astype(BF16)
        d = lambda a, b: jnp.dot(a, b[...], preferred_element_type=F32)
        logits = d(h_hi, wrh_ref) + (d(h_lo, wrh_ref) + d(h_hi, wrl_ref))
        lane = lax.broadcasted_iota(jnp.int32, logits.shape, 1)
        lane_f = lane.astype(F32)
        big = float(LANES)
        lg = jnp.where(lane < n_experts, logits, NEG_INF)
        v1 = jnp.max(lg, axis=-1, keepdims=True)
        i1 = jnp.min(jnp.where(lg == v1, lane_f, big), axis=-1, keepdims=True)
        lg2 = jnp.where(lane_f == i1, NEG_INF, lg)
        v2 = jnp.max(lg2, axis=-1, keepdims=True)
        i2 = jnp.min(jnp.where(lg2 == v2, lane_f, big), axis=-1, keepdims=True)
        e = jnp.exp(v2 - v1)
        g1 = 1.0 / (1.0 + e)
        g2 = e / (1.0 + e)
        rg_ref[...] = jnp.where(lane == 0, g1, jnp.where(lane == 1, g2, 0.0))
        re_ref[...] = jnp.where(lane == 0, i1, jnp.where(lane == 1, i2, 0.0)).astype(jnp.int32)


def _mix_call(branches_p, branches_s, w_out, x, g, w_router, *, tm):
    n, d = x.shape
    npt = branches_p[0].shape[0] // tm
    n_experts = 0 if w_router is None else w_router.shape[1]
    kern = functools.partial(_mix_kernel, n_experts=n_experts, n_prompt_tiles=npt)
    row = lambda c: pl.BlockSpec((tm, c), lambda i: (i, 0))
    row_p = lambda a: pl.BlockSpec((tm, a.shape[1]), lambda i: (jnp.minimum(i, npt - 1), 0))
    row_s = lambda a: pl.BlockSpec((tm, a.shape[1]), lambda i: (jnp.maximum(i - npt, 0), 0))
    in_specs = ([row_p(a) for a in branches_p] + [row_s(a) for a in branches_s]
                + [pl.BlockSpec((d, d), lambda i: (0, 0)), row(d), pl.BlockSpec((1, d), lambda i: (0, 0))])
    args = [*branches_p, *branches_s, w_out, x, g]
    if n_experts:
        wr = jnp.zeros((d, LANES), F32).at[:, :n_experts].set(w_router)
        wr_hi = wr.astype(BF16)
        wr_lo = (wr - wr_hi.astype(F32)).astype(BF16)
        in_specs += [pl.BlockSpec((d, LANES), lambda i: (0, 0))] * 2
        out_specs = [row(d), row(LANES), row(LANES)]
        out_shape = [jax.ShapeDtypeStruct((n, d), F32), jax.ShapeDtypeStruct((n, LANES), F32),
                     jax.ShapeDtypeStruct((n, LANES), jnp.int32)]
        args += [wr_hi, wr_lo]
    else:
        out_specs = [row(d), row(d)]
        out_shape = [jax.ShapeDtypeStruct((n, d), F32), jax.ShapeDtypeStruct((n, d), BF16)]
    return pl.pallas_call(
        kern,
        grid=(n // tm,),
        in_specs=in_specs,
        out_specs=out_specs,
        out_shape=out_shape,
        compiler_params=_params(("parallel",)),
        name="mix_router" if n_experts else "mix",
    )(*args)


def _swiglu_acc(h, wg, wu, wd):
    g = jnp.dot(h, wg.astype(BF16), preferred_element_type=F32)
    u = jnp.dot(h, wu.astype(BF16), preferred_element_type=F32)
    a = (g * _sigmoid(g) * u).astype(BF16)
    return jnp.dot(a, wd.astype(BF16), preferred_element_type=F32)


def _ffn_kernel(h_ref, wg_ref, wu_ref, wd_ref, x_ref, o_ref):
    j = pl.program_id(1)

    @pl.when(j == 0)
    def _():
        o_ref[...] = x_ref[...]

    o_ref[...] += _swiglu_acc(h_ref[...], wg_ref[...], wu_ref[...], wd_ref[...])


def _ffn_call(hn, wg, wu, wd, m, x1, *, tm, tf):
    n, d = x1.shape
    f = wg.shape[2]
    return pl.pallas_call(
        _ffn_kernel,
        grid=(n // tm, f // tf),
        in_specs=[
            pl.BlockSpec((tm, d), lambda i, j: (i, 0)),
            pl.BlockSpec((None, d, tf), lambda i, j: (m, 0, j)),
            pl.BlockSpec((None, d, tf), lambda i, j: (m, 0, j)),
            pl.BlockSpec((None, tf, d), lambda i, j: (m, j, 0)),
            pl.BlockSpec((tm, d), lambda i, j: (i, 0), pipeline_mode=pl.Buffered(1)),
        ],
        out_specs=pl.BlockSpec((tm, d), lambda i, j: (i, 0)),
        out_shape=jax.ShapeDtypeStruct((n, d), F32),
        compiler_params=_params(("parallel", "arbitrary")),
        name="ffn_dense",
    )(hn, wg, wu, wd, x1)


def _gather_norm_kernel(rt_ref, nv_ref, x_hbm, g_ref, o_ref, buf_ref, sem, *, tg):
    t = pl.program_id(0)

    def row_copy(r, tok):
        return pltpu.make_async_copy(x_hbm.at[pl.ds(tok, 1), :], buf_ref.at[pl.ds(r, 1), :], sem)

    @pl.when(t < nv_ref[0])
    def _():
        def issue(r2, carry):
            for prio in range(DMA_PRIORITIES):
                r = r2 * DMA_PRIORITIES + prio
                row_copy(r, rt_ref[t * tg + r]).start(priority=prio)
            return carry

        lax.fori_loop(0, tg // DMA_PRIORITIES, issue, 0, unroll=4)
        pltpu.make_async_copy(buf_ref, buf_ref, sem).wait()
        o_ref[...] = _rms(buf_ref[...], g_ref[...], RMS_EPS).astype(BF16)

    @pl.when(t >= nv_ref[0])
    def _():
        o_ref[...] = jnp.zeros(o_ref.shape, o_ref.dtype)


def _gather_norm_call(row_token, n_valid, x1, g, *, tg):
    r = row_token.shape[0]
    d = x1.shape[1]
    grid_spec = pltpu.PrefetchScalarGridSpec(
        num_scalar_prefetch=2,
        grid=(r // tg,),
        in_specs=[pl.BlockSpec(memory_space=pl.ANY), pl.BlockSpec((1, d), lambda t, rt, nv: (0, 0))],
        out_specs=pl.BlockSpec((tg, d), lambda t, rt, nv: (t, 0)),
        scratch_shapes=[pltpu.VMEM((tg, d), F32), pltpu.SemaphoreType.DMA(())],
    )
    return pl.pallas_call(
        functools.partial(_gather_norm_kernel, tg=tg),
        grid_spec=grid_spec,
        out_shape=jax.ShapeDtypeStruct((r, d), BF16),
        compiler_params=_params(("arbitrary",)),
        name="moe_gather",
    )(row_token, n_valid, x1, g)


def _moe_ffn_kernel(te_ref, nv_ref, h_ref, wg_ref, wu_ref, wd_ref, o_ref):
    del te_ref
    t = pl.program_id(0)
    j = pl.program_id(1)

    @pl.when(j == 0)
    def _():
        o_ref[...] = jnp.zeros(o_ref.shape, F32)

    @pl.when(t < nv_ref[0])
    def _():
        o_ref[...] += _swiglu_acc(h_ref[...], wg_ref[...], wu_ref[...], wd_ref[...])


def _moe_ffn_call(tile_expert, n_valid, xs, wg, wu, wd, m, *, tm, tf):
    r, d = xs.shape
    f = wg.shape[3]
    nj = f // tf

    def tile(t, nv):
        return jnp.minimum(t, nv[0] - 1)

    def fblk(t, j, nv):
        return jnp.where(t < nv[0], j, nj - 1)

    grid_spec = pltpu.PrefetchScalarGridSpec(
        num_scalar_prefetch=2,
        grid=(r // tm, nj),
        in_specs=[
            pl.BlockSpec((tm, d), lambda t, j, te, nv: (tile(t, nv), 0)),
            pl.BlockSpec((None, None, d, tf), lambda t, j, te, nv: (m, te[t], 0, fblk(t, j, nv))),
            pl.BlockSpec((None, None, d, tf), lambda t, j, te, nv: (m, te[t], 0, fblk(t, j, nv))),
            pl.BlockSpec((None, None, tf, d), lambda t, j, te, nv: (m, te[t], fblk(t, j, nv), 0)),
        ],
        out_specs=pl.BlockSpec((tm, d), lambda t, j, te, nv: (t, 0)),
    )
    return pl.pallas_call(
        _moe_ffn_kernel,
        grid_spec=grid_spec,
        out_shape=jax.ShapeDtypeStruct((r, d), F32),
        compiler_params=_params(("arbitrary", "arbitrary")),
        name="moe_ffn",
    )(tile_expert, n_valid, xs, wg, wu, wd)


def _combine_norm_kernel(pos_ref, x_ref, rg_ref, ys_hbm, g_ref, *rest, tc, final_norm, n_prompt_tiles):
    if final_norm:
        op_ref, os_ref, buf_ref, sem = rest
    else:
        o_ref, buf_ref, sem = rest
    i = pl.program_id(0)

    def row_copy(k, r, src):
        return pltpu.make_async_copy(ys_hbm.at[pl.ds(src, 1), :], buf_ref.at[k, pl.ds(r, 1), :], sem)

    def issue(r, carry):
        for k in range(TOP_K):
            row_copy(k, r, pos_ref[(i * tc + r) * TOP_K + k]).start(priority=k % DMA_PRIORITIES)
        return carry

    lax.fori_loop(0, tc, issue, 0, unroll=4)
    pltpu.make_async_copy(buf_ref, buf_ref, sem).wait()
    rg = rg_ref[...]
    moe = rg[:, 0:1] * buf_ref[0]
    for k in range(1, TOP_K):
        moe = moe + rg[:, k:k + 1] * buf_ref[k]
    x2 = x_ref[...] + moe
    if final_norm:
        y = _rms(x2, g_ref[...], RMS_EPS)

        @pl.when(i < n_prompt_tiles)
        def _():
            op_ref[...] = y

        @pl.when(i >= n_prompt_tiles)
        def _():
            os_ref[...] = y
    else:
        o_ref[...] = x2


def _combine_norm_call(pos, x1, rg, ys, g, *, tc, final_norm, n_prompt):
    n, d = x1.shape
    npt = n_prompt // tc
    if final_norm:
        out_specs = [pl.BlockSpec((tc, d), lambda i, ps: (jnp.minimum(i, npt - 1), 0)),
                     pl.BlockSpec((tc, d), lambda i, ps: (jnp.maximum(i - npt, 0), 0))]
        out_shape = [jax.ShapeDtypeStruct((n_prompt, d), F32), jax.ShapeDtypeStruct((n - n_prompt, d), F32)]
    else:
        out_specs = pl.BlockSpec((tc, d), lambda i, ps: (i, 0))
        out_shape = jax.ShapeDtypeStruct((n, d), F32)
    grid_spec = pltpu.PrefetchScalarGridSpec(
        num_scalar_prefetch=1,
        grid=(n // tc,),
        in_specs=[
            pl.BlockSpec((tc, d), lambda i, ps: (i, 0)),
            pl.BlockSpec((tc, LANES), lambda i, ps: (i, 0)),
            pl.BlockSpec(memory_space=pl.ANY),
            pl.BlockSpec((1, d), lambda i, ps: (0, 0)),
        ],
        out_specs=out_specs,
        scratch_shapes=[pltpu.VMEM((TOP_K, tc, d), F32), pltpu.SemaphoreType.DMA(())],
    )
    return pl.pallas_call(
        functools.partial(_combine_norm_kernel, tc=tc, final_norm=final_norm, n_prompt_tiles=npt),
        grid_spec=grid_spec,
        out_shape=out_shape,
        compiler_params=_params(("arbitrary",)),
        name="moe_combine",
    )(pos, x1, rg, ys, g)


def _norm_kernel(x_ref, g_ref, o_ref):
    o_ref[...] = _rms(x_ref[...], g_ref[...], RMS_EPS)


def _norm_call(x, g, *, tm):
    n, d = x.shape
    return pl.pallas_call(
        _norm_kernel,
        grid=(n // tm,),
        in_specs=[pl.BlockSpec((tm, d), lambda i: (i, 0)), pl.BlockSpec((1, d), lambda i: (0, 0))],
        out_specs=pl.BlockSpec((tm, d), lambda i: (i, 0)),
        out_shape=jax.ShapeDtypeStruct((n, d), F32),
        compiler_params=_params(("parallel",)),
        name="final_norm",
    )(x, g)


def _route_plan(re, n_experts, tm):
    n = re.shape[0]
    e_flat = re.reshape(-1)
    oh = (e_flat[:, None] == jnp.arange(n_experts, dtype=jnp.int32)[None, :]).astype(jnp.int32)
    csum = jnp.cumsum(oh, axis=0)
    rank = jnp.sum((csum - oh) * oh, axis=1)
    counts = csum[-1]
    padded = ((counts + tm - 1) // tm) * tm
    ends = jnp.cumsum(padded)
    offs = ends - padded
    pos = jnp.sum(oh * offs[None, :], axis=1) + rank
    n_rows = ((n * TOP_K + tm - 1) // tm) * tm + n_experts * tm
    row_token = jnp.zeros((n_rows,), jnp.int32).at[pos].set(jnp.arange(n * TOP_K, dtype=jnp.int32) // TOP_K)
    n_tiles = n_rows // tm
    n_valid = (ends[-1] // tm).astype(jnp.int32)
    tile_start = jnp.arange(n_tiles, dtype=jnp.int32) * tm
    tile_e = jnp.sum((tile_start[:, None] >= ends[None, :]).astype(jnp.int32), axis=1)
    last_e = jnp.sum((tile_start[n_valid - 1] >= ends).astype(jnp.int32))
    tile_e = jnp.where(jnp.arange(n_tiles) < n_valid, tile_e, last_e).astype(jnp.int32)
    return pos.astype(jnp.int32), row_token, tile_e, n_valid.reshape(1)


def _tile(pref, *dims):
    t = pref
    for dim in dims:
        t = math.gcd(t, dim)
    return t


def _rope_tables(seq, n_past, t_new, bs, dk, rot_dim):
    half = rot_dim // 2
    inv = ROPE_THETA ** (-(jnp.arange(half, dtype=F32) * 2.0) / rot_dim)
    pos = jnp.concatenate([jnp.arange(seq), jnp.tile(n_past + jnp.arange(t_new), bs)])
    ang = pos.astype(F32)[:, None] * inv[None, :]
    cos, sin = jnp.cos(ang), jnp.sin(ang)
    n = pos.shape[0]
    ones = jnp.ones((n, dk - rot_dim), F32)
    zeros_h = jnp.zeros((n, half), F32)
    zeros_r = jnp.zeros((n, dk - rot_dim), F32)
    cos_t = jnp.concatenate([cos, cos, ones], axis=1)
    sa_t = jnp.concatenate([zeros_h, sin, zeros_r], axis=1)
    sb_t = jnp.concatenate([-sin, zeros_h, zeros_r], axis=1)
    return cos_t, sa_t, sb_t


def _page_mats(heads, page, n_pages, nb):
    w = heads * page
    dst = jnp.arange(w)
    d_head, d_slot = dst // page, dst % page
    same = d_head[:, None] == d_head[None, :]
    mcs = (same & (d_slot[:, None] <= d_slot[None, :])).astype(BF16)
    mbc = ((d_head[:, None] == d_head[None, :]) & (d_slot[:, None] == page - 1)).astype(BF16)
    msel = ((d_slot[:, None] == page - 1) & (d_head[:, None] == jnp.arange(LANES)[None, :])).astype(BF16)
    r = jnp.arange(nb * n_pages)
    lst = ((r[None, :] // n_pages == r[:, None] // n_pages) & (r[None, :] < r[:, None])).astype(BF16)
    return mcs, mbc, msel, lst


def kernel(x_prompt, x_sample, cache_k_diff, cache_v_diff, cache_k_fox, cache_v_fox, cache_logf_fox, state_rglru_h, state_rglru_conv, page_table, norm_mix_g, w_in, b_fgate, lambda_q1, lambda_k1, lambda_q2, lambda_k2, subln_g, conv_w, conv_b, w_gate_a, b_gate_a, w_gate_x, b_gate_x, lru_lambda, w_out, norm_ffn_g, w_dense_gate, w_dense_up, w_dense_down, w_router, w_moe_gate, w_moe_up, w_moe_down, norm_final_g):
    bp, seq, d = x_prompt.shape
    bs, t_new, _ = x_sample.shape
    depth = w_in.shape[0]
    n_pool, page, h_a = cache_k_diff.shape[1], cache_k_diff.shape[2], cache_k_diff.shape[3]
    dk_a = cache_k_diff.shape[5]
    dv_a = cache_v_diff.shape[4]
    h_b, dh_b = cache_k_fox.shape[3], cache_k_fox.shape[4]
    w_c = state_rglru_h.shape[2]
    cw_len = conv_w.shape[1]
    n_pages = page_table.shape[1]
    n_past = n_pages * page
    rot_dim = dk_a // 4
    n_experts = w_router.shape[2]
    n_p, n_s = bp * seq, bs * t_new
    n = n_p + n_s
    assert t_new == SUBLANES and page == LANES and dv_a == 2 * dk_a and dk_a == LANES and dh_b == LANES

    w_qa = h_a * 2 * dk_a
    w_va = h_a * dv_a
    w_b = h_b * dh_b
    col_qa, col_ka, col_va = 0, w_qa, 2 * w_qa
    col_qb = col_va + w_va
    col_kb, col_vb = col_qb + w_b, col_qb + 2 * w_b
    col_g = col_vb + w_b
    col_x = col_g + w_c
    n_main = col_x + w_c
    src_fl = 2 * w_qa + w_va + 3 * w_b

    tm = _tile(TM_TOKENS, seq, n_s)
    tn = _tile(TN_PROJ, w_qa, src_fl, n_main - src_fl)
    tq = _tile(TQ_ATTN, seq)
    tt = _tile(TT_LRU, seq)
    tm_mix = _tile(TM_MIX, tm)
    tf = _tile(TF_FFN, w_dense_gate.shape[2])
    tm_moe = _tile(TM_MOE, n)
    tc = _tile(TC_COMBINE, n)

    rope = _rope_tables(seq, n_past, t_new, bs, dk_a, rot_dim)
    tri = jnp.tril(jnp.ones((seq, seq), BF16))
    page_mats = _page_mats(h_b, page, n_pages, _tile(LOGF_BATCH, bs))

    kc_d = cache_k_diff.reshape(depth, n_pool, page * h_a * 2, dk_a)
    nc_a = dv_a // LANES
    vc_d = jnp.swapaxes(cache_v_diff.reshape(depth, n_pool, page, h_a, nc_a, LANES), 3, 4).reshape(
        depth, n_pool, page * nc_a * h_a, LANES)
    kc_f = cache_k_fox.reshape(depth, n_pool, page * h_b, dh_b)
    vc_f = cache_v_fox.reshape(depth, n_pool, page * h_b, dh_b)
    lf_pages = jnp.swapaxes(cache_logf_fox, 2, 3).reshape(depth, n_pool, h_b * page)
    n_group = _tile(PAGES_PER_STEP, n_pages)

    x = jnp.concatenate([x_prompt.reshape(n_p, d), x_sample.reshape(n_s, d)], axis=0)
    row2 = lambda v: v.reshape(1, -1).astype(F32)
    st = {k: [] for k in ('kd', 'vd', 'kf', 'vf', 'lf', 'h_p', 'h_s', 'cv')}
    y_p = y_s = None
    cache_a = cache_b = None

    for l in range(depth):
        lam_init = 0.8 - 0.6 * math.exp(-0.3 * l)
        w_l = w_in[l]
        w_b2 = w_l[:, src_fl + h_b:].astype(BF16)
        b_fl = jnp.zeros((1, LANES), F32).at[0, :h_b].set(b_fgate[l])
        proj, logf = _proj_call(
            x, row2(norm_mix_g[l]), w_in, l, src_fl, w_b2, b_fl, rope, tm=tm, tn=tn, n_rope_cols=2 * w_qa,
            n_prompt_tiles=n_p // tm, rope_prompt_tiles=seq // tm, rot_half=rot_dim // 2)

        lams = (row2(lambda_q1[l]), row2(lambda_k1[l]), row2(lambda_q2[l]), row2(lambda_k2[l]))
        sg = row2(subln_g[l])

        oa_p, *cache_a = _diff_prompt_call(proj, lams, sg, cache_a, l, depth, batch=bp, seq=seq, heads=h_a,
                                           dk=dk_a, dv=dv_a, tq=tq, k_col=col_ka, v_col=col_va,
                                           lam_init=lam_init)
        c_col = _cumsum_call(logf, tri, bp, seq)
        c_row = jnp.swapaxes(c_col[:, :h_b].reshape(bp, seq, h_b), 1, 2).reshape(bp * h_b, seq // tq, tq)
        ob_p, *cache_b = _fox_prompt_call(proj, c_col, c_row, cache_b, l, depth, batch=bp, seq=seq, heads=h_b,
                                          dh=dh_b, tq=tq, q_col=col_qb, k_col=col_kb, v_col=col_vb)
        lru_w = (conv_w[l], row2(conv_b[l]), w_gate_a[l].astype(BF16), row2(b_gate_a[l]),
                 w_gate_x[l].astype(BF16), row2(b_gate_x[l]), row2(lru_lambda[l]))
        oc_p, ht_p = _lru_prompt_call(proj, lru_w, batch=bp, seq=seq, wc=w_c, tt=tt, g_col=col_g, x_col=col_x)

        cpast, ctot = _logf_pages_call(page_table, lf_pages[l], page_mats, heads=h_b, page=page)
        oa_s, ob_s = _sample_attn_call(
            page_table, proj, logf, (kc_d, vc_d, kc_f, vc_f), lams, sg, cpast, ctot, l, n_prompt=n_p,
            t_new=t_new, h_a=h_a, dk=dk_a, dv=dv_a, h_b=h_b, dh=dh_b, page=page,
            cols=(col_qa, col_ka, col_va, col_qb, col_kb, col_vb), lam_init=lam_init, n_group=n_group)
        prev8 = jnp.pad(state_rglru_conv[l], ((0, 0), (SUBLANES - (cw_len - 1), 0), (0, 0))).reshape(n_s, w_c)
        h0rep = jnp.repeat(state_rglru_h[l], t_new, axis=0)
        oc_s, hs_s = _lru_sample_call(proj, prev8, h0rep, lru_w, n_prompt=n_p, n_sample=n_s, wc=w_c,
                                      g_col=col_g, x_col=col_x)

        br_p, br_s = (oa_p, ob_p, oc_p), (oa_s, ob_s, oc_s)
        g_ffn = row2(norm_ffn_g[l])
        m = l // 2
        if l % 2 == 0:
            x1, hn = _mix_call(br_p, br_s, w_out[l].astype(BF16), x, g_ffn, None, tm=tm_mix)
            x = _ffn_call(hn, w_dense_gate, w_dense_up, w_dense_down, m, x1, tm=tm, tf=tf)
            if l == depth - 1:
                y_all = _norm_call(x, row2(norm_final_g), tm=tm_mix)
                y_p, y_s = y_all[:n_p], y_all[n_p:]
        else:
            x1, rg, re = _mix_call(br_p, br_s, w_out[l].astype(BF16), x, g_ffn, w_router[m], tm=tm_mix)
            pos, row_token, tile_e, n_valid = _route_plan(re[:, :TOP_K], n_experts, tm_moe)
            xs = _gather_norm_call(row_token, n_valid, x1, g_ffn, tg=tm_moe)
            ys = _moe_ffn_call(tile_e, n_valid, xs, w_moe_gate, w_moe_up, w_moe_down, m, tm=tm_moe,
                               tf=_tile(TF_MOE, w_moe_gate.shape[3]))
            final = l == depth - 1
            x = _combine_norm_call(pos, x1, rg, ys, row2(norm_final_g), tc=tc, final_norm=final, n_prompt=n_p)
            if final:
                y_p, y_s = x

        st['kd'].append(proj[n_p:, col_ka:col_ka + w_qa])
        st['vd'].append(proj[n_p:, col_va:col_va + w_va])
        st['kf'].append(proj[n_p:, col_kb:col_kb + w_b])
        st['vf'].append(proj[n_p:, col_vb:col_vb + w_b])
        st['lf'].append(logf[:, :h_b])
        st['h_p'].append(ht_p.reshape(bp, w_c))
        st['h_s'].append(hs_s.reshape(bs, t_new, w_c)[:, t_new - 1])
        st['cv'].append(proj[:, col_x:col_x + w_c])

    def both(name, shape_tail):
        a = jnp.stack(st[name])
        return (a[:, :n_p].reshape((depth, bp, seq) + shape_tail),
                a[:, n_p:].reshape((depth, bs, t_new) + shape_tail))

    sample = lambda name, tail: jnp.stack(st[name]).reshape((depth, bs, t_new) + tail)
    kd_s, vd_s = sample('kd', (h_a, 2, dk_a)), sample('vd', (h_a, dv_a))
    kf_s, vf_s = sample('kf', (h_b, dh_b)), sample('vf', (h_b, dh_b))
    kd_p = cache_a[0].reshape(depth, bp, seq, h_a, 2, dk_a)
    vd_p = jnp.swapaxes(cache_a[1].reshape(depth, bp, seq, nc_a, h_a, LANES), 3, 4).reshape(
        depth, bp, seq, h_a, dv_a)
    kf_p = cache_b[0].reshape(depth, bp, seq, h_b, dh_b)
    vf_p = cache_b[1].reshape(depth, bp, seq, h_b, dh_b)
    lf_p, lf_s = both('lf', (h_b,))
    cv_p, cv_s = both('cv', (w_c,))
    keep = cw_len - 1
    return (y_p.reshape(bp, seq, d), y_s.reshape(bs, t_new, d),
            kd_p, vd_p, kf_p, vf_p, lf_p, jnp.stack(st['h_p']), cv_p[:, :, seq - keep:],
            kd_s, vd_s, kf_s, vf_s, lf_s, jnp.stack(st['h_s']), cv_s[:, :, t_new - keep:])
```
